```python
import jax, jax.numpy as jnp
from jax import lax
import numpy as np

D_MODEL = 1024
BATCH = 16
SEQ = 2048
DEPTH = 2
DEC_BATCH = 16
DEC_SEQ = 32
PAST_LEN = 4096

CHUNK = 64
HEAD_DIM = 64
A_HEADS = 4
IDX_HEADS = 8
IDX_DIM = 32
TOPK_MAX = 256
B_GROUPS = 4
B_CHUNK = 128
C_HEADS = 4
D_HEADS = 4
BRANCH_WIDTH = 4 * HEAD_DIM
N_BRANCH = 4
D_FF = 2816
CONV_W = 3
Q_BLOCK = 128
ROPE_THETA = 10000.0
EPS = 1e-6
F_BIAS_INIT = 2.0
IN_SIZES = (A_HEADS * HEAD_DIM, A_HEADS * HEAD_DIM, A_HEADS * HEAD_DIM,
            IDX_HEADS * IDX_DIM, IDX_DIM, IDX_HEADS,
            B_GROUPS * HEAD_DIM, B_GROUPS * HEAD_DIM,
            C_HEADS * HEAD_DIM, C_HEADS * HEAD_DIM, C_HEADS * HEAD_DIM, C_HEADS,
            D_HEADS * HEAD_DIM, D_HEADS * HEAD_DIM, D_HEADS * HEAD_DIM)
IN_COLS = sum(IN_SIZES)

kernel_name = 'hybrid_streaming_encoder_step'


def rms_norm(x, g):
    xf = x.astype(jnp.float32)
    y = xf * lax.rsqrt(jnp.mean(xf * xf, axis=-1, keepdims=True) + EPS)
    return (y * g.astype(jnp.float32)).astype(x.dtype)


def layer_norm(x, g, b):
    xf = x.astype(jnp.float32)
    xc = xf - jnp.mean(xf, axis=-1, keepdims=True)
    var = jnp.mean(xc * xc, axis=-1, keepdims=True)
    return (xc * lax.rsqrt(var + EPS) * g.astype(jnp.float32) + b.astype(jnp.float32)).astype(x.dtype)


def rope(x, pos):
    half = x.shape[-1] // 2
    inv = ROPE_THETA ** (-jnp.arange(half, dtype=jnp.float32) / half)
    ang = pos.astype(jnp.float32)[:, None] * inv[None, :]
    cos = jnp.cos(ang)[:, None, :]
    sin = jnp.sin(ang)[:, None, :]
    xf = x.astype(jnp.float32)
    x1, x2 = xf[..., :half], xf[..., half:]
    return jnp.concatenate([x1 * cos - x2 * sin, x2 * cos + x1 * sin], axis=-1).astype(x.dtype)


def split_cols(p):
    offsets = np.cumsum(np.array(IN_SIZES))[:-1]
    return jnp.split(p, [int(o) for o in offsets], axis=-1)


def to_blocks(a, block):
    b, t = a.shape[:2]
    return jnp.swapaxes(a.reshape(b, t // block, block, *a.shape[2:]), 0, 1)


def from_blocks(a):
    nb, b, blk = a.shape[:3]
    return jnp.swapaxes(a, 0, 1).reshape(b, nb * blk, *a.shape[3:])


def dsa_attend(q, qi, wi, q_pos, k, v, ki, n_sel):
    n_keys = k.shape[1]
    limit = (q_pos // CHUNK + 1) * CHUNK
    dots = jnp.einsum('bqhd,bsd->bqhs', qi.astype(jnp.float32), ki.astype(jnp.float32)) * IDX_DIM ** -0.5
    score = jnp.einsum('bqh,bqhs->bqs', wi.astype(jnp.float32) * IDX_HEADS ** -0.5, jax.nn.relu(dots))
    admissible = jnp.arange(n_keys)[None, :] < limit[:, None]
    score = jnp.where(admissible[None], score, -jnp.inf)
    _, idx = lax.top_k(score, n_sel)
    valid = idx < limit[None, :, None]
    gather = jax.vmap(lambda rows, ids: rows[ids])
    k_sel = gather(k, idx)
    v_sel = gather(v, idx)
    logits = jnp.einsum('bqhd,bqkhd->bhqk', q, k_sel).astype(jnp.float32) * HEAD_DIM ** -0.5
    logits = jnp.where(valid[:, None], logits, -jnp.inf)
    p = jax.nn.softmax(logits, axis=-1).astype(v.dtype)
    return jnp.einsum('bhqk,bqkhd->bqhd', p, v_sel)


def fox_attend(q, f_q, q_pos, k, v, f_k):
    n_keys = k.shape[1]
    s = jnp.einsum('bqhd,bshd->bhqs', q, k).astype(jnp.float32) * HEAD_DIM ** -0.5
    s = s + jnp.moveaxis(f_q, 2, 1)[..., None] - jnp.moveaxis(f_k, 2, 1)[:, :, None, :]
    causal = jnp.arange(n_keys)[None, :] <= q_pos[:, None]
    s = jnp.where(causal, s, -jnp.inf)
    p = jax.nn.softmax(s, axis=-1).astype(v.dtype)
    return jnp.einsum('bhqs,bshd->bqhd', p, v)


def sb_attend(q, q_pos, k, v):
    n_keys = k.shape[1]
    z = jnp.einsum('bqhd,bshd->bhqs', q, k).astype(jnp.float32) * HEAD_DIM ** -0.5
    strict = jnp.arange(n_keys)[None, :] < q_pos[:, None]
    log_keep = jnp.where(strict, jax.nn.log_sigmoid(-z), 0.0)
    later = lax.cumsum(log_keep, axis=3, reverse=True) - log_keep
    weight = jnp.where(strict, jnp.exp(jax.nn.log_sigmoid(z) + later), 0.0)
    return jnp.einsum('bhqs,bshd->bqhd', weight.astype(v.dtype), v)


def spatial_gate(u, v, w_s, b_s):
    n = v.shape[2]
    tri = jnp.tril(jnp.ones((n, n), dtype=bool))
    w = jnp.where(tri[None], w_s[:, :n, :n], 0).astype(v.dtype)
    bias = jnp.swapaxes(b_s[:, :n], 0, 1)[None, None, :, :, None].astype(v.dtype)
    return u * (jnp.einsum('gts,bnsgc->bntgc', w, v) + bias)


def token_mixers(h, w_in, f_bias, lnv_g, lnv_b, w_s, b_s, w_branch, w_gate, b_gate, w_out, past):
    B, T, _ = h.shape
    (a_q, a_k, a_v, a_qi, a_ki, a_w, b_u, b_v,
     c_q, c_k, c_v, c_f, d_q, d_k, d_v) = split_cols(h @ w_in)
    heads = lambda t: t.reshape(B, T, -1, HEAD_DIM)
    past_len = 0 if past is None else past[0].shape[1]
    pos = past_len + jnp.arange(T)
    q_a, k_a, v_a = rope(heads(a_q), pos), rope(heads(a_k), pos), heads(a_v)
    qi = rope(a_qi.reshape(B, T, IDX_HEADS, IDX_DIM), pos)
    ki = rope(a_ki[:, :, None, :], pos)[:, :, 0, :]
    u = jax.nn.gelu(b_u)
    v_b = layer_norm(jax.nn.gelu(b_v), lnv_g, lnv_b)
    n = min(T, B_CHUNK)
    o_b = spatial_gate(u.reshape(B, T // n, n, B_GROUPS, HEAD_DIM),
                       v_b.reshape(B, T // n, n, B_GROUPS, HEAD_DIM), w_s, b_s).reshape(B, T, BRANCH_WIDTH)
    q_c, k_c, v_c = heads(c_q), heads(c_k), heads(c_v)
    logf = jax.nn.log_sigmoid(c_f.astype(jnp.float32) + f_bias.astype(jnp.float32))
    q_d, k_d, v_d = heads(d_q), heads(d_k), heads(d_v)
    if past is None:
        n_sel = min(TOPK_MAX, T // 4)
        o_a = from_blocks(lax.map(
            lambda blk: dsa_attend(blk[0], blk[1], blk[2], blk[3], k_a, v_a, ki, n_sel),
            (to_blocks(q_a, CHUNK), to_blocks(qi, CHUNK), to_blocks(a_w, CHUNK), pos.reshape(-1, CHUNK))))
        f_all = lax.cumsum(logf, axis=1)
        o_c = from_blocks(lax.map(
            lambda blk: fox_attend(blk[0], blk[1], blk[2], k_c, v_c, f_all),
            (to_blocks(q_c, Q_BLOCK), to_blocks(f_all, Q_BLOCK), pos.reshape(-1, Q_BLOCK))))
        o_d = from_blocks(lax.map(
            lambda blk: sb_attend(blk[0], blk[1], k_d, v_d),
            (to_blocks(q_d, Q_BLOCK), pos.reshape(-1, Q_BLOCK))))
    else:
        (pk_a, pv_a, pki, pk_c, pv_c, plogf, pk_d, pv_d) = past
        cat = lambda old, new: jnp.concatenate([old, new], axis=1)
        n_sel = min(TOPK_MAX, (past_len + T) // 4)
        o_a = dsa_attend(q_a, qi, a_w, pos, cat(pk_a, k_a), cat(pv_a, v_a), cat(pki, ki), n_sel)
        f_all = lax.cumsum(cat(plogf.astype(jnp.float32), logf), axis=1)
        o_c = fox_attend(q_c, f_all[:, past_len:], pos, cat(pk_c, k_c), cat(pv_c, v_c), f_all)
        o_d = sb_attend(q_d, pos, cat(pk_d, k_d), cat(pv_d, v_d))
    branches = (o_a.reshape(B, T, -1), o_b, o_c.reshape(B, T, -1), o_d.reshape(B, T, -1))
    gates = jax.nn.sigmoid((h @ w_gate + b_gate).astype(jnp.float32)).astype(h.dtype)
    gates = gates.reshape(B, T, N_BRANCH, D_MODEL)
    merged = gates[:, :, 0] * (branches[0] @ w_branch[0])
    for i in range(1, N_BRANCH):
        merged = merged + gates[:, :, i] * (branches[i] @ w_branch[i])
    return merged @ w_out, (k_a, v_a, ki, v_b, k_c, v_c, logf, k_d, v_d)


def conv_ffn(h, w_up, w_conv, b_conv, w_down, prefix):
    T = h.shape[1]
    up = h @ w_up
    ext = jnp.concatenate([prefix.astype(up.dtype), up], axis=1)
    conv = ext[:, 0:T] * w_conv[0]
    for j in range(1, CONV_W):
        conv = conv + ext[:, j:j + T] * w_conv[j]
    conv = conv + b_conv
    gate, val = jnp.split(conv, 2, axis=-1)
    return (jax.nn.silu(gate) * val) @ w_down, ext[:, T:]


def trunk_layer(x, c, params, past, conv_prefix):
    (g1, g2, w_mod, b_mod, w_in, f_bias, lnv_g, lnv_b, w_s, b_s,
     w_branch, w_gate, b_gate, w_out, w_up, w_conv, b_conv, w_down) = params
    mod = jax.nn.silu(c) @ w_mod + b_mod
    shift1, scale1, gate1, shift2, scale2, gate2 = [m[:, None, :] for m in jnp.split(mod, 6, axis=-1)]
    h = rms_norm(x, g1) * (1 + scale1) + shift1
    y, rows = token_mixers(h, w_in, f_bias, lnv_g, lnv_b, w_s, b_s, w_branch, w_gate, b_gate, w_out, past)
    x = x + gate1 * y
    h = rms_norm(x, g2) * (1 + scale2) + shift2
    f, conv_state = conv_ffn(h, w_up, w_conv, b_conv, w_down, conv_prefix)
    x = x + gate2 * f
    return x, rows + (conv_state,)


def setup_inputs(seed: int = 0) -> dict:
    key = jax.random.key(seed)
    ks = iter(jax.random.split(key, 40))
    nrm = lambda shape, scale=1.0: scale * jax.random.normal(next(ks), shape, jnp.float32)
    kv = lambda h: (DEPTH, DEC_BATCH, PAST_LEN, h, HEAD_DIM)
    return {
        'x_prompt': nrm((BATCH, SEQ, D_MODEL)),
        'x_sample': nrm((DEC_BATCH, DEC_SEQ, D_MODEL)),
        'cache_a_k': nrm(kv(A_HEADS)),
        'cache_a_v': nrm(kv(A_HEADS)),
        'cache_a_kidx': nrm((DEPTH, DEC_BATCH, PAST_LEN, IDX_DIM)),
        'cache_c_k': nrm(kv(C_HEADS)),
        'cache_c_v': nrm(kv(C_HEADS)),
        'cache_c_logf': jax.nn.log_sigmoid(nrm((DEPTH, DEC_BATCH, PAST_LEN, C_HEADS)) + F_BIAS_INIT),
        'cache_d_k': nrm(kv(D_HEADS)),
        'cache_d_v': nrm(kv(D_HEADS)),
        'state_ffn_conv': nrm((DEPTH, DEC_BATCH, CONV_W - 1, 2 * D_FF)),
        'c_prompt': nrm((BATCH, D_MODEL)),
        'c_sample': nrm((DEC_BATCH, D_MODEL)),
        'norm1_g': 1.0 + nrm((DEPTH, D_MODEL), 0.05),
        'norm2_g': 1.0 + nrm((DEPTH, D_MODEL), 0.05),
        'w_mod': nrm((DEPTH, D_MODEL, 6 * D_MODEL), 0.5 * D_MODEL ** -0.5),
        'b_mod': nrm((DEPTH, 6 * D_MODEL), 0.02),
        'w_in': nrm((DEPTH, D_MODEL, IN_COLS), D_MODEL ** -0.5),
        'f_bias': F_BIAS_INIT + nrm((DEPTH, C_HEADS), 0.5),
        'lnv_g': 1.0 + nrm((DEPTH, BRANCH_WIDTH), 0.05),
        'lnv_b': nrm((DEPTH, BRANCH_WIDTH), 0.02),
        'w_spatial': nrm((DEPTH, B_GROUPS, B_CHUNK, B_CHUNK), B_CHUNK ** -0.5),
        'b_spatial': 1.0 + nrm((DEPTH, B_GROUPS, B_CHUNK), 0.1),
        'w_branch': nrm((DEPTH, N_BRANCH, BRANCH_WIDTH, D_MODEL), BRANCH_WIDTH ** -0.5),
        'w_gate': nrm((DEPTH, D_MODEL, N_BRANCH * D_MODEL), D_MODEL ** -0.5),
        'b_gate': nrm((DEPTH, N_BRANCH * D_MODEL), 0.02),
        'w_out': nrm((DEPTH, D_MODEL, D_MODEL), D_MODEL ** -0.5),
        'w_up': nrm((DEPTH, D_MODEL, 2 * D_FF), D_MODEL ** -0.5),
        'w_conv': nrm((DEPTH, CONV_W, 2 * D_FF), CONV_W ** -0.5),
        'b_conv': nrm((DEPTH, 2 * D_FF), 0.02),
        'w_down': nrm((DEPTH, D_FF, D_MODEL), D_FF ** -0.5),
        'final_g': 1.0 + nrm((D_MODEL,), 0.05),
    }


def reference(x_prompt, x_sample, cache_a_k, cache_a_v, cache_a_kidx, cache_c_k, cache_c_v, cache_c_logf,
              cache_d_k, cache_d_v, state_ffn_conv, c_prompt, c_sample, norm1_g, norm2_g, w_mod, b_mod, w_in,
              f_bias, lnv_g, lnv_b, w_spatial, b_spatial, w_branch, w_gate, b_gate, w_out, w_up, w_conv, b_conv,
              w_down, final_g):
    prefix = jnp.zeros((x_prompt.shape[0], CONV_W - 1, w_up.shape[-1]), x_prompt.dtype)
    xp, xs = x_prompt, x_sample
    rows_p, rows_s = [], []
    for l in range(DEPTH):
        params = (norm1_g[l], norm2_g[l], w_mod[l], b_mod[l], w_in[l], f_bias[l], lnv_g[l], lnv_b[l],
                  w_spatial[l], b_spatial[l], w_branch[l], w_gate[l], b_gate[l], w_out[l], w_up[l],
                  w_conv[l], b_conv[l], w_down[l])
        past = (cache_a_k[l], cache_a_v[l], cache_a_kidx[l], cache_c_k[l], cache_c_v[l], cache_c_logf[l],
                cache_d_k[l], cache_d_v[l])
        xp, new_p = trunk_layer(xp, c_prompt, params, None, prefix)
        xs, new_s = trunk_layer(xs, c_sample, params, past, state_ffn_conv[l])
        rows_p.append(new_p)
        rows_s.append(new_s)

    def stacked(rows, i):
        return jnp.stack([r[i] for r in rows], axis=0)

    y_prompt = rms_norm(xp, final_g)
    y_sample = rms_norm(xs, final_g)
    return (y_prompt, y_sample,
            stacked(rows_p, 0), stacked(rows_s, 0),
            stacked(rows_p, 1), stacked(rows_s, 1),
            stacked(rows_p, 2), stacked(rows_s, 2),
            stacked(rows_s, 3),
            stacked(rows_p, 4), stacked(rows_s, 4),
            stacked(rows_p, 5), stacked(rows_s, 5),
            stacked(rows_p, 6), stacked(rows_s, 6),
            stacked(rows_p, 7), stacked(rows_s, 7),
            stacked(rows_p, 8), stacked(rows_s, 8),
            stacked(rows_p, 9), stacked(rows_s, 9))
```

```python
import functools

import jax
import jax.numpy as jnp
from jax import lax
from jax.experimental import pallas as pl
from jax.experimental.pallas import tpu as pltpu

F32 = jnp.float32
BF16 = jnp.bfloat16
I32 = jnp.int32

HEAD_DIM = 64
N_HEADS = 4
BRANCH = N_HEADS * HEAD_DIM
IDX_HEADS = 8
IDX_DIM = 32
CHUNK = 64
TOPK_MAX = 256
B_CHUNK = 128
N_BRANCH = 4
CONV_W = 3
ROPE_THETA = 10000.0
EPS = 1e-6

SLAB = 128
SLAB_KI = 0
SLAB_W = 32
SLAB_F = 40
AUG = 128
NEG = -1e30
INT_MIN = -2 ** 31
SB_CUTOFF = 150.0
VMEM_LIMIT = 56 * 1024 * 1024


def _params(n_axes, vmem=VMEM_LIMIT):
    return pltpu.CompilerParams(dimension_semantics=("arbitrary",) * n_axes, vmem_limit_bytes=vmem)


def _dot(a, b):
    return jnp.dot(a, b, preferred_element_type=F32)


def _dot_t(a, b):
    return lax.dot_general(a, b, (((1,), (1,)), ((), ())), preferred_element_type=F32)


def _split2(x):
    hi = x.astype(BF16)
    lo = (x - hi.astype(F32)).astype(BF16)
    return hi, lo


def _split3(x):
    x1 = x.astype(BF16)
    r = x - x1.astype(F32)
    x2 = r.astype(BF16)
    x3 = (r - x2.astype(F32)).astype(BF16)
    return x1, x2, x3


def _softplus(z):
    return jnp.maximum(z, 0.0) + jnp.log1p(jnp.exp(-jnp.abs(z)))


def _rms(x, g):
    ms = jnp.mean(x * x, axis=-1, keepdims=True)
    return x * lax.rsqrt(ms + EPS) * g


def _rope(x, cos, sin_signed, half):
    width = x.shape[-1]
    lane = lax.broadcasted_iota(I32, x.shape, 1)
    fwd = pltpu.roll(x, half, axis=1)
    bwd = pltpu.roll(x, width - half, axis=1)
    rot = jnp.where((lane % (2 * half)) < half, bwd, fwd)
    return x * cos + rot * sin_signed


def _mod_kernel(c_ref, w_ref, b_ref, o_ref):
    c = c_ref[...]
    a = c * jax.nn.sigmoid(c)
    a1, a2 = _split2(a)
    w1, w2 = _split2(w_ref[0])
    o_ref[0] = _dot(a1, w1) + _dot(a1, w2) + _dot(a2, w1) + b_ref[0]


def _modulation(c_all, w_mod, b_mod):
    depth, d, n = w_mod.shape
    rows = c_all.shape[0]
    tn = 1536
    return pl.pallas_call(
        _mod_kernel,
        grid=(depth, n // tn),
        in_specs=[pl.BlockSpec((rows, d), lambda l, j: (0, 0)),
                  pl.BlockSpec((1, d, tn), lambda l, j: (l, 0, j)),
                  pl.BlockSpec((1, 1, tn), lambda l, j: (l, 0, j))],
        out_specs=pl.BlockSpec((1, rows, tn), lambda l, j: (l, 0, j)),
        out_shape=jax.ShapeDtypeStruct((depth, rows, n), F32),
        compiler_params=_params(2),
        name="modulation",
    )(c_all, w_mod, b_mod.reshape(depth, 1, n))


def _proj_kernel(x_ref, mod_ref, g1_ref, w_ref, cosa_ref, sina_ref, cosi_ref, sini_ref, coss_ref, sins_ref,
                 fb_ref, lng_ref, lnb_ref, ws_ref, bs_ref,
                 qa_ref, ka_ref, va_ref, qi_ref, sm_ref, ob_ref, vb_ref,
                 qc_ref, kc_ref, vc_ref, qd_ref, kd_ref, vd_ref, *, nb, tt):
    m = nb * tt
    d = x_ref.shape[-1]
    x = x_ref[...]
    mod = mod_ref[...]
    h = _rms(x, g1_ref[...]) * (1.0 + mod[:, 1:2, :]) + mod[:, 0:1, :]
    hb = h.reshape(m, d).astype(BF16)

    def col(j):
        return _dot(hb, w_ref[:, j * BRANCH:(j + 1) * BRANCH])

    def put(ref, val):
        ref[...] = val.reshape(nb, tt, val.shape[-1]).astype(ref.dtype)

    scale = HEAD_DIM ** -0.5
    cosa, sina = cosa_ref[...], sina_ref[...]
    put(qa_ref, _rope(col(0), cosa, sina, HEAD_DIM // 2) * scale)
    put(ka_ref, _rope(col(1), cosa, sina, HEAD_DIM // 2))
    put(va_ref, col(2))
    put(qi_ref, _rope(col(3), cosi_ref[...], sini_ref[...], IDX_DIM // 2))
    u = jax.nn.gelu(col(4))
    vg = jax.nn.gelu(col(5))
    mu = jnp.mean(vg, axis=-1, keepdims=True)
    vc_ = vg - mu
    var = jnp.mean(vc_ * vc_, axis=-1, keepdims=True)
    vb = vc_ * lax.rsqrt(var + EPS) * lng_ref[...] + lnb_ref[...]
    put(vb_ref, vb)
    vbb = vb.astype(BF16)
    group = lax.broadcasted_iota(I32, (m, BRANCH), 1) // HEAD_DIM
    mixed = jnp.zeros((m, BRANCH), F32)
    for g in range(N_HEADS):
        mixed = jnp.where(group == g, _dot(ws_ref[g], vbb), mixed)
    put(ob_ref, u * (mixed + bs_ref[...]))
    put(qc_ref, col(6) * scale)
    put(kc_ref, col(7))
    put(vc_ref, col(8))
    put(qd_ref, col(9) * scale)
    put(kd_ref, col(10))
    put(vd_ref, col(11))
    sm = _dot(hb, w_ref[:, 12 * BRANCH:12 * BRANCH + SLAB])
    lane = lax.broadcasted_iota(I32, sm.shape, 1)
    roped = _rope(sm, coss_ref[...], sins_ref[...], IDX_DIM // 2)
    logf = -_softplus(-(sm + fb_ref[...]))
    is_f = jnp.where(lane >= SLAB_F, jnp.where(lane < SLAB_F + N_HEADS, 1, 0), 0)
    put(sm_ref, jnp.where(is_f == 1, logf, roped))


def _proj(x, mod, g1, w_in, tables, fb, lng, lnb, ws, bs, *, nb, tt):
    bsz, t, d = x.shape
    m = nb * tt
    n_t = t // tt
    ncol = w_in.shape[1]
    row_blk = lambda w: pl.BlockSpec((m, w), lambda ti, bi: (ti, 0))
    tok = lambda w: pl.BlockSpec((nb, tt, w), lambda ti, bi: (bi, ti, 0))
    const2 = lambda a: pl.BlockSpec(a.shape, lambda ti, bi: (0,) * a.ndim)
    cosa, sina, cosi, sini, coss, sins = tables
    out_dtypes = (BF16, F32, F32, F32, F32, BF16, F32, BF16, F32, F32, BF16, F32, F32)
    out_widths = (BRANCH, BRANCH, BRANCH, BRANCH, SLAB) + (BRANCH,) * 8
    return pl.pallas_call(
        functools.partial(_proj_kernel, nb=nb, tt=tt),
        grid=(n_t, bsz // nb),
        in_specs=[tok(d),
                  pl.BlockSpec((nb, 6, d), lambda ti, bi: (bi, 0, 0)),
                  const2(g1), const2(w_in),
                  row_blk(BRANCH), row_blk(BRANCH), row_blk(BRANCH), row_blk(BRANCH), row_blk(SLAB), row_blk(SLAB),
                  const2(fb), const2(lng), const2(lnb), const2(ws), const2(bs)],
        out_specs=[tok(w) for w in out_widths],
        out_shape=[jax.ShapeDtypeStruct((bsz, t, w), dt) for w, dt in zip(out_widths, out_dtypes)],
        compiler_params=_params(2),
        name="proj",
    )(x, mod, g1, w_in, cosa, sina, cosi, sini, coss, sins, fb, lng, lnb, ws, bs)


def _head_mask(shape):
    return lax.broadcasted_iota(I32, shape, 1) // HEAD_DIM


def _online_softmax_heads(q, nblk, tq, tk, k_blk, v_blk, bias_blk):
    head = _head_mask((tq, BRANCH))
    out = jnp.zeros((tq, BRANCH), F32)
    for h in range(N_HEADS):
        qh = q(h)

        def body(j, carry, qh=qh, h=h):
            m_run, l_run, acc = carry
            off = pl.multiple_of(j * tk, tk)
            s = _dot_t(qh, k_blk(h, off)) + bias_blk(off)
            m_new = jnp.maximum(m_run, jnp.max(s, axis=1, keepdims=True))
            alpha = jnp.exp(m_run - m_new)
            p = jnp.exp(s - m_new)
            l_new = alpha * l_run + jnp.sum(p, axis=1, keepdims=True)
            acc = alpha * acc + _dot(p.astype(BF16), v_blk(off))
            return m_new, l_new, acc

        init = (jnp.full((tq, 1), NEG, F32), jnp.zeros((tq, 1), F32), jnp.zeros((tq, BRANCH), F32))
        _, l_fin, acc = lax.fori_loop(0, nblk, body, init)
        out = jnp.where(head == h, acc / l_fin, out)
    return out


def _dsa_kernel(qa_ref, qi_ref, qsm_ref, k_ref, v_ref, ksm_ref, o_ref,
                kb_sc, vb_sc, kic_sc, lhs_sc, key_sc, bias_sc, *, tq, tk, past_len, l_valid, n_sel):
    qb = pl.program_id(1)

    @pl.when(qb == 0)
    def _():
        kb_sc[...] = k_ref[0].astype(BF16)
        vb_sc[...] = v_ref[0].astype(BF16)
        ks = ksm_ref[0]
        lane = lax.broadcasted_iota(I32, ks.shape, 1)
        ki = jnp.where(lane < IDX_DIM, ks, 0.0)
        hi = ki.astype(BF16).astype(F32)
        lo = ki - hi
        kic_sc[...] = (hi + pltpu.roll(lo, IDX_DIM, axis=1) + pltpu.roll(hi, 2 * IDX_DIM, axis=1)).astype(BF16)

    qi = qi_ref[0]
    lane = lax.broadcasted_iota(I32, (tq, SLAB), 1)
    for h in range(IDX_HEADS):
        rolled = qi if h == 0 else pltpu.roll(qi, BRANCH - IDX_DIM * h, axis=1)
        piece = jnp.where(lane < IDX_DIM, rolled[:, :SLAB], 0.0)
        hi = piece.astype(BF16).astype(F32)
        lo = piece - hi
        lhs_sc[h] = (hi + pltpu.roll(hi, IDX_DIM, axis=1) + pltpu.roll(lo, 2 * IDX_DIM, axis=1)).astype(BF16)
    qsm = qsm_ref[0]
    w_cols = [qsm[:, SLAB_W + h:SLAB_W + h + 1] for h in range(IDX_HEADS)]

    row0 = past_len + qb * tq
    pos = row0 + lax.broadcasted_iota(I32, (tq, 1), 0)
    limit = jnp.minimum((pos // CHUNK + 1) * CHUNK, l_valid)
    top_limit = jnp.minimum(((row0 + tq - 1) // CHUNK + 1) * CHUNK, l_valid)
    nblk = (top_limit + tk - 1) // tk

    def score_blk(j, _):
        off = pl.multiple_of(j * tk, tk)
        kc = kic_sc[pl.ds(off, tk), :]
        acc = jnp.zeros((tq, tk), F32)
        for h in range(IDX_HEADS):
            acc = acc + w_cols[h] * jnp.maximum(_dot_t(lhs_sc[h], kc), 0.0)
        acc = jnp.where(acc == 0.0, 0.0, acc)
        bits = pltpu.bitcast(acc, I32)
        key = jnp.where(bits < 0, bits ^ 0x7FFFFFFF, bits)
        colx = off + lax.broadcasted_iota(I32, (tq, tk), 1)
        key_sc[:, pl.ds(off, tk)] = jnp.where(colx < limit, key, INT_MIN)
        return 0

    lax.fori_loop(0, nblk, score_blk, 0)

    def count_ge(cand):
        cand_b = jnp.broadcast_to(cand, (tq, 128))

        def body(j, part):
            off = pl.multiple_of(j * tk, tk)
            kb = key_sc[:, pl.ds(off, tk)]
            for c in range(tk // 128):
                part = part + jnp.where(kb[:, c * 128:(c + 1) * 128] >= cand_b, 1.0, 0.0)
            return part

        part = lax.fori_loop(0, nblk, body, jnp.zeros((tq, 128), F32))
        return jnp.sum(part, axis=1, keepdims=True)

    kf = float(n_sel)
    thr = jnp.where(count_ge(jnp.zeros((tq, 1), I32)) >= kf, 0, INT_MIN).astype(I32)

    def search(i, thr):
        cand = thr | jnp.left_shift(jnp.int32(1), 30 - i)
        return jnp.where(count_ge(cand) >= kf, cand, thr)

    thr = lax.fori_loop(0, 31, search, thr)
    thr = jnp.maximum(thr, INT_MIN + 1)
    cnt_ge = count_ge(thr)
    cnt_gt = count_ge(thr + 1)
    room = kf - cnt_gt
    any_tie = jnp.max(jnp.where(cnt_ge > kf, 1.0, 0.0)) > 0.0

    def bias_fast(j, _):
        off = pl.multiple_of(j * tk, tk)
        bias_sc[:, pl.ds(off, tk)] = jnp.where(key_sc[:, pl.ds(off, tk)] >= thr, 0.0, NEG)
        return 0

    lax.fori_loop(0, nblk, bias_fast, 0)

    @pl.when(any_tie)
    def _():
        ri = lax.broadcasted_iota(I32, (tk, tk), 0)
        ci = lax.broadcasted_iota(I32, (tk, tk), 1)
        upper = jnp.where(ri <= ci, 1.0, 0.0).astype(BF16)

        def bias_tie(j, seen):
            off = pl.multiple_of(j * tk, tk)
            kb = key_sc[:, pl.ds(off, tk)]
            eq = jnp.where(kb == thr, 1.0, 0.0)
            rank = _dot(eq.astype(BF16), upper) + seen
            keep_eq = jnp.where(rank <= room, 0.0, NEG)
            bias_sc[:, pl.ds(off, tk)] = jnp.where(kb > thr, 0.0, jnp.where(kb == thr, keep_eq, NEG))
            return seen + jnp.sum(eq, axis=1, keepdims=True)

        lax.fori_loop(0, nblk, bias_tie, jnp.zeros((tq, 1), F32))

    qa = qa_ref[0]
    head_b = _head_mask((tq, BRANCH))
    out = _online_softmax_heads(
        lambda h: jnp.where(head_b == h, qa, jnp.zeros_like(qa)), nblk, tq, tk,
        lambda h, off: kb_sc[pl.ds(off, tk), :],
        lambda off: vb_sc[pl.ds(off, tk), :],
        lambda off: bias_sc[:, pl.ds(off, tk)])
    o_ref[0] = out.astype(o_ref.dtype)


def _dsa(qa, qi, qsm, k, v, ksm, *, tq, tk, past_len, l_valid, n_sel):
    bsz, t, _ = qa.shape
    lp = k.shape[1]
    qspec = lambda w: pl.BlockSpec((1, tq, w), lambda b, i: (b, i, 0))
    kspec = lambda w: pl.BlockSpec((1, lp, w), lambda b, i: (b, 0, 0))
    return pl.pallas_call(
        functools.partial(_dsa_kernel, tq=tq, tk=tk, past_len=past_len, l_valid=l_valid, n_sel=n_sel),
        grid=(bsz, t // tq),
        in_specs=[qspec(BRANCH), qspec(BRANCH), qspec(SLAB), kspec(BRANCH), kspec(BRANCH), kspec(SLAB)],
        out_specs=qspec(BRANCH),
        out_shape=jax.ShapeDtypeStruct((bsz, t, BRANCH), BF16),
        scratch_shapes=[pltpu.VMEM((lp, BRANCH), BF16), pltpu.VMEM((lp, BRANCH), BF16),
                        pltpu.VMEM((lp, SLAB), BF16), pltpu.VMEM((IDX_HEADS, tq, SLAB), BF16),
                        pltpu.VMEM((tq, lp), I32), pltpu.VMEM((tq, lp), F32)],
        compiler_params=_params(2),
        name="dsa",
    )(qa, qi, qsm, k, v, ksm)


def _foxprep_kernel(sm_ref, k_ref, kaug_ref, fs_ref, carry_sc, *, tl):
    @pl.when(pl.program_id(1) == 0)
    def _():
        carry_sc[...] = jnp.zeros_like(carry_sc)

    sm = sm_ref[0]
    ri = lax.broadcasted_iota(I32, (tl, tl), 0)
    ci = lax.broadcasted_iota(I32, (tl, tl), 1)
    lower = jnp.where(ri >= ci, 1.0, 0.0).astype(BF16)
    s1, s2, s3 = _split3(sm)
    fsum = _dot(lower, s1) + _dot(lower, s2) + _dot(lower, s3) + carry_sc[...]
    carry_sc[...] = fsum[tl - 1:tl, :]
    fs_ref[0] = fsum
    f1, f2, f3 = (p.astype(F32) for p in _split3(fsum))
    k = k_ref[0]
    lane = lax.broadcasted_iota(I32, (tl, AUG), 1)
    for h in range(N_HEADS):
        rolled = k if h == 0 else pltpu.roll(k, BRANCH - HEAD_DIM * h, axis=1)
        piece = rolled[:, :AUG]
        c = SLAB_F + h
        extra = jnp.where(lane < HEAD_DIM + 3, 1.0,
                          jnp.where(lane == HEAD_DIM + 3, -f1[:, c:c + 1],
                                    jnp.where(lane == HEAD_DIM + 4, -f2[:, c:c + 1],
                                              jnp.where(lane == HEAD_DIM + 5, -f3[:, c:c + 1], 0.0))))
        kaug_ref[0, :, h * AUG:(h + 1) * AUG] = jnp.where(lane < HEAD_DIM, piece, extra).astype(BF16)


def _foxprep(sm, k, *, tl):
    bsz, lp, _ = k.shape
    return pl.pallas_call(
        functools.partial(_foxprep_kernel, tl=tl),
        grid=(bsz, lp // tl),
        in_specs=[pl.BlockSpec((1, tl, SLAB), lambda b, i: (b, i, 0)),
                  pl.BlockSpec((1, tl, BRANCH), lambda b, i: (b, i, 0))],
        out_specs=[pl.BlockSpec((1, tl, N_HEADS * AUG), lambda b, i: (b, i, 0)),
                   pl.BlockSpec((1, tl, SLAB), lambda b, i: (b, i, 0))],
        out_shape=[jax.ShapeDtypeStruct((bsz, lp, N_HEADS * AUG), BF16),
                   jax.ShapeDtypeStruct((bsz, lp, SLAB), F32)],
        scratch_shapes=[pltpu.VMEM((1, SLAB), F32)],
        compiler_params=_params(2),
        name="foxprep",
    )(sm, k)


def _fox_kernel(q_ref, fq_ref, kaug_ref, v_ref, o_ref, vb_sc, *, tq, tk, past_len):
    qb = pl.program_id(1)

    @pl.when(qb == 0)
    def _():
        vb_sc[...] = v_ref[0].astype(BF16)

    q = q_ref[0].astype(F32)
    f1, f2, f3 = (p.astype(F32) for p in _split3(fq_ref[0]))
    lane = lax.broadcasted_iota(I32, (tq, AUG), 1)
    row0 = past_len + qb * tq
    pos = row0 + lax.broadcasted_iota(I32, (tq, 1), 0)
    nblk = (row0 + tq + tk - 1) // tk

    def q_aug(h):
        rolled = q if h == 0 else pltpu.roll(q, BRANCH - HEAD_DIM * h, axis=1)
        c = SLAB_F + h
        extra = jnp.where(lane == HEAD_DIM, f1[:, c:c + 1],
                          jnp.where(lane == HEAD_DIM + 1, f2[:, c:c + 1],
                                    jnp.where(lane == HEAD_DIM + 2, f3[:, c:c + 1],
                                              jnp.where(lane < HEAD_DIM + 6, 1.0, 0.0))))
        return jnp.where(lane < HEAD_DIM, rolled[:, :AUG], extra).astype(BF16)

    def bias(off):
        colx = off + lax.broadcasted_iota(I32, (tq, tk), 1)
        return jnp.where(colx <= pos, 0.0, NEG)

    out = _online_softmax_heads(
        q_aug, nblk, tq, tk,
        lambda h, off: kaug_ref[0, pl.ds(off, tk), h * AUG:(h + 1) * AUG],
        lambda off: vb_sc[pl.ds(off, tk), :],
        bias)
    o_ref[0] = out.astype(o_ref.dtype)


def _fox(q, fs, kaug, v, *, tq, tk, past_len):
    bsz, t, _ = q.shape
    lp = v.shape[1]
    q_off = past_len // tq
    return pl.pallas_call(
        functools.partial(_fox_kernel, tq=tq, tk=tk, past_len=past_len),
        grid=(bsz, t // tq),
        in_specs=[pl.BlockSpec((1, tq, BRANCH), lambda b, i: (b, i, 0)),
                  pl.BlockSpec((1, tq, SLAB), lambda b, i: (b, i + q_off, 0)),
                  pl.BlockSpec((1, lp, N_HEADS * AUG), lambda b, i: (b, 0, 0)),
                  pl.BlockSpec((1, lp, BRANCH), lambda b, i: (b, 0, 0))],
        out_specs=pl.BlockSpec((1, tq, BRANCH), lambda b, i: (b, i, 0)),
        out_shape=jax.ShapeDtypeStruct((bsz, t, BRANCH), BF16),
        scratch_shapes=[pltpu.VMEM((lp, BRANCH), BF16)],
        compiler_params=_params(2),
        name="fox",
    )(q, fs, kaug, v)


def _sb_kernel(q_ref, k_ref, v_ref, o_ref, kb_sc, vb_sc, *, tq, tk, past_len):
    qb = pl.program_id(1)

    @pl.when(qb == 0)
    def _():
        kb_sc[...] = k_ref[0].astype(BF16)
        vb_sc[...] = v_ref[0].astype(BF16)

    q = q_ref[0]
    row0 = past_len + qb * tq
    pos = row0 + lax.broadcasted_iota(I32, (tq, 1), 0)
    j_top = (row0 + tq - 1) // tk
    ri = lax.broadcasted_iota(I32, (tk, tk), 0)
    ci = lax.broadcasted_iota(I32, (tk, tk), 1)
    after = jnp.where(ri > ci, 1.0, 0.0).astype(BF16)
    head = _head_mask((tq, BRANCH))
    out = jnp.zeros((tq, BRANCH), F32)
    for h in range(N_HEADS):
        qh = jnp.where(head == h, q, jnp.zeros_like(q))

        def cond(carry):
            j, run, _ = carry
            return jnp.logical_and(j >= 0, jnp.max(run) > -SB_CUTOFF)

        def body(carry, qh=qh):
            j, run, acc = carry
            off = pl.multiple_of(j * tk, tk)
            z = _dot_t(qh, kb_sc[pl.ds(off, tk), :])
            colx = off + lax.broadcasted_iota(I32, (tq, tk), 1)
            strict = colx < pos
            sp = _softplus(z)
            log_keep = jnp.where(strict, -sp, 0.0)
            lk_hi, lk_lo = _split2(log_keep)
            later = _dot(lk_hi, after) + _dot(lk_lo, after) + run
            w = jnp.where(strict, jnp.exp(z - sp + later), 0.0)
            acc = acc + _dot(w.astype(BF16), vb_sc[pl.ds(off, tk), :])
            return j - 1, run + jnp.sum(log_keep, axis=1, keepdims=True), acc

        init = (j_top, jnp.zeros((tq, 1), F32), jnp.zeros((tq, BRANCH), F32))
        _, _, acc = lax.while_loop(cond, body, init)
        out = jnp.where(head == h, acc, out)
    o_ref[0] = out.astype(o_ref.dtype)


def _sb(q, k, v, *, tq, tk, past_len):
    bsz, t, _ = q.shape
    lp = k.shape[1]
    qspec = pl.BlockSpec((1, tq, BRANCH), lambda b, i: (b, i, 0))
    kspec = pl.BlockSpec((1, lp, BRANCH), lambda b, i: (b, 0, 0))
    return pl.pallas_call(
        functools.partial(_sb_kernel, tq=tq, tk=tk, past_len=past_len),
        grid=(bsz, t // tq),
        in_specs=[qspec, kspec, kspec],
        out_specs=qspec,
        out_shape=jax.ShapeDtypeStruct((bsz, t, BRANCH), BF16),
        scratch_shapes=[pltpu.VMEM((lp, BRANCH), BF16), pltpu.VMEM((lp, BRANCH), BF16)],
        compiler_params=_params(2),
        name="sb",
    )(q, k, v)


def _merge_kernel(x_ref, mod_ref, g1_ref, oa_ref, ob_ref, oc_ref, od_ref, wg_ref, bg_ref, wb_ref, wo_ref,
                  xo_ref, *, nb, tt):
    m = nb * tt
    d = x_ref.shape[-1]
    x = x_ref[...]
    mod = mod_ref[...]
    h = _rms(x, g1_ref[...]) * (1.0 + mod[:, 1:2, :]) + mod[:, 0:1, :]
    hb = h.reshape(m, d).astype(BF16)
    merged = None
    for i, o_ref in enumerate((oa_ref, ob_ref, oc_ref, od_ref)):
        gate = jax.nn.sigmoid(_dot(hb, wg_ref[:, i * d:(i + 1) * d]) + bg_ref[:, i * d:(i + 1) * d])
        term = gate * _dot(o_ref[...].reshape(m, BRANCH), wb_ref[i])
        merged = term if merged is None else merged + term
    y = _dot(merged.astype(BF16), wo_ref[...])
    xo_ref[...] = x + mod[:, 2:3, :] * y.reshape(nb, tt, d)


def _merge(x, mod, g1, oa, ob, oc, od, w_gate, b_gate, w_branch, w_out, *, nb, tt):
    bsz, t, d = x.shape
    tok = lambda w: pl.BlockSpec((nb, tt, w), lambda bi, ti: (bi, ti, 0))
    const = lambda a: pl.BlockSpec(a.shape, lambda bi, ti: (0,) * a.ndim)
    return pl.pallas_call(
        functools.partial(_merge_kernel, nb=nb, tt=tt),
        grid=(bsz // nb, t // tt),
        in_specs=[tok(d), pl.BlockSpec((nb, 6, d), lambda bi, ti: (bi, 0, 0)), const(g1),
                  tok(BRANCH), tok(BRANCH), tok(BRANCH), tok(BRANCH),
                  const(w_gate), const(b_gate), const(w_branch), const(w_out)],
        out_specs=tok(d),
        out_shape=jax.ShapeDtypeStruct((bsz, t, d), F32),
        compiler_params=_params(2),
        name="merge",
    )(x, mod, g1, oa, ob, oc, od, w_gate, b_gate, w_branch, w_out)


def _ffn_kernel(x_ref, mod_ref, g2_ref, pfx_ref, wu_ref, wc_ref, bc_ref, wd_ref, fg_ref,
                xo_ref, st_ref, carry_sc, *, nb, tt, cw, final_norm):
    m = nb * tt
    d = x_ref.shape[-1]
    dff = wd_ref.shape[0]
    ti = pl.program_id(1)

    @pl.when(ti == 0)
    def _():
        carry_sc[...] = pfx_ref[...]

    x = x_ref[...]
    mod = mod_ref[...]
    h = _rms(x, g2_ref[...]) * (1.0 + mod[:, 4:5, :]) + mod[:, 3:4, :]
    hb = h.reshape(m, d).astype(BF16)
    rowi = lax.broadcasted_iota(I32, (nb, tt, cw), 1)

    def conv_cols(c0):
        up = _dot(hb, wu_ref[:, c0:c0 + cw]).reshape(nb, tt, cw)
        prev0 = carry_sc[:, 0:1, c0:c0 + cw]
        prev1 = carry_sc[:, 1:2, c0:c0 + cw]
        back1 = jnp.where(rowi == 0, prev1, pltpu.roll(up, 1, axis=1))
        back2 = jnp.where(rowi == 0, prev0, jnp.where(rowi == 1, prev1, pltpu.roll(up, 2, axis=1)))
        carry_sc[:, :, c0:c0 + cw] = up[:, tt - 2:tt, :]
        wc = wc_ref[:, c0:c0 + cw]
        conv = back2 * wc[0:1, :] + back1 * wc[1:2, :] + up * wc[2:3, :] + bc_ref[:, c0:c0 + cw]
        return conv.reshape(m, cw)

    acc = jnp.zeros((m, d), F32)
    for j in range(dff // cw):
        gate = conv_cols(j * cw)
        val = conv_cols(dff + j * cw)
        act = (gate * jax.nn.sigmoid(gate) * val).astype(BF16)
        acc = acc + _dot(act, wd_ref[j * cw:(j + 1) * cw, :])
    xo = x + mod[:, 5:6, :] * acc.reshape(nb, tt, d)
    if final_norm:
        xo = _rms(xo, fg_ref[...])
    xo_ref[...] = xo

    @pl.when(ti == pl.num_programs(1) - 1)
    def _():
        st_ref[...] = carry_sc[...]


def _ffn(x, mod, g2, prefix, w_up, w_conv, b_conv, w_down, final_g, *, nb, tt, final_norm):
    bsz, t, d = x.shape
    dff = w_down.shape[0]
    cw = 256
    tok = pl.BlockSpec((nb, tt, d), lambda bi, ti: (bi, ti, 0))
    const = lambda a: pl.BlockSpec(a.shape, lambda bi, ti: (0,) * a.ndim)
    state = pl.BlockSpec((nb, CONV_W - 1, 2 * dff), lambda bi, ti: (bi, 0, 0))
    return pl.pallas_call(
        functools.partial(_ffn_kernel, nb=nb, tt=tt, cw=cw, final_norm=final_norm),
        grid=(bsz // nb, t // tt),
        in_specs=[tok, pl.BlockSpec((nb, 6, d), lambda bi, ti: (bi, 0, 0)), const(g2), state,
                  const(w_up), const(w_conv), const(b_conv), const(w_down), const(final_g)],
        out_specs=[tok, state],
        out_shape=[jax.ShapeDtypeStruct((bsz, t, d), F32),
                   jax.ShapeDtypeStruct((bsz, CONV_W - 1, 2 * dff), F32)],
        scratch_shapes=[pltpu.VMEM((nb, CONV_W - 1, 2 * dff), F32)],
        compiler_params=_params(2),
        name="ffn",
    )(x, mod, g2, prefix, w_up, w_conv, b_conv, w_down, final_g)


def _rope_tables(pos):
    def tab(half, heads, pad):
        inv = ROPE_THETA ** (-jnp.arange(half, dtype=F32) / half)
        ang = pos.astype(F32)[:, None] * inv[None, :]
        cos, sin = jnp.cos(ang), jnp.sin(ang)
        cos_t = jnp.tile(jnp.concatenate([cos, cos], axis=-1), (1, heads))
        sin_t = jnp.tile(jnp.concatenate([-sin, sin], axis=-1), (1, heads))
        if pad:
            cos_t = jnp.concatenate([cos_t, jnp.ones((pos.shape[0], pad), F32)], axis=-1)
            sin_t = jnp.concatenate([sin_t, jnp.zeros((pos.shape[0], pad), F32)], axis=-1)
        return cos_t, sin_t

    cosa, sina = tab(HEAD_DIM // 2, N_HEADS, 0)
    cosi, sini = tab(IDX_DIM // 2, IDX_HEADS, 0)
    coss, sins = tab(IDX_DIM // 2, 1, SLAB - IDX_DIM)
    return cosa, sina, cosi, sini, coss, sins


def _layout_w_in(w_in):
    sizes = (BRANCH, BRANCH, BRANCH, IDX_HEADS * IDX_DIM, IDX_DIM, IDX_HEADS, BRANCH, BRANCH,
             BRANCH, BRANCH, BRANCH, N_HEADS, BRANCH, BRANCH, BRANCH)
    offs = [0]
    for s in sizes:
        offs.append(offs[-1] + s)
    piece = lambda i: w_in[:, offs[i]:offs[i + 1]]
    a_q, a_k, a_v, a_qi, a_ki, a_w, b_u, b_v, c_q, c_k, c_v, c_f, d_q, d_k, d_v = (piece(i) for i in range(15))
    pad = jnp.zeros((w_in.shape[0], SLAB - IDX_DIM - IDX_HEADS - N_HEADS), w_in.dtype)
    return jnp.concatenate([a_q, a_k, a_v, a_qi, b_u, b_v, c_q, c_k, c_v, d_q, d_k, d_v,
                            a_ki, a_w, c_f, pad], axis=-1).astype(BF16)


def _block_diag_tril(w_s, n, reps):
    tri = jnp.tril(jnp.ones((n, n), dtype=bool))
    w = jnp.where(tri[None], w_s[:, :n, :n], 0)
    eye = jnp.eye(reps, dtype=w.dtype)
    return jnp.einsum("ab,gts->gatbs", eye, w).reshape(w.shape[0], reps * n, reps * n).astype(BF16)


def _round_up(a, b):
    return (a + b - 1) // b * b


def _layer(x, mod, lp, past, prefix, final_g, *, final_norm, past_len):
    bsz, t, d = x.shape
    is_prompt = past is None
    if is_prompt:
        nb, tt = 1, min(t, 256)
        n = min(t, B_CHUNK)
        reps = tt // n
        pos_rows = jnp.arange(t)
    else:
        nb, tt = bsz, t
        n = min(t, B_CHUNK)
        reps = (nb * tt) // n
        pos_rows = jnp.tile(past_len + jnp.arange(t), nb)
    tables = _rope_tables(pos_rows)
    ws = _block_diag_tril(lp["w_s"], n, reps)
    bs = jnp.tile(jnp.repeat(lp["b_s"][:, :n].T, HEAD_DIM, axis=1), (reps, 1))
    (qa, ka, va, qi, sm, ob, vb, qc, kc, vc, qd, kd, vd) = _proj(
        x, mod, lp["g1"], lp["w_in"], tables, lp["fb"], lp["lng"], lp["lnb"], ws, bs, nb=nb, tt=tt)

    l_valid = past_len + t
    if is_prompt:
        tk = min(512, t)
        lpad = _round_up(l_valid, tk)
        keys = lambda new, old: new
        ksm = sm
        n_sel = min(TOPK_MAX, t // 4)
        tq_a, tq_c, tq_d, tk_d = min(256, t), min(256, t), min(128, t), min(256, t)
    else:
        tk = 256
        lpad = _round_up(l_valid, tk)
        padrows = lpad - l_valid

        def keys(new, old):
            old = old.reshape(bsz, past_len, -1)
            return jnp.concatenate([old, new, jnp.zeros((bsz, padrows, new.shape[-1]), new.dtype)], axis=1)

        pk_a, pv_a, pki, pk_c, pv_c, plogf, pk_d, pv_d = past
        old_sm = jnp.concatenate(
            [pki, jnp.zeros((bsz, past_len, SLAB_F - IDX_DIM), F32), plogf.astype(F32),
             jnp.zeros((bsz, past_len, SLAB - SLAB_F - N_HEADS), F32)], axis=-1)
        ksm = keys(sm, old_sm)
        n_sel = min(TOPK_MAX, l_valid // 4)
        tq_a = tq_c = tq_d = t
        tk_d = tk
    p = past if past is not None else (None,) * 8
    k_a, v_a = keys(ka, p[0]), keys(va, p[1])
    k_c, v_c = keys(kc, p[3]), keys(vc, p[4])
    k_d, v_d = keys(kd, p[6]), keys(vd, p[7])

    oa = _dsa(qa, qi, sm, k_a, v_a, ksm, tq=tq_a, tk=tk, past_len=past_len, l_valid=l_valid, n_sel=n_sel)
    kaug, fs = _foxprep(ksm, k_c, tl=min(256, lpad))
    oc = _fox(qc, fs, kaug, v_c, tq=tq_c, tk=tk, past_len=past_len)
    od = _sb(qd, k_d, v_d, tq=tq_d, tk=tk_d, past_len=past_len)

    mtt = min(t, 512) if is_prompt else tt
    x = _merge(x, mod, lp["g1"], oa, ob, oc, od, lp["w_gate"], lp["b_gate"], lp["w_branch"], lp["w_out"],
               nb=nb, tt=mtt)
    x, conv_state = _ffn(x, mod, lp["g2"], prefix, lp["w_up"], lp["w_conv"], lp["b_conv"], lp["w_down"],
                         final_g, nb=nb, tt=tt, final_norm=final_norm)
    heads = lambda a: a.reshape(bsz, t, N_HEADS, HEAD_DIM)
    rows = (heads(ka), heads(va), sm[..., SLAB_KI:SLAB_KI + IDX_DIM], vb, heads(kc), heads(vc),
            sm[..., SLAB_F:SLAB_F + N_HEADS], heads(kd), heads(vd), conv_state)
    return x, rows


def kernel(x_prompt, x_sample, cache_a_k, cache_a_v, cache_a_kidx, cache_c_k, cache_c_v, cache_c_logf, cache_d_k, cache_d_v, state_ffn_conv, c_prompt, c_sample, norm1_g, norm2_g, w_mod, b_mod, w_in, f_bias, lnv_g, lnv_b, w_spatial, b_spatial, w_branch, w_gate, b_gate, w_out, w_up, w_conv, b_conv, w_down, final_g):
    depth = w_in.shape[0]
    bsz, _, d = x_prompt.shape
    dbsz = x_sample.shape[0]
    past_len = cache_a_k.shape[2]
    mods = _modulation(jnp.concatenate([c_prompt, c_sample], axis=0), w_mod, b_mod)
    mods = mods.reshape(depth, bsz + dbsz, 6, d)
    prefix = jnp.zeros((bsz, CONV_W - 1, w_up.shape[-1]), x_prompt.dtype)
    fg = final_g.reshape(1, d)
    xp, xs = x_prompt, x_sample
    rows_p, rows_s = [], []
    for l in range(depth):
        fb = jnp.zeros((1, SLAB), F32).at[0, SLAB_F:SLAB_F + N_HEADS].set(f_bias[l].astype(F32))
        lp = dict(g1=norm1_g[l].reshape(1, d), g2=norm2_g[l].reshape(1, d), w_in=_layout_w_in(w_in[l]), fb=fb,
                  lng=lnv_g[l].reshape(1, BRANCH), lnb=lnv_b[l].reshape(1, BRANCH),
                  w_s=w_spatial[l], b_s=b_spatial[l], w_branch=w_branch[l].astype(BF16),
                  w_gate=w_gate[l].astype(BF16), b_gate=b_gate[l].reshape(1, -1), w_out=w_out[l].astype(BF16),
                  w_up=w_up[l].astype(BF16), w_conv=w_conv[l], b_conv=b_conv[l].reshape(1, -1),
                  w_down=w_down[l].astype(BF16))
        past = (cache_a_k[l], cache_a_v[l], cache_a_kidx[l], cache_c_k[l], cache_c_v[l], cache_c_logf[l],
                cache_d_k[l], cache_d_v[l])
        last = l == depth - 1
        xp, new_p = _layer(xp, mods[l, :bsz], lp, None, prefix, fg, final_norm=last, past_len=0)
        xs, new_s = _layer(xs, mods[l, bsz:], lp, past, state_ffn_conv[l], fg, final_norm=last, past_len=past_len)
        rows_p.append(new_p)
        rows_s.append(new_s)

    def stacked(rows, i):
        return jnp.stack([r[i] for r in rows], axis=0)

    return (xp, xs,
            stacked(rows_p, 0), stacked(rows_s, 0),
            stacked(rows_p, 1), stacked(rows_s, 1),
            stacked(rows_p, 2), stacked(rows_s, 2),
            stacked(rows_s, 3),
            stacked(rows_p, 4), stacked(rows_s, 4),
            stacked(rows_p, 5), stacked(rows_s, 5),
            stacked(rows_p, 6), stacked(rows_s, 6),
            stacked(rows_p, 7), stacked(rows_s, 7),
            stacked(rows_p, 8), stacked(rows_s, 8),
            stacked(rows_p, 9), stacked(rows_s, 9))
```

```python
import functools

import jax
import jax.numpy as jnp
from jax import lax
from jax.experimental import pallas as pl
from jax.experimental.pallas import tpu as pltpu

F32 = jnp.float32
BF16 = jnp.bfloat16
I32 = jnp.int32
I16 = jnp.int16

HEAD_DIM = 64
N_HEADS = 4
BRANCH = N_HEADS * HEAD_DIM
IDX_HEADS = 8
IDX_DIM = 32
CHUNK = 64
TOPK_MAX = 256
B_CHUNK = 128
N_BRANCH = 4
CONV_W = 3
ROPE_THETA = 10000.0
EPS = 1e-6
LOG2E = 1.4426950408889634
Q_SCALE = HEAD_DIM ** -0.5 * LOG2E

SLAB = 128
SLAB_KI = 0
SLAB_W = 32
SLAB_F = 40
AUG = 128
NEG = -1e30
INT_MIN = -2 ** 31
I16_MIN = -2 ** 15
SB_CUTOFF = 220.0
Q_PAD = 128
VMEM_LIMIT = 56 * 1024 * 1024


def _params(n_axes, vmem=VMEM_LIMIT):
    return pltpu.CompilerParams(dimension_semantics=("arbitrary",) * n_axes, vmem_limit_bytes=vmem)


def _dot(a, b):
    return jnp.dot(a, b, preferred_element_type=F32)


def _dot_t(a, b):
    return lax.dot_general(a, b, (((1,), (1,)), ((), ())), preferred_element_type=F32)


def _split2(x):
    hi = x.astype(BF16)
    lo = (x - hi.astype(F32)).astype(BF16)
    return hi, lo


def _split3(x):
    x1 = x.astype(BF16)
    r = x - x1.astype(F32)
    x2 = r.astype(BF16)
    x3 = (r - x2.astype(F32)).astype(BF16)
    return x1, x2, x3


def _softplus(z):
    return jnp.maximum(z, 0.0) + jnp.log1p(jnp.exp(-jnp.abs(z)))


def _rms(x, g):
    ms = jnp.mean(x * x, axis=-1, keepdims=True)
    return x * lax.rsqrt(ms + EPS) * g


def _rope(x, cos, sin_signed, half):
    width = x.shape[-1]
    lane = lax.broadcasted_iota(I32, x.shape, 1)
    fwd = pltpu.roll(x, half, axis=1)
    bwd = pltpu.roll(x, width - half, axis=1)
    rot = jnp.where((lane % (2 * half)) < half, bwd, fwd)
    return x * cos + rot * sin_signed


def _mod_kernel(c_ref, w_ref, b_ref, o_ref):
    c = c_ref[...]
    a = c * jax.nn.sigmoid(c)
    a1, a2 = _split2(a)
    w1, w2 = _split2(w_ref[0])
    o_ref[0] = _dot(a1, w1) + _dot(a1, w2) + _dot(a2, w1) + b_ref[0]


def _modulation(c_all, w_mod, b_mod):
    depth, d, n = w_mod.shape
    rows = c_all.shape[0]
    tn = 1536
    return pl.pallas_call(
        _mod_kernel,
        grid=(depth, n // tn),
        in_specs=[pl.BlockSpec((rows, d), lambda l, j: (0, 0)),
                  pl.BlockSpec((1, d, tn), lambda l, j: (l, 0, j)),
                  pl.BlockSpec((1, 1, tn), lambda l, j: (l, 0, j))],
        out_specs=pl.BlockSpec((1, rows, tn), lambda l, j: (l, 0, j)),
        out_shape=jax.ShapeDtypeStruct((depth, rows, n), F32),
        compiler_params=_params(2),
        name="modulation",
    )(c_all, w_mod, b_mod.reshape(depth, 1, n))


def _proj_kernel(x_ref, mod_ref, g1_ref, w_ref, cosa_ref, sina_ref, cosi_ref, sini_ref, coss_ref, sins_ref,
                 fb_ref, lng_ref, lnb_ref, ws_ref, bs_ref,
                 qa_ref, ka_ref, va_ref, qi_ref, sm_ref, ob_ref, vb_ref,
                 qc_ref, kc_ref, vc_ref, qd_ref, kd_ref, vd_ref, *, nb, tt):
    m = nb * tt
    d = x_ref.shape[-1]
    x = x_ref[...]
    mod = mod_ref[...]
    h = _rms(x, g1_ref[...]) * (1.0 + mod[:, 1:2, :]) + mod[:, 0:1, :]
    hb = h.reshape(m, d).astype(BF16)

    def col(j):
        return _dot(hb, w_ref[:, j * BRANCH:(j + 1) * BRANCH])

    def put(ref, val):
        ref[...] = val.reshape(nb, tt, val.shape[-1]).astype(ref.dtype)

    cosa, sina = cosa_ref[...], sina_ref[...]
    put(qa_ref, _rope(col(0), cosa, sina, HEAD_DIM // 2) * Q_SCALE)
    put(ka_ref, _rope(col(1), cosa, sina, HEAD_DIM // 2))
    put(va_ref, col(2))
    put(qi_ref, _rope(col(3), cosi_ref[...], sini_ref[...], IDX_DIM // 2))
    u = jax.nn.gelu(col(4))
    vg = jax.nn.gelu(col(5))
    mu = jnp.mean(vg, axis=-1, keepdims=True)
    vc_ = vg - mu
    var = jnp.mean(vc_ * vc_, axis=-1, keepdims=True)
    vb = vc_ * lax.rsqrt(var + EPS) * lng_ref[...] + lnb_ref[...]
    put(vb_ref, vb)
    vbb = vb.astype(BF16)
    group = lax.broadcasted_iota(I32, (m, BRANCH), 1) // HEAD_DIM
    mixed = jnp.zeros((m, BRANCH), F32)
    for g in range(N_HEADS):
        mixed = jnp.where(group == g, _dot(ws_ref[g], vbb), mixed)
    put(ob_ref, u * (mixed + bs_ref[...]))
    put(qc_ref, col(6) * Q_SCALE)
    put(kc_ref, col(7))
    put(vc_ref, col(8))
    put(qd_ref, col(9) * Q_SCALE)
    put(kd_ref, col(10))
    put(vd_ref, col(11))
    sm = _dot(hb, w_ref[:, 12 * BRANCH:12 * BRANCH + SLAB])
    lane = lax.broadcasted_iota(I32, sm.shape, 1)
    roped = _rope(sm, coss_ref[...], sins_ref[...], IDX_DIM // 2)
    logf = -_softplus(-(sm + fb_ref[...]))
    is_f = jnp.where(lane >= SLAB_F, jnp.where(lane < SLAB_F + N_HEADS, 1, 0), 0)
    put(sm_ref, jnp.where(is_f == 1, logf, roped))


def _proj(x, mod, g1, w_in, tables, fb, lng, lnb, ws, bs, *, nb, tt):
    bsz, t, d = x.shape
    m = nb * tt
    n_t = t // tt
    row_blk = lambda w: pl.BlockSpec((m, w), lambda ti, bi: (ti, 0))
    tok = lambda w: pl.BlockSpec((nb, tt, w), lambda ti, bi: (bi, ti, 0))
    const2 = lambda a: pl.BlockSpec(a.shape, lambda ti, bi: (0,) * a.ndim)
    cosa, sina, cosi, sini, coss, sins = tables
    out_dtypes = (BF16, F32, F32, F32, F32, BF16, F32, BF16, F32, F32, BF16, F32, F32)
    out_widths = (BRANCH, BRANCH, BRANCH, BRANCH, SLAB) + (BRANCH,) * 8
    return pl.pallas_call(
        functools.partial(_proj_kernel, nb=nb, tt=tt),
        grid=(n_t, bsz // nb),
        in_specs=[tok(d),
                  pl.BlockSpec((nb, 6, d), lambda ti, bi: (bi, 0, 0)),
                  const2(g1), const2(w_in),
                  row_blk(BRANCH), row_blk(BRANCH), row_blk(BRANCH), row_blk(BRANCH), row_blk(SLAB), row_blk(SLAB),
                  const2(fb), const2(lng), const2(lnb), const2(ws), const2(bs)],
        out_specs=[tok(w) for w in out_widths],
        out_shape=[jax.ShapeDtypeStruct((bsz, t, w), dt) for w, dt in zip(out_widths, out_dtypes)],
        compiler_params=_params(2),
        name="proj",
    )(x, mod, g1, w_in, cosa, sina, cosi, sini, coss, sins, fb, lng, lnb, ws, bs)


def _flash_init(tq):
    return tuple((jnp.full((1, tq), NEG, F32), jnp.zeros((1, tq), F32), jnp.zeros((HEAD_DIM, tq), F32))
                 for _ in range(N_HEADS))


def _flash_step(carry, scores, vt_sc, off, tk):
    ms = [jnp.maximum(carry[h][0], jnp.max(scores[h], axis=0, keepdims=True)) for h in range(N_HEADS)]
    ps = [jnp.exp2(scores[h] - ms[h]) for h in range(N_HEADS)]
    pvs = [_dot(vt_sc[h * HEAD_DIM:(h + 1) * HEAD_DIM, pl.ds(off, tk)], ps[h].astype(BF16))
           for h in range(N_HEADS)]
    new = []
    for h in range(N_HEADS):
        m_run, l_run, acc = carry[h]
        alpha = jnp.exp2(m_run - ms[h])
        new.append((ms[h], alpha * l_run + jnp.sum(ps[h], axis=0, keepdims=True), alpha * acc + pvs[h]))
    return tuple(new)


def _flash_finish(carry):
    out_t = jnp.concatenate([acc / l_run for (_, l_run, acc) in carry], axis=0)
    return out_t.T


def _masked_heads(q):
    head = lax.broadcasted_iota(I32, q.shape, 1) // HEAD_DIM
    return [jnp.where(head == h, q, jnp.zeros_like(q)) for h in range(N_HEADS)]


def _dsa_kernel(qa_ref, qi_ref, qsm_ref, k_ref, v_ref, ksm_ref, o_ref,
                kb_sc, vt_sc, kic_sc, lhs_sc, key_sc, hi_sc, lo_sc,
                *, tq, tk, past_len, l_valid, n_sel, t_valid):
    qb = pl.program_id(1)

    @pl.when(qb == 0)
    def _():
        kb_sc[...] = k_ref[0].astype(BF16)
        vt_sc[...] = v_ref[0].T.astype(BF16)
        ks = ksm_ref[0]
        lane = lax.broadcasted_iota(I32, ks.shape, 1)
        ki = jnp.where(lane < IDX_DIM, ks, 0.0)
        hi = ki.astype(BF16).astype(F32)
        lo = ki - hi
        kic_sc[...] = (hi + pltpu.roll(lo, IDX_DIM, axis=1) + pltpu.roll(hi, 2 * IDX_DIM, axis=1)).astype(BF16)

    qi = qi_ref[0]
    lane = lax.broadcasted_iota(I32, (tq, SLAB), 1)
    for h in range(IDX_HEADS):
        rolled = qi if h == 0 else pltpu.roll(qi, BRANCH - IDX_DIM * h, axis=1)
        piece = jnp.where(lane < IDX_DIM, rolled[:, :SLAB], 0.0)
        hi = piece.astype(BF16).astype(F32)
        lo = piece - hi
        lhs_sc[h] = (hi + pltpu.roll(hi, IDX_DIM, axis=1) + pltpu.roll(lo, 2 * IDX_DIM, axis=1)).astype(BF16)
    w_t = qsm_ref[0].T

    row0 = past_len + qb * tq
    qlane = lax.broadcasted_iota(I32, (1, tq), 1)
    pos = row0 + qlane
    limit = jnp.minimum((pos // CHUNK + 1) * CHUNK, l_valid)
    top_limit = jnp.minimum(((row0 + tq - 1) // CHUNK + 1) * CHUNK, l_valid)
    nblk = (top_limit + tk - 1) // tk

    def score_blk(j, _):
        off = pl.multiple_of(j * tk, tk)
        kc = kic_sc[pl.ds(off, tk), :]
        dots = [_dot_t(kc, lhs_sc[h]) for h in range(IDX_HEADS)]
        acc = jnp.zeros((tk, tq), F32)
        for h in range(IDX_HEADS):
            acc = acc + w_t[SLAB_W + h:SLAB_W + h + 1, :] * jnp.maximum(dots[h], 0.0)
        acc = jnp.where(acc == 0.0, 0.0, acc)
        bits = pltpu.bitcast(acc, I32)
        key = jnp.where(bits < 0, bits ^ 0x7FFFFFFF, bits)
        kidx = off + lax.broadcasted_iota(I32, (tk, tq), 0)
        key = jnp.where(kidx < limit, key, INT_MIN)
        key_sc[pl.ds(off, tk), :] = key
        hi_sc[pl.ds(off, tk), :] = (key >> 16).astype(I16)
        lo_sc[pl.ds(off, tk), :] = ((key & 0xFFFF) + I16_MIN).astype(I16)
        return 0

    lax.fori_loop(0, nblk, score_blk, 0)

    def count16(ref, cand):
        cand_b = jnp.broadcast_to(cand, (16, tq)).astype(I16)

        def body(j, part):
            off = pl.multiple_of(j * tk, tk)
            kb = ref[pl.ds(off, tk), :]
            for c in range(tk // 16):
                part = part + jnp.where(kb[c * 16:(c + 1) * 16, :] >= cand_b, jnp.int16(1), jnp.int16(0))
            return part

        part = lax.fori_loop(0, nblk, body, jnp.zeros((16, tq), I16))
        return jnp.sum(part.astype(F32), axis=0, keepdims=True)

    def count32(cand):
        cand_b = jnp.broadcast_to(cand, (8, tq))

        def body(j, part):
            off = pl.multiple_of(j * tk, tk)
            kb = key_sc[pl.ds(off, tk), :]
            for c in range(tk // 8):
                part = part + jnp.where(kb[c * 8:(c + 1) * 8, :] >= cand_b, 1.0, 0.0)
            return part

        part = lax.fori_loop(0, nblk, body, jnp.zeros((8, tq), F32))
        return jnp.sum(part, axis=0, keepdims=True)

    def kth16(ref, want):
        t = jnp.where(count16(ref, jnp.zeros((1, tq), I32)) >= want, 0, I16_MIN).astype(I32)

        def search(i, t):
            cand = t | jnp.left_shift(jnp.int32(1), 14 - i)
            return jnp.where(count16(ref, cand) >= want, cand, t)

        return lax.fori_loop(0, 15, search, t)

    kf = float(n_sel)
    t_hi = kth16(hi_sc, kf)
    above = jnp.where(t_hi >= -I16_MIN - 1, 0.0, count16(hi_sc, t_hi + 1))
    t_hi_b = jnp.broadcast_to(t_hi, (16, tq)).astype(I16)

    def keep_equal_hi(j, _):
        off = pl.multiple_of(j * tk, tk)
        for c in range(tk // 16):
            rows = pl.ds(off + c * 16, 16)
            lo_sc[rows, :] = jnp.where(hi_sc[rows, :] == t_hi_b, lo_sc[rows, :], jnp.int16(I16_MIN))
        return 0

    lax.fori_loop(0, nblk, keep_equal_hi, 0)
    t_lo = kth16(lo_sc, kf - above)
    thr = t_hi * 65536 + (t_lo - I16_MIN)
    thr = jnp.maximum(thr, INT_MIN + 1)
    cnt_ge = count32(thr)
    cnt_gt = count32(thr + 1)
    room = kf - cnt_gt
    real_q = qlane < (t_valid - qb * tq)
    any_tie = jnp.max(jnp.where(real_q, jnp.where(cnt_ge > kf, 1.0, 0.0), 0.0)) > 0.0

    @pl.when(any_tie)
    def _():
        ri = lax.broadcasted_iota(I32, (tk, tk), 0)
        ci = lax.broadcasted_iota(I32, (tk, tk), 1)
        upto = jnp.where(ci <= ri, 1.0, 0.0).astype(BF16)

        def drop_late_ties(j, seen):
            off = pl.multiple_of(j * tk, tk)
            kb = key_sc[pl.ds(off, tk), :]
            eq = jnp.where(kb == thr, 1.0, 0.0)
            rank = _dot(upto, eq.astype(BF16)) + seen
            key_sc[pl.ds(off, tk), :] = jnp.where(eq * rank > room, INT_MIN, kb)
            return seen + jnp.sum(eq, axis=0, keepdims=True)

        lax.fori_loop(0, nblk, drop_late_ties, jnp.zeros((1, tq), F32))

    qh = _masked_heads(qa_ref[0])

    def attend(j, carry):
        off = pl.multiple_of(j * tk, tk)
        kblk = kb_sc[pl.ds(off, tk), :]
        keep = key_sc[pl.ds(off, tk), :] >= thr
        scores = [jnp.where(keep, _dot_t(kblk, qh[h]), NEG) for h in range(N_HEADS)]
        return _flash_step(carry, scores, vt_sc, off, tk)

    carry = lax.fori_loop(0, nblk, attend, _flash_init(tq))
    o_ref[0] = _flash_finish(carry).astype(o_ref.dtype)


def _dsa(qa, qi, qsm, k, v, ksm, *, tq, tk, past_len, l_valid, n_sel, t_valid):
    bsz, t, _ = qa.shape
    lp = k.shape[1]
    qspec = lambda w: pl.BlockSpec((1, tq, w), lambda b, i: (b, i, 0))
    kspec = lambda w: pl.BlockSpec((1, lp, w), lambda b, i: (b, 0, 0))
    return pl.pallas_call(
        functools.partial(_dsa_kernel, tq=tq, tk=tk, past_len=past_len, l_valid=l_valid, n_sel=n_sel,
                          t_valid=t_valid),
        grid=(bsz, t // tq),
        in_specs=[qspec(BRANCH), qspec(BRANCH), qspec(SLAB), kspec(BRANCH), kspec(BRANCH), kspec(SLAB)],
        out_specs=qspec(BRANCH),
        out_shape=jax.ShapeDtypeStruct((bsz, t, BRANCH), BF16),
        scratch_shapes=[pltpu.VMEM((lp, BRANCH), BF16), pltpu.VMEM((BRANCH, lp), BF16),
                        pltpu.VMEM((lp, SLAB), BF16), pltpu.VMEM((IDX_HEADS, tq, SLAB), BF16),
                        pltpu.VMEM((lp, tq), I32), pltpu.VMEM((lp, tq), I16), pltpu.VMEM((lp, tq), I16)],
        compiler_params=_params(2),
        name="dsa",
    )(qa, qi, qsm, k, v, ksm)


def _foxprep_kernel(sm_ref, k_ref, kaug_ref, fs_ref, carry_sc, *, tl):
    @pl.when(pl.program_id(1) == 0)
    def _():
        carry_sc[...] = jnp.zeros_like(carry_sc)

    sm = sm_ref[0]
    ri = lax.broadcasted_iota(I32, (tl, tl), 0)
    ci = lax.broadcasted_iota(I32, (tl, tl), 1)
    lower = jnp.where(ri >= ci, 1.0, 0.0).astype(BF16)
    s1, s2, s3 = _split3(sm)
    fsum = _dot(lower, s1) + _dot(lower, s2) + _dot(lower, s3) + carry_sc[...]
    carry_sc[...] = fsum[tl - 1:tl, :]
    fsum = fsum * LOG2E
    fs_ref[0] = fsum
    f1, f2, f3 = (p.astype(F32) for p in _split3(fsum))
    k = k_ref[0]
    lane = lax.broadcasted_iota(I32, (tl, AUG), 1)
    for h in range(N_HEADS):
        rolled = k if h == 0 else pltpu.roll(k, BRANCH - HEAD_DIM * h, axis=1)
        piece = rolled[:, :AUG]
        c = SLAB_F + h
        extra = jnp.where(lane < HEAD_DIM + 3, 1.0,
                          jnp.where(lane == HEAD_DIM + 3, -f1[:, c:c + 1],
                                    jnp.where(lane == HEAD_DIM + 4, -f2[:, c:c + 1],
                                              jnp.where(lane == HEAD_DIM + 5, -f3[:, c:c + 1], 0.0))))
        kaug_ref[0, :, h * AUG:(h + 1) * AUG] = jnp.where(lane < HEAD_DIM, piece, extra).astype(BF16)


def _foxprep(sm, k, *, tl):
    bsz, lp, _ = k.shape
    return pl.pallas_call(
        functools.partial(_foxprep_kernel, tl=tl),
        grid=(bsz, lp // tl),
        in_specs=[pl.BlockSpec((1, tl, SLAB), lambda b, i: (b, i, 0)),
                  pl.BlockSpec((1, tl, BRANCH), lambda b, i: (b, i, 0))],
        out_specs=[pl.BlockSpec((1, tl, N_HEADS * AUG), lambda b, i: (b, i, 0)),
                   pl.BlockSpec((1, tl, SLAB), lambda b, i: (b, i, 0))],
        out_shape=[jax.ShapeDtypeStruct((bsz, lp, N_HEADS * AUG), BF16),
                   jax.ShapeDtypeStruct((bsz, lp, SLAB), F32)],
        scratch_shapes=[pltpu.VMEM((1, SLAB), F32)],
        compiler_params=_params(2),
        name="foxprep",
    )(sm, k)


def _fox_kernel(q_ref, fq_ref, kaug_ref, v_ref, o_ref, vt_sc, *, tq, tk, past_len):
    qb = pl.program_id(1)

    @pl.when(qb == 0)
    def _():
        vt_sc[...] = v_ref[0].T.astype(BF16)

    q = q_ref[0].astype(F32)
    f1, f2, f3 = (p.astype(F32) for p in _split3(fq_ref[0]))
    lane = lax.broadcasted_iota(I32, (tq, AUG), 1)
    row0 = past_len + qb * tq
    pos = row0 + lax.broadcasted_iota(I32, (1, tq), 1)
    n_full = row0 // tk
    n_all = (row0 + tq + tk - 1) // tk

    def q_aug(h):
        rolled = q if h == 0 else pltpu.roll(q, BRANCH - HEAD_DIM * h, axis=1)
        c = SLAB_F + h
        extra = jnp.where(lane == HEAD_DIM, f1[:, c:c + 1],
                          jnp.where(lane == HEAD_DIM + 1, f2[:, c:c + 1],
                                    jnp.where(lane == HEAD_DIM + 2, f3[:, c:c + 1],
                                              jnp.where(lane < HEAD_DIM + 6, 1.0, 0.0))))
        return jnp.where(lane < HEAD_DIM, rolled[:, :AUG], extra).astype(BF16)

    qh = [q_aug(h) for h in range(N_HEADS)]

    def step(j, carry, masked):
        off = pl.multiple_of(j * tk, tk)
        scores = [_dot_t(kaug_ref[0, pl.ds(off, tk), h * AUG:(h + 1) * AUG], qh[h]) for h in range(N_HEADS)]
        if masked:
            causal = off + lax.broadcasted_iota(I32, (tk, tq), 0) <= pos
            scores = [jnp.where(causal, s, NEG) for s in scores]
        return _flash_step(carry, scores, vt_sc, off, tk)

    carry = lax.fori_loop(0, n_full, lambda j, c: step(j, c, False), _flash_init(tq))
    carry = lax.fori_loop(n_full, n_all, lambda j, c: step(j, c, True), carry)
    o_ref[0] = _flash_finish(carry).astype(o_ref.dtype)


def _fox(q, fs, kaug, v, *, tq, tk, past_len):
    bsz, t, _ = q.shape
    lp = v.shape[1]
    q_off = past_len // tq
    return pl.pallas_call(
        functools.partial(_fox_kernel, tq=tq, tk=tk, past_len=past_len),
        grid=(bsz, t // tq),
        in_specs=[pl.BlockSpec((1, tq, BRANCH), lambda b, i: (b, i, 0)),
                  pl.BlockSpec((1, tq, SLAB), lambda b, i: (b, i + q_off, 0)),
                  pl.BlockSpec((1, lp, N_HEADS * AUG), lambda b, i: (b, 0, 0)),
                  pl.BlockSpec((1, lp, BRANCH), lambda b, i: (b, 0, 0))],
        out_specs=pl.BlockSpec((1, tq, BRANCH), lambda b, i: (b, i, 0)),
        out_shape=jax.ShapeDtypeStruct((bsz, t, BRANCH), BF16),
        scratch_shapes=[pltpu.VMEM((BRANCH, lp), BF16)],
        compiler_params=_params(2),
        name="fox",
    )(q, fs, kaug, v)


def _sb_kernel(q_ref, k_ref, v_ref, o_ref, kb_sc, vt_sc, *, tq, tk, past_len):
    qb = pl.program_id(1)

    @pl.when(qb == 0)
    def _():
        kb_sc[...] = k_ref[0].astype(BF16)
        vt_sc[...] = v_ref[0].T.astype(BF16)

    qh = _masked_heads(q_ref[0])
    row0 = past_len + qb * tq
    pos = row0 + lax.broadcasted_iota(I32, (1, tq), 1)
    j_top = (row0 + tq - 1) // tk
    ri = lax.broadcasted_iota(I32, (tk, tk), 0)
    ci = lax.broadcasted_iota(I32, (tk, tk), 1)
    after = jnp.where(ci > ri, 1.0, 0.0).astype(BF16)

    def cond(carry):
        j, state = carry
        live = state[0][0]
        for h in range(1, N_HEADS):
            live = jnp.maximum(live, state[h][0])
        return jnp.logical_and(j >= 0, jnp.max(live) > -SB_CUTOFF)

    def body(carry):
        j, state = carry
        off = pl.multiple_of(j * tk, tk)
        kblk = kb_sc[pl.ds(off, tk), :]
        zs = [_dot_t(kblk, qh[h]) for h in range(N_HEADS)]
        strict = off + lax.broadcasted_iota(I32, (tk, tq), 0) < pos
        sps = [jnp.maximum(z, 0.0) + jnp.log2(1.0 + jnp.exp2(-jnp.abs(z))) for z in zs]
        keeps = [jnp.where(strict, -sp, 0.0) for sp in sps]
        laters = []
        for h in range(N_HEADS):
            k_hi, k_lo = _split2(keeps[h])
            laters.append(_dot(after, k_hi) + _dot(after, k_lo) + state[h][0])
        ws = [jnp.where(strict, jnp.exp2(zs[h] - sps[h] + laters[h]), 0.0) for h in range(N_HEADS)]
        pvs = [_dot(vt_sc[h * HEAD_DIM:(h + 1) * HEAD_DIM, pl.ds(off, tk)], ws[h].astype(BF16))
               for h in range(N_HEADS)]
        new = tuple((state[h][0] + jnp.sum(keeps[h], axis=0, keepdims=True), state[h][1] + pvs[h])
                    for h in range(N_HEADS))
        return j - 1, new

    init = tuple((jnp.zeros((1, tq), F32), jnp.zeros((HEAD_DIM, tq), F32)) for _ in range(N_HEADS))
    _, state = lax.while_loop(cond, body, (j_top, init))
    out_t = jnp.concatenate([acc for (_, acc) in state], axis=0)
    o_ref[0] = out_t.T.astype(o_ref.dtype)


def _sb(q, k, v, *, tq, tk, past_len):
    bsz, t, _ = q.shape
    lp = k.shape[1]
    qspec = pl.BlockSpec((1, tq, BRANCH), lambda b, i: (b, i, 0))
    kspec = pl.BlockSpec((1, lp, BRANCH), lambda b, i: (b, 0, 0))
    return pl.pallas_call(
        functools.partial(_sb_kernel, tq=tq, tk=tk, past_len=past_len),
        grid=(bsz, t // tq),
        in_specs=[qspec, kspec, kspec],
        out_specs=qspec,
        out_shape=jax.ShapeDtypeStruct((bsz, t, BRANCH), BF16),
        scratch_shapes=[pltpu.VMEM((lp, BRANCH), BF16), pltpu.VMEM((BRANCH, lp), BF16)],
        compiler_params=_params(2),
        name="sb",
    )(q, k, v)


def _merge_kernel(x_ref, mod_ref, g1_ref, oa_ref, ob_ref, oc_ref, od_ref, wg_ref, bg_ref, wb_ref, wo_ref,
                  xo_ref, *, nb, tt):
    m = nb * tt
    d = x_ref.shape[-1]
    x = x_ref[...]
    mod = mod_ref[...]
    h = _rms(x, g1_ref[...]) * (1.0 + mod[:, 1:2, :]) + mod[:, 0:1, :]
    hb = h.reshape(m, d).astype(BF16)
    merged = None
    for i, o_ref in enumerate((oa_ref, ob_ref, oc_ref, od_ref)):
        gate = jax.nn.sigmoid(_dot(hb, wg_ref[:, i * d:(i + 1) * d]) + bg_ref[:, i * d:(i + 1) * d])
        term = gate * _dot(o_ref[...].reshape(m, BRANCH), wb_ref[i])
        merged = term if merged is None else merged + term
    y = _dot(merged.astype(BF16), wo_ref[...])
    xo_ref[...] = x + mod[:, 2:3, :] * y.reshape(nb, tt, d)


def _merge(x, mod, g1, oa, ob, oc, od, w_gate, b_gate, w_branch, w_out, *, nb, tt):
    bsz, t, d = x.shape
    tok = lambda w: pl.BlockSpec((nb, tt, w), lambda bi, ti: (bi, ti, 0))
    const = lambda a: pl.BlockSpec(a.shape, lambda bi, ti: (0,) * a.ndim)
    return pl.pallas_call(
        functools.partial(_merge_kernel, nb=nb, tt=tt),
        grid=(bsz // nb, t // tt),
        in_specs=[tok(d), pl.BlockSpec((nb, 6, d), lambda bi, ti: (bi, 0, 0)), const(g1),
                  tok(BRANCH), tok(BRANCH), tok(BRANCH), tok(BRANCH),
                  const(w_gate), const(b_gate), const(w_branch), const(w_out)],
        out_specs=tok(d),
        out_shape=jax.ShapeDtypeStruct((bsz, t, d), F32),
        compiler_params=_params(2),
        name="merge",
    )(x, mod, g1, oa, ob, oc, od, w_gate, b_gate, w_branch, w_out)


def _ffn_kernel(x_ref, mod_ref, g2_ref, pfx_ref, wu_ref, wc_ref, bc_ref, wd_ref, fg_ref,
                xo_ref, st_ref, carry_sc, *, nb, tt, cw, final_norm):
    m = nb * tt
    d = x_ref.shape[-1]
    dff = wd_ref.shape[0]
    ti = pl.program_id(1)

    @pl.when(ti == 0)
    def _():
        carry_sc[...] = pfx_ref[...]

    x = x_ref[...]
    mod = mod_ref[...]
    h = _rms(x, g2_ref[...]) * (1.0 + mod[:, 4:5, :]) + mod[:, 3:4, :]
    hb = h.reshape(m, d).astype(BF16)
    rowi = lax.broadcasted_iota(I32, (nb, tt, cw), 1)

    def conv_cols(c0):
        up = _dot(hb, wu_ref[:, c0:c0 + cw]).reshape(nb, tt, cw)
        prev0 = carry_sc[:, 0:1, c0:c0 + cw]
        prev1 = carry_sc[:, 1:2, c0:c0 + cw]
        back1 = jnp.where(rowi == 0, prev1, pltpu.roll(up, 1, axis=1))
        back2 = jnp.where(rowi == 0, prev0, jnp.where(rowi == 1, prev1, pltpu.roll(up, 2, axis=1)))
        carry_sc[:, :, c0:c0 + cw] = up[:, tt - 2:tt, :]
        wc = wc_ref[:, c0:c0 + cw]
        conv = back2 * wc[0:1, :] + back1 * wc[1:2, :] + up * wc[2:3, :] + bc_ref[:, c0:c0 + cw]
        return conv.reshape(m, cw)

    acc = jnp.zeros((m, d), F32)
    for j in range(dff // cw):
        gate = conv_cols(j * cw)
        val = conv_cols(dff + j * cw)
        act = (gate * jax.nn.sigmoid(gate) * val).astype(BF16)
        acc = acc + _dot(act, wd_ref[j * cw:(j + 1) * cw, :])
    xo = x + mod[:, 5:6, :] * acc.reshape(nb, tt, d)
    if final_norm:
        xo = _rms(xo, fg_ref[...])
    xo_ref[...] = xo

    @pl.when(ti == pl.num_programs(1) - 1)
    def _():
        st_ref[...] = carry_sc[...]


def _ffn(x, mod, g2, prefix, w_up, w_conv, b_conv, w_down, final_g, *, nb, tt, final_norm):
    bsz, t, d = x.shape
    dff = w_down.shape[0]
    cw = 256
    tok = pl.BlockSpec((nb, tt, d), lambda bi, ti: (bi, ti, 0))
    const = lambda a: pl.BlockSpec(a.shape, lambda bi, ti: (0,) * a.ndim)
    state = pl.BlockSpec((nb, CONV_W - 1, 2 * dff), lambda bi, ti: (bi, 0, 0))
    return pl.pallas_call(
        functools.partial(_ffn_kernel, nb=nb, tt=tt, cw=cw, final_norm=final_norm),
        grid=(bsz // nb, t // tt),
        in_specs=[tok, pl.BlockSpec((nb, 6, d), lambda bi, ti: (bi, 0, 0)), const(g2), state,
                  const(w_up), const(w_conv), const(b_conv), const(w_down), const(final_g)],
        out_specs=[tok, state],
        out_shape=[jax.ShapeDtypeStruct((bsz, t, d), F32),
                   jax.ShapeDtypeStruct((bsz, CONV_W - 1, 2 * dff), F32)],
        scratch_shapes=[pltpu.VMEM((nb, CONV_W - 1, 2 * dff), F32)],
        compiler_params=_params(2),
        name="ffn",
    )(x, mod, g2, prefix, w_up, w_conv, b_conv, w_down, final_g)


def _rope_tables(pos):
    def tab(half, heads, pad):
        inv = ROPE_THETA ** (-jnp.arange(half, dtype=F32) / half)
        ang = pos.astype(F32)[:, None] * inv[None, :]
        cos, sin = jnp.cos(ang), jnp.sin(ang)
        cos_t = jnp.tile(jnp.concatenate([cos, cos], axis=-1), (1, heads))
        sin_t = jnp.tile(jnp.concatenate([-sin, sin], axis=-1), (1, heads))
        if pad:
            cos_t = jnp.concatenate([cos_t, jnp.ones((pos.shape[0], pad), F32)], axis=-1)
            sin_t = jnp.concatenate([sin_t, jnp.zeros((pos.shape[0], pad), F32)], axis=-1)
        return cos_t, sin_t

    cosa, sina = tab(HEAD_DIM // 2, N_HEADS, 0)
    cosi, sini = tab(IDX_DIM // 2, IDX_HEADS, 0)
    coss, sins = tab(IDX_DIM // 2, 1, SLAB - IDX_DIM)
    return cosa, sina, cosi, sini, coss, sins


def _layout_w_in(w_in):
    sizes = (BRANCH, BRANCH, BRANCH, IDX_HEADS * IDX_DIM, IDX_DIM, IDX_HEADS, BRANCH, BRANCH,
             BRANCH, BRANCH, BRANCH, N_HEADS, BRANCH, BRANCH, BRANCH)
    offs = [0]
    for s in sizes:
        offs.append(offs[-1] + s)
    piece = lambda i: w_in[:, offs[i]:offs[i + 1]]
    a_q, a_k, a_v, a_qi, a_ki, a_w, b_u, b_v, c_q, c_k, c_v, c_f, d_q, d_k, d_v = (piece(i) for i in range(15))
    pad = jnp.zeros((w_in.shape[0], SLAB - IDX_DIM - IDX_HEADS - N_HEADS), w_in.dtype)
    return jnp.concatenate([a_q, a_k, a_v, a_qi, b_u, b_v, c_q, c_k, c_v, d_q, d_k, d_v,
                            a_ki, a_w, c_f, pad], axis=-1).astype(BF16)


def _block_diag_tril(w_s, n, reps):
    tri = jnp.tril(jnp.ones((n, n), dtype=bool))
    w = jnp.where(tri[None], w_s[:, :n, :n], 0)
    eye = jnp.eye(reps, dtype=w.dtype)
    return jnp.einsum("ab,gts->gatbs", eye, w).reshape(w.shape[0], reps * n, reps * n).astype(BF16)


def _round_up(a, b):
    return (a + b - 1) // b * b


def _layer(x, mod, lp, past, prefix, final_g, *, final_norm, past_len):
    bsz, t, d = x.shape
    is_prompt = past is None
    if is_prompt:
        nb, tt = 1, min(t, 256)
        n = min(t, B_CHUNK)
        reps = tt // n
        pos_rows = jnp.arange(t)
    else:
        nb, tt = bsz, t
        n = min(t, B_CHUNK)
        reps = (nb * tt) // n
        pos_rows = jnp.tile(past_len + jnp.arange(t), nb)
    tables = _rope_tables(pos_rows)
    ws = _block_diag_tril(lp["w_s"], n, reps)
    bs = jnp.tile(jnp.repeat(lp["b_s"][:, :n].T, HEAD_DIM, axis=1), (reps, 1))
    (qa, ka, va, qi, sm, ob, vb, qc, kc, vc, qd, kd, vd) = _proj(
        x, mod, lp["g1"], lp["w_in"], tables, lp["fb"], lp["lng"], lp["lnb"], ws, bs, nb=nb, tt=tt)

    l_valid = past_len + t
    tqp = _round_up(t, Q_PAD)
    if is_prompt:
        tk = min(512, t)
        lpad = _round_up(l_valid, tk)
        keys = lambda new, old: new
        ksm = sm
        n_sel = min(TOPK_MAX, t // 4)
        tq = min(256, tqp)
        tk_d = min(256, t)
    else:
        tk = tk_d = 256
        lpad = _round_up(past_len + tqp, tk)
        padrows = lpad - l_valid

        def keys(new, old):
            old = old.reshape(bsz, past_len, -1)
            return jnp.concatenate([old, new, jnp.zeros((bsz, padrows, new.shape[-1]), new.dtype)], axis=1)

        pk_a, pv_a, pki, pk_c, pv_c, plogf, pk_d, pv_d = past
        old_sm = jnp.concatenate(
            [pki, jnp.zeros((bsz, past_len, SLAB_F - IDX_DIM), F32), plogf.astype(F32),
             jnp.zeros((bsz, past_len, SLAB - SLAB_F - N_HEADS), F32)], axis=-1)
        ksm = keys(sm, old_sm)
        n_sel = min(TOPK_MAX, l_valid // 4)
        tq = tqp
    p = past if past is not None else (None,) * 8
    k_a, v_a = keys(ka, p[0]), keys(va, p[1])
    k_c, v_c = keys(kc, p[3]), keys(vc, p[4])
    k_d, v_d = keys(kd, p[6]), keys(vd, p[7])
    padq = lambda a: a if tqp == t else jnp.pad(a, ((0, 0), (0, tqp - t), (0, 0)))

    oa = _dsa(padq(qa), padq(qi), padq(sm), k_a, v_a, ksm, tq=tq, tk=tk, past_len=past_len, l_valid=l_valid,
              n_sel=n_sel, t_valid=t)[:, :t]
    kaug, fs = _foxprep(ksm, k_c, tl=min(256, lpad))
    oc = _fox(padq(qc), fs, kaug, v_c, tq=tq, tk=tk, past_len=past_len)[:, :t]
    od = _sb(padq(qd), k_d, v_d, tq=tq, tk=tk_d, past_len=past_len)[:, :t]

    mtt = min(t, 512) if is_prompt else tt
    x = _merge(x, mod, lp["g1"], oa, ob, oc, od, lp["w_gate"], lp["b_gate"], lp["w_branch"], lp["w_out"],
               nb=nb, tt=mtt)
    x, conv_state = _ffn(x, mod, lp["g2"], prefix, lp["w_up"], lp["w_conv"], lp["b_conv"], lp["w_down"],
                         final_g, nb=nb, tt=tt, final_norm=final_norm)
    heads = lambda a: a.reshape(bsz, t, N_HEADS, HEAD_DIM)
    rows = (heads(ka), heads(va), sm[..., SLAB_KI:SLAB_KI + IDX_DIM], vb, heads(kc), heads(vc),
            sm[..., SLAB_F:SLAB_F + N_HEADS], heads(kd), heads(vd), conv_state)
    return x, rows


def kernel(x_prompt, x_sample, cache_a_k, cache_a_v, cache_a_kidx, cache_c_k, cache_c_v, cache_c_logf, cache_d_k, cache_d_v, state_ffn_conv, c_prompt, c_sample, norm1_g, norm2_g, w_mod, b_mod, w_in, f_bias, lnv_g, lnv_b, w_spatial, b_spatial, w_branch, w_gate, b_gate, w_out, w_up, w_conv, b_conv, w_down, final_g):
    depth = w_in.shape[0]
    bsz, _, d = x_prompt.shape
    dbsz = x_sample.shape[0]
    past_len = cache_a_k.shape[2]
    mods = _modulation(jnp.concatenate([c_prompt, c_sample], axis=0), w_mod, b_mod)
    mods = mods.reshape(depth, bsz + dbsz, 6, d)
    prefix = jnp.zeros((bsz, CONV_W - 1, w_up.shape[-1]), x_prompt.dtype)
    fg = final_g.reshape(1, d)
    xp, xs = x_prompt, x_sample
    rows_p, rows_s = [], []
    for l in range(depth):
        fb = jnp.zeros((1, SLAB), F32).at[0, SLAB_F:SLAB_F + N_HEADS].set(f_bias[l].astype(F32))
        lp = dict(g1=norm1_g[l].reshape(1, d), g2=norm2_g[l].reshape(1, d), w_in=_layout_w_in(w_in[l]), fb=fb,
                  lng=lnv_g[l].reshape(1, BRANCH), lnb=lnv_b[l].reshape(1, BRANCH),
                  w_s=w_spatial[l], b_s=b_spatial[l], w_branch=w_branch[l].astype(BF16),
                  w_gate=w_gate[l].astype(BF16), b_gate=b_gate[l].reshape(1, -1), w_out=w_out[l].astype(BF16),
                  w_up=w_up[l].astype(BF16), w_conv=w_conv[l], b_conv=b_conv[l].reshape(1, -1),
                  w_down=w_down[l].astype(BF16))
        past = (cache_a_k[l], cache_a_v[l], cache_a_kidx[l], cache_c_k[l], cache_c_v[l], cache_c_logf[l],
                cache_d_k[l], cache_d_v[l])
        last = l == depth - 1
        xp, new_p = _layer(xp, mods[l, :bsz], lp, None, prefix, fg, final_norm=last, past_len=0)
        xs, new_s = _layer(xs, mods[l, bsz:], lp, past, state_ffn_conv[l], fg, final_norm=last, past_len=past_len)
        rows_p.append(new_p)
        rows_s.append(new_s)

    def stacked(rows, i):
        return jnp.stack([r[i] for r in rows], axis=0)

    return (xp, xs,
            stacked(rows_p, 0), stacked(rows_s, 0),
            stacked(rows_p, 1), stacked(rows_s, 1),
            stacked(rows_p, 2), stacked(rows_s, 2),
            stacked(rows_s, 3),
            stacked(rows_p, 4), stacked(rows_s, 4),
            stacked(rows_p, 5), stacked(rows_s, 5),
            stacked(rows_p, 6), stacked(rows_s, 6),
            stacked(rows_p, 7), stacked(rows_s, 7),
            stacked(rows_p, 8), stacked(rows_s, 8),
            stacked(rows_p, 9), stacked(rows_s, 9))
```

```python
import functools

import jax
import jax.numpy as jnp
from jax import lax
from jax.experimental import pallas as pl
from jax.experimental.pallas import tpu as pltpu

F32 = jnp.float32
BF16 = jnp.bfloat16
I32 = jnp.int32
I16 = jnp.int16

HEAD_DIM = 64
N_HEADS = 4
BRANCH = N_HEADS * HEAD_DIM
IDX_HEADS = 8
IDX_DIM = 32
CHUNK = 64
TOPK_MAX = 256
B_CHUNK = 128
N_BRANCH = 4
CONV_W = 3
ROPE_THETA = 10000.0
EPS = 1e-6
LOG2E = 1.4426950408889634
Q_SCALE = HEAD_DIM ** -0.5 * LOG2E

SLAB = 128
SLAB_KI = 0
SLAB_W = 32
SLAB_F = 40
AUG = 128
F_ROWS = 8
F_BLK = 256
N_T_GROUPS = 10
NEG = -1e30
INT_MIN = -2 ** 31
I16_MIN = -2 ** 15
SB_CUTOFF = 220.0
Q_PAD = 128
VMEM_LIMIT = 56 * 1024 * 1024


def _params(n_axes, vmem=VMEM_LIMIT):
    return pltpu.CompilerParams(dimension_semantics=("arbitrary",) * n_axes, vmem_limit_bytes=vmem)


def _dot(a, b):
    return jnp.dot(a, b, preferred_element_type=F32)


def _split2(x):
    hi = x.astype(BF16)
    lo = (x - hi.astype(F32)).astype(BF16)
    return hi, lo


def _split3(x):
    x1 = x.astype(BF16)
    r = x - x1.astype(F32)
    x2 = r.astype(BF16)
    x3 = (r - x2.astype(F32)).astype(BF16)
    return x1, x2, x3


def _softplus(z):
    return jnp.maximum(z, 0.0) + jnp.log1p(jnp.exp(-jnp.abs(z)))


def _rms(x, g):
    ms = jnp.mean(x * x, axis=-1, keepdims=True)
    return x * lax.rsqrt(ms + EPS) * g


def _rope_t(x, cos, sin_signed, half):
    rows = x.shape[0]
    row = lax.broadcasted_iota(I32, x.shape, 0)
    fwd = pltpu.roll(x, half, axis=0)
    bwd = pltpu.roll(x, rows - half, axis=0)
    rot = jnp.where((row % (2 * half)) < half, bwd, fwd)
    return x * cos + rot * sin_signed


def _mod_kernel(c_ref, w_ref, b_ref, o_ref):
    c = c_ref[...]
    a = c * jax.nn.sigmoid(c)
    a1, a2 = _split2(a)
    w1, w2 = _split2(w_ref[0])
    o_ref[0] = _dot(a1, w1) + _dot(a1, w2) + _dot(a2, w1) + b_ref[0]


def _modulation(c_all, w_mod, b_mod):
    depth, d, n = w_mod.shape
    rows = c_all.shape[0]
    tn = 1536
    return pl.pallas_call(
        _mod_kernel,
        grid=(depth, n // tn),
        in_specs=[pl.BlockSpec((rows, d), lambda l, j: (0, 0)),
                  pl.BlockSpec((1, d, tn), lambda l, j: (l, 0, j)),
                  pl.BlockSpec((1, 1, tn), lambda l, j: (l, 0, j))],
        out_specs=pl.BlockSpec((1, rows, tn), lambda l, j: (l, 0, j)),
        out_shape=jax.ShapeDtypeStruct((depth, rows, n), F32),
        compiler_params=_params(2),
        name="modulation",
    )(c_all, w_mod, b_mod.reshape(depth, 1, n))


def _proj_kernel(x_ref, mod_ref, g1_ref, wt_ref, wr_ref, cosa_ref, sina_ref, cosi_ref, sini_ref, coss_ref,
                 sins_ref, fb_ref, lng_ref, lnb_ref, ws_ref, bs_ref, *rest, nb, tt, n_alias):
    (ka_ref, va_ref, kc_ref, vc_ref, kd_ref, vd_ref,
     qa_ref, qc_ref, qd_ref, qi_ref, sm_ref, ob_ref, vb_ref) = rest[n_alias:]
    m = nb * tt
    d = x_ref.shape[-1]
    x = x_ref[...]
    mod = mod_ref[...]
    h = (_rms(x, g1_ref[...]) * (1.0 + mod[:, 1:2, :]) + mod[:, 0:1, :]).reshape(m, d)
    hb = h.astype(BF16)
    ht = h.T.astype(BF16)

    def col_t(j):
        return _dot(wt_ref[j * BRANCH:(j + 1) * BRANCH, :], ht)

    def put_t(ref, val):
        ref[...] = val.reshape(ref.shape).astype(ref.dtype)

    def put(ref, val):
        ref[...] = val.reshape(nb, tt, val.shape[-1]).astype(ref.dtype)

    cosa, sina = cosa_ref[...], sina_ref[...]
    put_t(qa_ref, _rope_t(col_t(0), cosa, sina, HEAD_DIM // 2) * Q_SCALE)
    put_t(ka_ref, _rope_t(col_t(1), cosa, sina, HEAD_DIM // 2))
    put_t(va_ref, col_t(2))
    put_t(qi_ref, _rope_t(col_t(3), cosi_ref[...], sini_ref[...], IDX_DIM // 2))
    put_t(qc_ref, col_t(4) * Q_SCALE)
    put_t(kc_ref, col_t(5))
    put_t(vc_ref, col_t(6))
    put_t(qd_ref, col_t(7) * Q_SCALE)
    put_t(kd_ref, col_t(8))
    put_t(vd_ref, col_t(9))
    sm = _dot(wt_ref[N_T_GROUPS * BRANCH:N_T_GROUPS * BRANCH + SLAB, :], ht)
    row = lax.broadcasted_iota(I32, sm.shape, 0)
    roped = _rope_t(sm, coss_ref[...], sins_ref[...], IDX_DIM // 2)
    logf = -_softplus(-(sm + fb_ref[...]))
    is_f = jnp.where(row >= SLAB_F, jnp.where(row < SLAB_F + N_HEADS, 1, 0), 0)
    put_t(sm_ref, jnp.where(is_f == 1, logf, roped))
    u = jax.nn.gelu(_dot(hb, wr_ref[:, :BRANCH]))
    vg = jax.nn.gelu(_dot(hb, wr_ref[:, BRANCH:]))
    mu = jnp.mean(vg, axis=-1, keepdims=True)
    vc_ = vg - mu
    var = jnp.mean(vc_ * vc_, axis=-1, keepdims=True)
    vb = vc_ * lax.rsqrt(var + EPS) * lng_ref[...] + lnb_ref[...]
    put(vb_ref, vb)
    vbb = vb.astype(BF16)
    group = lax.broadcasted_iota(I32, (m, BRANCH), 1) // HEAD_DIM
    mixed = jnp.zeros((m, BRANCH), F32)
    for g in range(N_HEADS):
        mixed = jnp.where(group == g, _dot(ws_ref[g], vbb), mixed)
    put(ob_ref, u * (mixed + bs_ref[...]))


def _proj(x, mod, g1, w_t, w_r, tables, fb, lng, lnb, ws, bs, *, nb, tt, stacked, layer, depth):
    bsz, t, d = x.shape
    m = nb * tt
    n_t = t // tt
    tok = lambda w: pl.BlockSpec((nb, tt, w), lambda ti, bi: (bi, ti, 0))
    const = lambda a: pl.BlockSpec(a.shape, lambda ti, bi: (0,) * a.ndim)
    tab = lambda c: pl.BlockSpec((c, m), lambda ti, bi: (0, ti))
    if stacked is None:
        assert n_t == 1 and nb == bsz
        kv_spec = pl.BlockSpec((BRANCH, m), lambda ti, bi: (0, 0))
        kv_shape = jax.ShapeDtypeStruct((BRANCH, m), F32)
        q_spec = lambda c: pl.BlockSpec((c, m), lambda ti, bi: (0, 0))
        q_shape = lambda c, dt: jax.ShapeDtypeStruct((c, m), dt)
        alias_in = ()
    else:
        assert nb == 1
        kv_spec = pl.BlockSpec((1, 1, BRANCH, tt), lambda ti, bi: (layer, bi, 0, ti))
        kv_shape = jax.ShapeDtypeStruct((depth, bsz, BRANCH, t), F32)
        q_spec = lambda c: pl.BlockSpec((1, c, tt), lambda ti, bi: (bi, 0, ti))
        q_shape = lambda c, dt: jax.ShapeDtypeStruct((bsz, c, t), dt)
        alias_in = tuple(stacked)
    n_alias = len(alias_in)
    n_in = 16
    cosa, sina, cosi, sini, coss, sins = tables
    return pl.pallas_call(
        functools.partial(_proj_kernel, nb=nb, tt=tt, n_alias=n_alias),
        grid=(n_t, bsz // nb),
        in_specs=[tok(d), pl.BlockSpec((nb, 6, d), lambda ti, bi: (bi, 0, 0)), const(g1), const(w_t), const(w_r),
                  tab(BRANCH), tab(BRANCH), tab(BRANCH), tab(BRANCH), tab(SLAB), tab(SLAB),
                  const(fb), const(lng), const(lnb), const(ws), const(bs)]
                 + [pl.BlockSpec(memory_space=pl.ANY)] * n_alias,
        out_specs=[kv_spec] * 6 + [q_spec(BRANCH)] * 4 + [q_spec(SLAB), tok(BRANCH), tok(BRANCH)],
        out_shape=[kv_shape] * 6 + [q_shape(BRANCH, BF16)] * 3 + [q_shape(BRANCH, F32), q_shape(SLAB, F32),
                                                                 jax.ShapeDtypeStruct((bsz, t, BRANCH), BF16),
                                                                 jax.ShapeDtypeStruct((bsz, t, BRANCH), F32)],
        input_output_aliases={n_in + i: i for i in range(n_alias)},
        compiler_params=_params(2),
        name="proj",
    )(x, mod, g1, w_t, w_r, cosa, sina, cosi, sini, coss, sins, fb, lng, lnb, ws, bs, *alias_in)


def _load2d(ref):
    return ref[(0,) * (len(ref.shape) - 2)]


def _fill_token_major(dst_sc, srcs):
    off = 0
    for s in srcs:
        n = s.shape[1]
        dst_sc[off:off + n, :] = s.T.astype(BF16)
        off += n
    lp = dst_sc.shape[0]
    if off < lp:
        dst_sc[off:lp, :] = jnp.zeros((lp - off, dst_sc.shape[1]), BF16)


def _fill_channel_major(dst_sc, srcs):
    off = 0
    for s in srcs:
        n = s.shape[1]
        dst_sc[:, off:off + n] = s.astype(BF16)
        off += n
    lp = dst_sc.shape[1]
    if off < lp:
        dst_sc[:, off:lp] = jnp.zeros((dst_sc.shape[0], lp - off), BF16)


def _flash_init(tq):
    return tuple((jnp.full((1, tq), NEG, F32), jnp.zeros((1, tq), F32), jnp.zeros((HEAD_DIM, tq), F32))
                 for _ in range(N_HEADS))


def _flash_step(carry, scores, vt_sc, off, tk):
    ms = [jnp.maximum(carry[h][0], jnp.max(scores[h], axis=0, keepdims=True)) for h in range(N_HEADS)]
    ps = [jnp.exp2(scores[h] - ms[h]) for h in range(N_HEADS)]
    pvs = [_dot(vt_sc[h * HEAD_DIM:(h + 1) * HEAD_DIM, pl.ds(off, tk)], ps[h].astype(BF16))
           for h in range(N_HEADS)]
    new = []
    for h in range(N_HEADS):
        m_run, l_run, acc = carry[h]
        alpha = jnp.exp2(m_run - ms[h])
        new.append((ms[h], alpha * l_run + jnp.sum(ps[h], axis=0, keepdims=True), alpha * acc + pvs[h]))
    return tuple(new)


def _flash_finish(carry):
    out_t = jnp.concatenate([acc / l_run for (_, l_run, acc) in carry], axis=0)
    return out_t.T


def _masked_heads_t(q_t):
    head = lax.broadcasted_iota(I32, q_t.shape, 0) // HEAD_DIM
    return [jnp.where(head == h, q_t, jnp.zeros_like(q_t)) for h in range(N_HEADS)]


def _src_spec(arr, layer):
    if arr.ndim == 4:
        return pl.BlockSpec((1, 1) + arr.shape[2:], lambda b, i: (layer, b, 0, 0))
    return pl.BlockSpec((1,) + arr.shape[1:], lambda b, i: (b, 0, 0))


def _q_spec(c, tq):
    return pl.BlockSpec((1, c, tq), lambda b, i: (b, 0, i))


def _o_spec(tq):
    return pl.BlockSpec((1, tq, BRANCH), lambda b, i: (b, i, 0))


def _dsa_kernel(qa_ref, qi_ref, qsm_ref, *refs, n_src, tq, tk, past_len, l_valid, n_sel, t_valid):
    k_refs, v_refs, ki_refs = refs[:n_src], refs[n_src:2 * n_src], refs[2 * n_src:3 * n_src]
    o_ref, kb_sc, vt_sc, kic_sc, lhs_sc, key_sc, hi_sc, lo_sc = refs[3 * n_src:]
    qb = pl.program_id(1)

    @pl.when(qb == 0)
    def _():
        _fill_token_major(kb_sc, [_load2d(r) for r in k_refs])
        _fill_channel_major(vt_sc, [_load2d(r) for r in v_refs])
        cats = []
        for r in ki_refs:
            ki = _load2d(r)[SLAB_KI:SLAB_KI + IDX_DIM, :]
            hi = ki.astype(BF16).astype(F32)
            cats.append(jnp.concatenate([hi, ki - hi, hi, jnp.zeros_like(hi)], axis=0))
        _fill_token_major(kic_sc, cats)

    qi = qi_ref[0]
    for h in range(IDX_HEADS):
        piece = qi[h * IDX_DIM:(h + 1) * IDX_DIM, :]
        hi = piece.astype(BF16).astype(F32)
        lhs_sc[h] = jnp.concatenate([hi, hi, piece - hi, jnp.zeros_like(hi)], axis=0).astype(BF16)
    w_t = qsm_ref[0]

    row0 = past_len + qb * tq
    qlane = lax.broadcasted_iota(I32, (1, tq), 1)
    pos = row0 + qlane
    limit = jnp.minimum((pos // CHUNK + 1) * CHUNK, l_valid)
    top_limit = jnp.minimum(((row0 + tq - 1) // CHUNK + 1) * CHUNK, l_valid)
    nblk = (top_limit + tk - 1) // tk

    def score_blk(j, _):
        off = pl.multiple_of(j * tk, tk)
        kc = kic_sc[pl.ds(off, tk), :]
        dots = [_dot(kc, lhs_sc[h]) for h in range(IDX_HEADS)]
        acc = jnp.zeros((tk, tq), F32)
        for h in range(IDX_HEADS):
            acc = acc + w_t[SLAB_W + h:SLAB_W + h + 1, :] * jnp.maximum(dots[h], 0.0)
        acc = jnp.where(acc == 0.0, 0.0, acc)
        bits = pltpu.bitcast(acc, I32)
        key = jnp.where(bits < 0, bits ^ 0x7FFFFFFF, bits)
        kidx = off + lax.broadcasted_iota(I32, (tk, tq), 0)
        key = jnp.where(kidx < limit, key, INT_MIN)
        key_sc[pl.ds(off, tk), :] = key
        hi_sc[pl.ds(off, tk), :] = (key >> 16).astype(I16)
        lo_sc[pl.ds(off, tk), :] = ((key & 0xFFFF) + I16_MIN).astype(I16)
        return 0

    lax.fori_loop(0, nblk, score_blk, 0)

    def count16(ref, cand):
        cand_b = jnp.broadcast_to(cand, (16, tq)).astype(I16)

        def body(j, part):
            off = pl.multiple_of(j * tk, tk)
            kb = ref[pl.ds(off, tk), :]
            for c in range(tk // 16):
                part = part + jnp.where(kb[c * 16:(c + 1) * 16, :] >= cand_b, jnp.int16(1), jnp.int16(0))
            return part

        part = lax.fori_loop(0, nblk, body, jnp.zeros((16, tq), I16))
        return jnp.sum(part.astype(F32), axis=0, keepdims=True)

    def count32(cand):
        cand_b = jnp.broadcast_to(cand, (8, tq))

        def body(j, part):
            off = pl.multiple_of(j * tk, tk)
            kb = key_sc[pl.ds(off, tk), :]
            for c in range(tk // 8):
                part = part + jnp.where(kb[c * 8:(c + 1) * 8, :] >= cand_b, 1.0, 0.0)
            return part

        part = lax.fori_loop(0, nblk, body, jnp.zeros((8, tq), F32))
        return jnp.sum(part, axis=0, keepdims=True)

    def kth16(ref, want):
        t = jnp.where(count16(ref, jnp.zeros((1, tq), I32)) >= want, 0, I16_MIN).astype(I32)

        def search(i, t):
            cand = t | jnp.left_shift(jnp.int32(1), 14 - i)
            return jnp.where(count16(ref, cand) >= want, cand, t)

        return lax.fori_loop(0, 15, search, t)

    kf = float(n_sel)
    t_hi = kth16(hi_sc, kf)
    above = jnp.where(t_hi >= -I16_MIN - 1, 0.0, count16(hi_sc, t_hi + 1))
    t_hi_b = jnp.broadcast_to(t_hi, (16, tq)).astype(I16)

    def keep_equal_hi(j, _):
        off = pl.multiple_of(j * tk, tk)
        hi_blk = hi_sc[pl.ds(off, tk), :]
        lo_blk = lo_sc[pl.ds(off, tk), :]
        kept = [jnp.where(hi_blk[c * 16:(c + 1) * 16, :] == t_hi_b, lo_blk[c * 16:(c + 1) * 16, :],
                          jnp.int16(I16_MIN)) for c in range(tk // 16)]
        lo_sc[pl.ds(off, tk), :] = jnp.concatenate(kept, axis=0)
        return 0

    lax.fori_loop(0, nblk, keep_equal_hi, 0)
    t_lo = kth16(lo_sc, kf - above)
    thr = t_hi * 65536 + (t_lo - I16_MIN)
    thr = jnp.maximum(thr, INT_MIN + 1)
    cnt_ge = count32(thr)
    cnt_gt = count32(thr + 1)
    room = kf - cnt_gt
    real_q = qlane < (t_valid - qb * tq)
    any_tie = jnp.max(jnp.where(real_q, jnp.where(cnt_ge > kf, 1.0, 0.0), 0.0)) > 0.0

    @pl.when(any_tie)
    def _():
        ri = lax.broadcasted_iota(I32, (tk, tk), 0)
        ci = lax.broadcasted_iota(I32, (tk, tk), 1)
        upto = jnp.where(ci <= ri, 1.0, 0.0).astype(BF16)

        def drop_late_ties(j, seen):
            off = pl.multiple_of(j * tk, tk)
            kb = key_sc[pl.ds(off, tk), :]
            eq = jnp.where(kb == thr, 1.0, 0.0)
            rank = _dot(upto, eq.astype(BF16)) + seen
            key_sc[pl.ds(off, tk), :] = jnp.where(eq * rank > room, INT_MIN, kb)
            return seen + jnp.sum(eq, axis=0, keepdims=True)

        lax.fori_loop(0, nblk, drop_late_ties, jnp.zeros((1, tq), F32))

    qh = _masked_heads_t(qa_ref[0])

    def attend(j, carry):
        off = pl.multiple_of(j * tk, tk)
        kblk = kb_sc[pl.ds(off, tk), :]
        keep = key_sc[pl.ds(off, tk), :] >= thr
        scores = [jnp.where(keep, _dot(kblk, qh[h]), NEG) for h in range(N_HEADS)]
        return _flash_step(carry, scores, vt_sc, off, tk)

    carry = lax.fori_loop(0, nblk, attend, _flash_init(tq))
    o_ref[0] = _flash_finish(carry).astype(o_ref.dtype)


def _dsa(qa, qi, qsm, k_srcs, v_srcs, ki_srcs, *, layer, lp, tq, tk, past_len, l_valid, n_sel, t_valid):
    bsz, _, t = qa.shape
    n_src = len(k_srcs)
    srcs = list(k_srcs) + list(v_srcs) + list(ki_srcs)
    return pl.pallas_call(
        functools.partial(_dsa_kernel, n_src=n_src, tq=tq, tk=tk, past_len=past_len, l_valid=l_valid,
                          n_sel=n_sel, t_valid=t_valid),
        grid=(bsz, t // tq),
        in_specs=[_q_spec(BRANCH, tq), _q_spec(BRANCH, tq), _q_spec(SLAB, tq)] + [_src_spec(a, layer) for a in srcs],
        out_specs=_o_spec(tq),
        out_shape=jax.ShapeDtypeStruct((bsz, t, BRANCH), BF16),
        scratch_shapes=[pltpu.VMEM((lp, BRANCH), BF16), pltpu.VMEM((BRANCH, lp), BF16),
                        pltpu.VMEM((lp, SLAB), BF16), pltpu.VMEM((IDX_HEADS, SLAB, tq), BF16),
                        pltpu.VMEM((lp, tq), I32), pltpu.VMEM((lp, tq), I16), pltpu.VMEM((lp, tq), I16)],
        compiler_params=_params(2),
        name="dsa",
    )(qa, qi, qsm, *srcs)


def _fox_kernel(q_ref, *refs, n_src, f_row, tq, tk, past_len):
    k_refs, v_refs, f_refs = refs[:n_src], refs[n_src:2 * n_src], refs[2 * n_src:3 * n_src]
    o_ref, vt_sc, kaug_sc, f_sc = refs[3 * n_src:]
    qb = pl.program_id(1)
    lp = f_sc.shape[1]

    @pl.when(qb == 0)
    def _():
        _fill_channel_major(vt_sc, [_load2d(r) for r in v_refs])
        ks = [_load2d(r) for r in k_refs]
        fs = []
        for r, row in zip(f_refs, f_row):
            logf = _load2d(r)
            if row + F_ROWS <= logf.shape[0]:
                fs.append(logf[row:row + F_ROWS, :])
            else:
                gates = logf[row:row + N_HEADS, :]
                fs.append(jnp.concatenate([gates, jnp.zeros((F_ROWS - N_HEADS, gates.shape[1]), F32)], axis=0))
        kt = ks[0] if len(ks) == 1 else jnp.concatenate(ks, axis=1)
        ft = fs[0] if len(fs) == 1 else jnp.concatenate(fs, axis=1)
        n_real = kt.shape[1]
        ri = lax.broadcasted_iota(I32, (F_BLK, F_BLK), 0)
        ci = lax.broadcasted_iota(I32, (F_BLK, F_BLK), 1)
        upto = jnp.where(ri <= ci, 1.0, 0.0).astype(BF16)
        rowx = lax.broadcasted_iota(I32, (HEAD_DIM, F_BLK), 0)
        run = jnp.zeros((F_ROWS, 1), F32)
        for b in range(lp // F_BLK):
            lo, hi_ = b * F_BLK, min((b + 1) * F_BLK, n_real)
            if hi_ <= lo:
                kaug_sc[lo:lo + F_BLK, :] = jnp.zeros((F_BLK, N_HEADS * AUG), BF16)
                f_sc[:, lo:lo + F_BLK] = jnp.broadcast_to(run * LOG2E, (F_ROWS, F_BLK))
                continue
            k_blk, f_blk = kt[:, lo:hi_], ft[:, lo:hi_]
            if hi_ - lo < F_BLK:
                k_blk = jnp.concatenate([k_blk, jnp.zeros((BRANCH, F_BLK - (hi_ - lo)), F32)], axis=1)
                f_blk = jnp.concatenate([f_blk, jnp.zeros((F_ROWS, F_BLK - (hi_ - lo)), F32)], axis=1)
            s1, s2, s3 = _split3(f_blk)
            fsum = _dot(s1, upto) + _dot(s2, upto) + _dot(s3, upto) + run
            run = fsum[:, F_BLK - 1:F_BLK]
            fsum = fsum * LOG2E
            f_sc[:, lo:lo + F_BLK] = fsum
            f1, f2, f3 = (p.astype(F32) for p in _split3(fsum))
            for h in range(N_HEADS):
                extra = jnp.where(rowx < 3, 1.0,
                                  jnp.where(rowx == 3, -f1[h:h + 1, :],
                                            jnp.where(rowx == 4, -f2[h:h + 1, :],
                                                      jnp.where(rowx == 5, -f3[h:h + 1, :], 0.0))))
                aug_t = jnp.concatenate([k_blk[h * HEAD_DIM:(h + 1) * HEAD_DIM, :], extra], axis=0)
                kaug_sc[lo:lo + F_BLK, h * AUG:(h + 1) * AUG] = aug_t.T.astype(BF16)

    row0 = past_len + qb * tq
    pos = row0 + lax.broadcasted_iota(I32, (1, tq), 1)
    n_full = row0 // tk
    n_all = (row0 + tq + tk - 1) // tk
    q_t = q_ref[0].astype(F32)
    f1, f2, f3 = (p.astype(F32) for p in _split3(f_sc[:, pl.ds(pl.multiple_of(row0, Q_PAD), tq)]))
    rowq = lax.broadcasted_iota(I32, (HEAD_DIM, tq), 0)
    qh = []
    for h in range(N_HEADS):
        extra = jnp.where(rowq == 0, f1[h:h + 1, :],
                          jnp.where(rowq == 1, f2[h:h + 1, :],
                                    jnp.where(rowq == 2, f3[h:h + 1, :], jnp.where(rowq < 6, 1.0, 0.0))))
        qh.append(jnp.concatenate([q_t[h * HEAD_DIM:(h + 1) * HEAD_DIM, :], extra], axis=0).astype(BF16))

    def step(j, carry, masked):
        off = pl.multiple_of(j * tk, tk)
        scores = [_dot(kaug_sc[pl.ds(off, tk), h * AUG:(h + 1) * AUG], qh[h]) for h in range(N_HEADS)]
        if masked:
            causal = off + lax.broadcasted_iota(I32, (tk, tq), 0) <= pos
            scores = [jnp.where(causal, s, NEG) for s in scores]
        return _flash_step(carry, scores, vt_sc, off, tk)

    carry = lax.fori_loop(0, n_full, lambda j, c: step(j, c, False), _flash_init(tq))
    carry = lax.fori_loop(n_full, n_all, lambda j, c: step(j, c, True), carry)
    o_ref[0] = _flash_finish(carry).astype(o_ref.dtype)


def _fox(q, k_srcs, v_srcs, f_srcs, f_row, *, layer, lp, tq, tk, past_len):
    bsz, _, t = q.shape
    n_src = len(k_srcs)
    srcs = list(k_srcs) + list(v_srcs) + list(f_srcs)
    return pl.pallas_call(
        functools.partial(_fox_kernel, n_src=n_src, f_row=tuple(f_row), tq=tq, tk=tk, past_len=past_len),
        grid=(bsz, t // tq),
        in_specs=[_q_spec(BRANCH, tq)] + [_src_spec(a, layer) for a in srcs],
        out_specs=_o_spec(tq),
        out_shape=jax.ShapeDtypeStruct((bsz, t, BRANCH), BF16),
        scratch_shapes=[pltpu.VMEM((BRANCH, lp), BF16), pltpu.VMEM((lp, N_HEADS * AUG), BF16),
                        pltpu.VMEM((F_ROWS, lp), F32)],
        compiler_params=_params(2),
        name="fox",
    )(q, *srcs)


def _sb_kernel(q_ref, *refs, n_src, tq, tk, past_len):
    k_refs, v_refs = refs[:n_src], refs[n_src:2 * n_src]
    o_ref, kb_sc, vt_sc = refs[2 * n_src:]
    qb = pl.program_id(1)

    @pl.when(qb == 0)
    def _():
        _fill_token_major(kb_sc, [_load2d(r) for r in k_refs])
        _fill_channel_major(vt_sc, [_load2d(r) for r in v_refs])

    qh = _masked_heads_t(q_ref[0])
    row0 = past_len + qb * tq
    pos = row0 + lax.broadcasted_iota(I32, (1, tq), 1)
    j_top = (row0 + tq - 1) // tk
    ri = lax.broadcasted_iota(I32, (tk, tk), 0)
    ci = lax.broadcasted_iota(I32, (tk, tk), 1)
    after = jnp.where(ci > ri, 1.0, 0.0).astype(BF16)

    def cond(carry):
        j, state = carry
        live = state[0][0]
        for h in range(1, N_HEADS):
            live = jnp.maximum(live, state[h][0])
        return jnp.logical_and(j >= 0, jnp.max(live) > -SB_CUTOFF)

    def body(carry):
        j, state = carry
        off = pl.multiple_of(j * tk, tk)
        kblk = kb_sc[pl.ds(off, tk), :]
        zs = [_dot(kblk, qh[h]) for h in range(N_HEADS)]
        strict = off + lax.broadcasted_iota(I32, (tk, tq), 0) < pos
        sps = [jnp.maximum(z, 0.0) + jnp.log2(1.0 + jnp.exp2(-jnp.abs(z))) for z in zs]
        keeps = [jnp.where(strict, -sp, 0.0) for sp in sps]
        laters = []
        for h in range(N_HEADS):
            k_hi, k_lo = _split2(keeps[h])
            laters.append(_dot(after, k_hi) + _dot(after, k_lo) + state[h][0])
        ws = [jnp.where(strict, jnp.exp2(zs[h] - sps[h] + laters[h]), 0.0) for h in range(N_HEADS)]
        pvs = [_dot(vt_sc[h * HEAD_DIM:(h + 1) * HEAD_DIM, pl.ds(off, tk)], ws[h].astype(BF16))
               for h in range(N_HEADS)]
        new = tuple((state[h][0] + jnp.sum(keeps[h], axis=0, keepdims=True), state[h][1] + pvs[h])
                    for h in range(N_HEADS))
        return j - 1, new

    init = tuple((jnp.zeros((1, tq), F32), jnp.zeros((HEAD_DIM, tq), F32)) for _ in range(N_HEADS))
    _, state = lax.while_loop(cond, body, (j_top, init))
    out_t = jnp.concatenate([acc for (_, acc) in state], axis=0)
    o_ref[0] = out_t.T.astype(o_ref.dtype)


def _sb(q, k_srcs, v_srcs, *, layer, lp, tq, tk, past_len):
    bsz, _, t = q.shape
    n_src = len(k_srcs)
    srcs = list(k_srcs) + list(v_srcs)
    return pl.pallas_call(
        functools.partial(_sb_kernel, n_src=n_src, tq=tq, tk=tk, past_len=past_len),
        grid=(bsz, t // tq),
        in_specs=[_q_spec(BRANCH, tq)] + [_src_spec(a, layer) for a in srcs],
        out_specs=_o_spec(tq),
        out_shape=jax.ShapeDtypeStruct((bsz, t, BRANCH), BF16),
        scratch_shapes=[pltpu.VMEM((lp, BRANCH), BF16), pltpu.VMEM((BRANCH, lp), BF16)],
        compiler_params=_params(2),
        name="sb",
    )(q, *srcs)


def _merge_kernel(x_ref, mod_ref, g1_ref, oa_ref, ob_ref, oc_ref, od_ref, wg_ref, bg_ref, wb_ref, wo_ref,
                  xo_ref, *, nb, tt):
    m = nb * tt
    d = x_ref.shape[-1]
    x = x_ref[...]
    mod = mod_ref[...]
    h = _rms(x, g1_ref[...]) * (1.0 + mod[:, 1:2, :]) + mod[:, 0:1, :]
    hb = h.reshape(m, d).astype(BF16)
    o_refs = (oa_ref, ob_ref, oc_ref, od_ref)

    def pre(i):
        return (_dot(hb, wg_ref[:, i * d:(i + 1) * d]), _dot(o_refs[i][...].reshape(m, BRANCH), wb_ref[i]))

    merged = None
    cur = pre(0)
    for i in range(N_BRANCH):
        nxt = pre(i + 1) if i + 1 < N_BRANCH else None
        term = jax.nn.sigmoid(cur[0] + bg_ref[:, i * d:(i + 1) * d]) * cur[1]
        merged = term if merged is None else merged + term
        cur = nxt
    y = _dot(merged.astype(BF16), wo_ref[...])
    xo_ref[...] = x + mod[:, 2:3, :] * y.reshape(nb, tt, d)


def _merge(x, mod, g1, oa, ob, oc, od, w_gate, b_gate, w_branch, w_out, *, nb, tt):
    bsz, t, d = x.shape
    tok = lambda w: pl.BlockSpec((nb, tt, w), lambda bi, ti: (bi, ti, 0))
    const = lambda a: pl.BlockSpec(a.shape, lambda bi, ti: (0,) * a.ndim)
    return pl.pallas_call(
        functools.partial(_merge_kernel, nb=nb, tt=tt),
        grid=(bsz // nb, t // tt),
        in_specs=[tok(d), pl.BlockSpec((nb, 6, d), lambda bi, ti: (bi, 0, 0)), const(g1),
                  tok(BRANCH), tok(BRANCH), tok(BRANCH), tok(BRANCH),
                  const(w_gate), const(b_gate), const(w_branch), const(w_out)],
        out_specs=tok(d),
        out_shape=jax.ShapeDtypeStruct((bsz, t, d), F32),
        compiler_params=_params(2),
        name="merge",
    )(x, mod, g1, oa, ob, oc, od, w_gate, b_gate, w_branch, w_out)


def _ffn_kernel(x_ref, mod_ref, g2_ref, pfx_ref, wu_ref, wc_ref, bc_ref, wd_ref, fg_ref,
                xo_ref, st_ref, carry_sc, *, nb, tt, cw, final_norm):
    m = nb * tt
    d = x_ref.shape[-1]
    dff = wd_ref.shape[0]
    ti = pl.program_id(1)

    @pl.when(ti == 0)
    def _():
        carry_sc[...] = pfx_ref[...]

    x = x_ref[...]
    mod = mod_ref[...]
    h = _rms(x, g2_ref[...]) * (1.0 + mod[:, 4:5, :]) + mod[:, 3:4, :]
    hb = h.reshape(m, d).astype(BF16)
    rowi = lax.broadcasted_iota(I32, (nb, tt, cw), 1)

    def up_cols(c0):
        return _dot(hb, wu_ref[:, c0:c0 + cw]).reshape(nb, tt, cw)

    def conv_cols(up, c0):
        prev0 = carry_sc[:, 0:1, c0:c0 + cw]
        prev1 = carry_sc[:, 1:2, c0:c0 + cw]
        back1 = jnp.where(rowi == 0, prev1, pltpu.roll(up, 1, axis=1))
        back2 = jnp.where(rowi == 0, prev0, jnp.where(rowi == 1, prev1, pltpu.roll(up, 2, axis=1)))
        carry_sc[:, :, c0:c0 + cw] = up[:, tt - 2:tt, :]
        wc = wc_ref[:, c0:c0 + cw]
        conv = back2 * wc[0:1, :] + back1 * wc[1:2, :] + up * wc[2:3, :] + bc_ref[:, c0:c0 + cw]
        return conv.reshape(m, cw)

    acc = jnp.zeros((m, d), F32)
    n_chunks = dff // cw
    ups = (up_cols(0), up_cols(dff))
    for j in range(n_chunks):
        nxt = (up_cols((j + 1) * cw), up_cols(dff + (j + 1) * cw)) if j + 1 < n_chunks else None
        gate = conv_cols(ups[0], j * cw)
        val = conv_cols(ups[1], dff + j * cw)
        act = (gate * jax.nn.sigmoid(gate) * val).astype(BF16)
        acc = acc + _dot(act, wd_ref[j * cw:(j + 1) * cw, :])
        ups = nxt
    xo = x + mod[:, 5:6, :] * acc.reshape(nb, tt, d)
    if final_norm:
        xo = _rms(xo, fg_ref[...])
    xo_ref[...] = xo

    @pl.when(ti == pl.num_programs(1) - 1)
    def _():
        st_ref[...] = carry_sc[...]


def _ffn(x, mod, g2, prefix, w_up, w_conv, b_conv, w_down, final_g, *, nb, tt, final_norm):
    bsz, t, d = x.shape
    dff = w_down.shape[0]
    cw = 256
    tok = pl.BlockSpec((nb, tt, d), lambda bi, ti: (bi, ti, 0))
    const = lambda a: pl.BlockSpec(a.shape, lambda bi, ti: (0,) * a.ndim)
    state = pl.BlockSpec((nb, CONV_W - 1, 2 * dff), lambda bi, ti: (bi, 0, 0))
    return pl.pallas_call(
        functools.partial(_ffn_kernel, nb=nb, tt=tt, cw=cw, final_norm=final_norm),
        grid=(bsz // nb, t // tt),
        in_specs=[tok, pl.BlockSpec((nb, 6, d), lambda bi, ti: (bi, 0, 0)), const(g2), state,
                  const(w_up), const(w_conv), const(b_conv), const(w_down), const(final_g)],
        out_specs=[tok, state],
        out_shape=[jax.ShapeDtypeStruct((bsz, t, d), F32),
                   jax.ShapeDtypeStruct((bsz, CONV_W - 1, 2 * dff), F32)],
        scratch_shapes=[pltpu.VMEM((nb, CONV_W - 1, 2 * dff), F32)],
        compiler_params=_params(2),
        name="ffn",
    )(x, mod, g2, prefix, w_up, w_conv, b_conv, w_down, final_g)


def _rope_tables_t(pos):
    def tab(half, heads, pad):
        inv = ROPE_THETA ** (-jnp.arange(half, dtype=F32) / half)
        ang = inv[:, None] * pos.astype(F32)[None, :]
        cos, sin = jnp.cos(ang), jnp.sin(ang)
        cos_t = jnp.tile(jnp.concatenate([cos, cos], axis=0), (heads, 1))
        sin_t = jnp.tile(jnp.concatenate([-sin, sin], axis=0), (heads, 1))
        if pad:
            cos_t = jnp.concatenate([cos_t, jnp.ones((pad, pos.shape[0]), F32)], axis=0)
            sin_t = jnp.concatenate([sin_t, jnp.zeros((pad, pos.shape[0]), F32)], axis=0)
        return cos_t, sin_t

    cosa, sina = tab(HEAD_DIM // 2, N_HEADS, 0)
    cosi, sini = tab(IDX_DIM // 2, IDX_HEADS, 0)
    coss, sins = tab(IDX_DIM // 2, 1, SLAB - IDX_DIM)
    return cosa, sina, cosi, sini, coss, sins


def _layout_w_in(w_in):
    sizes = (BRANCH, BRANCH, BRANCH, IDX_HEADS * IDX_DIM, IDX_DIM, IDX_HEADS, BRANCH, BRANCH,
             BRANCH, BRANCH, BRANCH, N_HEADS, BRANCH, BRANCH, BRANCH)
    offs = [0]
    for s in sizes:
        offs.append(offs[-1] + s)
    piece = lambda i: w_in[:, offs[i]:offs[i + 1]]
    a_q, a_k, a_v, a_qi, a_ki, a_w, b_u, b_v, c_q, c_k, c_v, c_f, d_q, d_k, d_v = (piece(i) for i in range(15))
    pad = jnp.zeros((w_in.shape[0], SLAB - IDX_DIM - IDX_HEADS - N_HEADS), w_in.dtype)
    w_t = jnp.concatenate([a_q, a_k, a_v, a_qi, c_q, c_k, c_v, d_q, d_k, d_v, a_ki, a_w, c_f, pad], axis=-1)
    return w_t.T.astype(BF16), jnp.concatenate([b_u, b_v], axis=-1).astype(BF16)


def _block_diag_tril(w_s, n, reps):
    tri = jnp.tril(jnp.ones((n, n), dtype=bool))
    w = jnp.where(tri[None], w_s[:, :n, :n], 0)
    eye = jnp.eye(reps, dtype=w.dtype)
    return jnp.einsum("ab,gts->gatbs", eye, w).reshape(w.shape[0], reps * n, reps * n).astype(BF16)


def _round_up(a, b):
    return (a + b - 1) // b * b


def _channel_major(cache):
    depth, bsz, length = cache.shape[:3]
    flat = cache.reshape(depth, bsz, length, -1)
    return jnp.swapaxes(flat, 2, 3)


def _layer(x, mod, lp, past, prefix, final_g, stacked, *, layer, depth, final_norm, past_len):
    bsz, t, d = x.shape
    is_prompt = past is None
    n = min(t, B_CHUNK)
    if is_prompt:
        nb, tt = 1, min(t, 512)
        reps = tt // n
        pos_rows = jnp.arange(t)
    else:
        nb, tt = bsz, t
        reps = (nb * tt) // n
        pos_rows = jnp.tile(past_len + jnp.arange(t), nb)
    m = nb * tt
    tables = _rope_tables_t(pos_rows)
    ws = _block_diag_tril(lp["w_s"], n, reps)
    bs = jnp.tile(jnp.repeat(lp["b_s"][:, :n].T, HEAD_DIM, axis=1), (reps, 1))
    fb = jnp.broadcast_to(lp["fb"], (SLAB, m))
    outs = _proj(x, mod, lp["g1"], lp["w_t"], lp["w_r"], tables, fb, lp["lng"], lp["lnb"], ws, bs,
                 nb=nb, tt=tt, stacked=stacked if is_prompt else None, layer=layer, depth=depth)
    kv, (qa, qc, qd, qi, sm, ob, vb) = outs[:6], outs[6:]

    l_valid = past_len + t
    tqp = _round_up(t, Q_PAD)
    if is_prompt:
        tk = min(512, t)
        tk_d = min(256, t)
        lpad = t
        tq = min(256, tqp)
        n_sel = min(TOPK_MAX, t // 4)
        ka, va, kc, vc, kd, vd = ([a] for a in kv)
        ki_srcs, f_srcs, f_row = [sm], [sm], (SLAB_F,)
        att_layer = layer
    else:
        tk = tk_d = 256
        tq = tqp
        lpad = _round_up(past_len + tqp, tk)
        n_sel = min(TOPK_MAX, l_valid // 4)

        def tokens(a):
            a = jnp.swapaxes(a.reshape(a.shape[0], bsz, t), 0, 1)
            return jnp.pad(a, ((0, 0), (0, 0), (0, tqp - t)))

        pk_a, pv_a, pki, pk_c, pv_c, plogf, pk_d, pv_d = past
        qa, qc, qd, qi, sm_tok = (tokens(a) for a in (qa, qc, qd, qi, sm))
        new = [tokens(a) for a in kv]
        ka, va, kc, vc, kd, vd = ([p_, n_] for p_, n_ in zip((pk_a, pv_a, pk_c, pv_c, pk_d, pv_d), new))
        ki_srcs, f_srcs, f_row = [pki, sm_tok], [plogf, sm_tok], (0, SLAB_F)
        att_layer = layer
    sm_q = sm if is_prompt else sm_tok

    oa = _dsa(qa, qi, sm_q, ka, va, ki_srcs, layer=att_layer, lp=lpad, tq=tq, tk=tk, past_len=past_len,
              l_valid=l_valid, n_sel=n_sel, t_valid=t)[:, :t]
    oc = _fox(qc, kc, vc, f_srcs, f_row, layer=att_layer, lp=lpad, tq=tq, tk=tk, past_len=past_len)[:, :t]
    od = _sb(qd, kd, vd, layer=att_layer, lp=lpad, tq=tq, tk=tk_d, past_len=past_len)[:, :t]

    mtt = min(t, 512) if is_prompt else tt
    x = _merge(x, mod, lp["g1"], oa, ob, oc, od, lp["w_gate"], lp["b_gate"], lp["w_branch"], lp["w_out"],
               nb=nb, tt=mtt)
    x, conv_state = _ffn(x, mod, lp["g2"], prefix, lp["w_up"], lp["w_conv"], lp["b_conv"], lp["w_down"],
                         final_g, nb=nb, tt=tt, final_norm=final_norm)
    if is_prompt:
        rows = (sm, conv_state)
    else:
        heads = lambda a: jnp.transpose(a.reshape(N_HEADS, HEAD_DIM, bsz, t), (2, 3, 0, 1))
        small = lambda a, r0, r1: jnp.transpose(a[r0:r1].reshape(r1 - r0, bsz, t), (1, 2, 0))
        ka_n, va_n, kc_n, vc_n, kd_n, vd_n = kv
        rows = (heads(ka_n), heads(va_n), small(sm, SLAB_KI, SLAB_KI + IDX_DIM), vb, heads(kc_n), heads(vc_n),
                small(sm, SLAB_F, SLAB_F + N_HEADS), heads(kd_n), heads(vd_n), conv_state)
    return x, rows, kv


def kernel(x_prompt, x_sample, cache_a_k, cache_a_v, cache_a_kidx, cache_c_k, cache_c_v, cache_c_logf, cache_d_k, cache_d_v, state_ffn_conv, c_prompt, c_sample, norm1_g, norm2_g, w_mod, b_mod, w_in, f_bias, lnv_g, lnv_b, w_spatial, b_spatial, w_branch, w_gate, b_gate, w_out, w_up, w_conv, b_conv, w_down, final_g):
    depth = w_in.shape[0]
    bsz, t_prompt, d = x_prompt.shape
    dbsz = x_sample.shape[0]
    past_len = cache_a_k.shape[2]
    mods = _modulation(jnp.concatenate([c_prompt, c_sample], axis=0), w_mod, b_mod)
    mods = mods.reshape(depth, bsz + dbsz, 6, d)
    prefix = jnp.zeros((bsz, CONV_W - 1, w_up.shape[-1]), x_prompt.dtype)
    fg = final_g.reshape(1, d)
    past = tuple(_channel_major(c) for c in (cache_a_k, cache_a_v, cache_a_kidx, cache_c_k, cache_c_v,
                                              cache_c_logf.astype(F32), cache_d_k, cache_d_v))
    xp, xs = x_prompt, x_sample
    rows_p, rows_s = [], []
    stacked = ()
    for l in range(depth):
        fb = jnp.zeros((SLAB, 1), F32).at[SLAB_F:SLAB_F + N_HEADS, 0].set(f_bias[l].astype(F32))
        w_t, w_r = _layout_w_in(w_in[l])
        lp = dict(g1=norm1_g[l].reshape(1, d), g2=norm2_g[l].reshape(1, d), w_t=w_t, w_r=w_r, fb=fb,
                  lng=lnv_g[l].reshape(1, BRANCH), lnb=lnv_b[l].reshape(1, BRANCH),
                  w_s=w_spatial[l], b_s=b_spatial[l], w_branch=w_branch[l].astype(BF16),
                  w_gate=w_gate[l].astype(BF16), b_gate=b_gate[l].reshape(1, -1), w_out=w_out[l].astype(BF16),
                  w_up=w_up[l].astype(BF16), w_conv=w_conv[l], b_conv=b_conv[l].reshape(1, -1),
                  w_down=w_down[l].astype(BF16))
        last = l == depth - 1
        xp, new_p, stacked = _layer(xp, mods[l, :bsz], lp, None, prefix, fg, stacked, layer=l, depth=depth,
                                    final_norm=last, past_len=0)
        xs, new_s, _ = _layer(xs, mods[l, bsz:], lp, past, state_ffn_conv[l], fg, None, layer=l, depth=depth,
                              final_norm=last, past_len=past_len)
        rows_p.append(new_p)
        rows_s.append(new_s)

    def stacked_s(i):
        return jnp.stack([r[i] for r in rows_s], axis=0)

    def heads_p(a):
        return jnp.transpose(a.reshape(depth, bsz, N_HEADS, HEAD_DIM, t_prompt), (0, 1, 4, 2, 3))

    sm_p = jnp.stack([r[0] for r in rows_p], axis=0)
    small_p = lambda r0, r1: jnp.swapaxes(sm_p[:, :, r0:r1, :], 2, 3)
    ka_p, va_p, kc_p, vc_p, kd_p, vd_p = (heads_p(a) for a in stacked)
    conv_p = jnp.stack([r[1] for r in rows_p], axis=0)
    return (xp, xs,
            ka_p, stacked_s(0),
            va_p, stacked_s(1),
            small_p(SLAB_KI, SLAB_KI + IDX_DIM), stacked_s(2),
            stacked_s(3),
            kc_p, stacked_s(4),
            vc_p, stacked_s(5),
            small_p(SLAB_F, SLAB_F + N_HEADS), stacked_s(6),
            kd_p, stacked_s(7),
            vd_p, stacked_s(8),
            conv_p, stacked_s(9))
```

```python
import functools

import jax
import jax.numpy as jnp
from jax import lax
from jax.experimental import pallas as pl
from jax.experimental.pallas import tpu as pltpu

F32 = jnp.float32
BF16 = jnp.bfloat16
I32 = jnp.int32
I16 = jnp.int16

HEAD_DIM = 64
N_HEADS = 4
BRANCH = N_HEADS * HEAD_DIM
IDX_HEADS = 8
IDX_DIM = 32
CHUNK = 64
TOPK_MAX = 256
B_CHUNK = 128
N_BRANCH = 4
CONV_W = 3
ROPE_THETA = 10000.0
EPS = 1e-6
LOG2E = 1.4426950408889634
Q_SCALE = HEAD_DIM ** -0.5 * LOG2E

SLAB = 128
SLAB_KI = 0
SLAB_W = 32
SLAB_F = 40
AUG = 128
F_ROWS = 8
F_BLK = 256
N_T_GROUPS = 10
NEG = -1e30
INT_MIN = -2 ** 31
I16_MIN = -2 ** 15
SB_CUTOFF = 220.0
Q_PAD = 128
SUB = 8
VMEM_LIMIT = 56 * 1024 * 1024


def _params(n_axes, vmem=VMEM_LIMIT):
    return pltpu.CompilerParams(dimension_semantics=("arbitrary",) * n_axes, vmem_limit_bytes=vmem)


def _dot(a, b):
    return jnp.dot(a, b, preferred_element_type=F32)


def _split2(x):
    hi = x.astype(BF16)
    lo = (x - hi.astype(F32)).astype(BF16)
    return hi, lo


def _split3(x):
    x1 = x.astype(BF16)
    r = x - x1.astype(F32)
    x2 = r.astype(BF16)
    x3 = (r - x2.astype(F32)).astype(BF16)
    return x1, x2, x3


def _softplus(z):
    return jnp.maximum(z, 0.0) + jnp.log1p(jnp.exp(-jnp.abs(z)))


def _rms(x, g):
    ms = jnp.mean(x * x, axis=-1, keepdims=True)
    return x * lax.rsqrt(ms + EPS) * g


def _rope_t(x, cos, sin_signed, half):
    rows = x.shape[0]
    row = lax.broadcasted_iota(I32, x.shape, 0)
    fwd = pltpu.roll(x, half, axis=0)
    bwd = pltpu.roll(x, rows - half, axis=0)
    rot = jnp.where((row % (2 * half)) < half, bwd, fwd)
    return x * cos + rot * sin_signed


def _mod_kernel(c_ref, w_ref, b_ref, o_ref):
    c = c_ref[...]
    a = c * jax.nn.sigmoid(c)
    a1, a2 = _split2(a)
    w1, w2 = _split2(w_ref[0])
    o_ref[0] = _dot(a1, w1) + _dot(a1, w2) + _dot(a2, w1) + b_ref[0]


def _modulation(c_all, w_mod, b_mod):
    depth, d, n = w_mod.shape
    rows = c_all.shape[0]
    tn = 1536
    return pl.pallas_call(
        _mod_kernel,
        grid=(depth, n // tn),
        in_specs=[pl.BlockSpec((rows, d), lambda l, j: (0, 0)),
                  pl.BlockSpec((1, d, tn), lambda l, j: (l, 0, j)),
                  pl.BlockSpec((1, 1, tn), lambda l, j: (l, 0, j))],
        out_specs=pl.BlockSpec((1, rows, tn), lambda l, j: (l, 0, j)),
        out_shape=jax.ShapeDtypeStruct((depth, rows, n), F32),
        compiler_params=_params(2),
        name="modulation",
    )(c_all, w_mod, b_mod.reshape(depth, 1, n))


def _proj_kernel(x_ref, mod_ref, g1_ref, wt_ref, cosa_ref, sina_ref, cosi_ref, sini_ref, coss_ref,
                 sins_ref, fb_ref, lng_ref, lnb_ref, ws_ref, bs_ref, *rest, nb, tt, n_alias):
    (ka_ref, va_ref, kc_ref, vc_ref, kd_ref, vd_ref,
     qa_ref, qc_ref, qd_ref, qi_ref, sm_ref, ob_ref, vb_ref) = rest[n_alias:]
    m = nb * tt
    d = x_ref.shape[-1]
    x = x_ref[...]
    mod = mod_ref[...]
    h = (_rms(x, g1_ref[...]) * (1.0 + mod[:, 1:2, :]) + mod[:, 0:1, :]).reshape(m, d)
    ht = h.T.astype(BF16)

    def col_t(j):
        return _dot(wt_ref[j * BRANCH:(j + 1) * BRANCH, :], ht)

    def put_t(ref, val):
        ref[...] = val.reshape(ref.shape).astype(ref.dtype)

    def put(ref, val):
        ref[...] = val.reshape(nb, tt, val.shape[-1]).astype(ref.dtype)

    cosa, sina = cosa_ref[...], sina_ref[...]
    put_t(qa_ref, _rope_t(col_t(0), cosa, sina, HEAD_DIM // 2) * Q_SCALE)
    put_t(ka_ref, _rope_t(col_t(1), cosa, sina, HEAD_DIM // 2))
    put_t(va_ref, col_t(2))
    put_t(qi_ref, _rope_t(col_t(3), cosi_ref[...], sini_ref[...], IDX_DIM // 2))
    put_t(qc_ref, col_t(4) * Q_SCALE)
    put_t(kc_ref, col_t(5))
    put_t(vc_ref, col_t(6))
    put_t(qd_ref, col_t(7) * Q_SCALE)
    put_t(kd_ref, col_t(8))
    put_t(vd_ref, col_t(9))
    sm = _dot(wt_ref[(N_T_GROUPS + 2) * BRANCH:(N_T_GROUPS + 2) * BRANCH + SLAB, :], ht)
    row = lax.broadcasted_iota(I32, sm.shape, 0)
    roped = _rope_t(sm, coss_ref[...], sins_ref[...], IDX_DIM // 2)
    logf = -_softplus(-(sm + fb_ref[...]))
    is_f = jnp.where(row >= SLAB_F, jnp.where(row < SLAB_F + N_HEADS, 1, 0), 0)
    put_t(sm_ref, jnp.where(is_f == 1, logf, roped))
    u = jax.nn.gelu(col_t(N_T_GROUPS))
    vg = jax.nn.gelu(col_t(N_T_GROUPS + 1))
    mu = jnp.mean(vg, axis=0, keepdims=True)
    vc_ = vg - mu
    var = jnp.mean(vc_ * vc_, axis=0, keepdims=True)
    vb = vc_ * lax.rsqrt(var + EPS) * lng_ref[...] + lnb_ref[...]
    put(vb_ref, vb.T)
    vbb = vb.astype(BF16)
    mixed = jnp.concatenate([_dot(vbb[g * HEAD_DIM:(g + 1) * HEAD_DIM, :], ws_ref[g]) for g in range(N_HEADS)],
                            axis=0)
    put(ob_ref, (u * (mixed + bs_ref[...])).T)


def _proj(x, mod, g1, w_t, tables, fb, lng, lnb, ws, bs, *, nb, tt, stacked, layer, depth):
    bsz, t, d = x.shape
    m = nb * tt
    n_t = t // tt
    tok = lambda w: pl.BlockSpec((nb, tt, w), lambda ti, bi: (bi, ti, 0))
    const = lambda a: pl.BlockSpec(a.shape, lambda ti, bi: (0,) * a.ndim)
    tab = lambda c: pl.BlockSpec((c, m), lambda ti, bi: (0, ti))
    if stacked is None:
        assert n_t == 1 and nb == bsz
        kv_spec = pl.BlockSpec((BRANCH, m), lambda ti, bi: (0, 0))
        kv_shape = jax.ShapeDtypeStruct((BRANCH, m), F32)
        q_spec = lambda c: pl.BlockSpec((c, m), lambda ti, bi: (0, 0))
        q_shape = lambda c, dt: jax.ShapeDtypeStruct((c, m), dt)
        alias_in = ()
    else:
        assert nb == 1
        kv_spec = pl.BlockSpec((1, 1, BRANCH, tt), lambda ti, bi: (layer, bi, 0, ti))
        kv_shape = jax.ShapeDtypeStruct((depth, bsz, BRANCH, t), F32)
        q_spec = lambda c: pl.BlockSpec((1, c, tt), lambda ti, bi: (bi, 0, ti))
        q_shape = lambda c, dt: jax.ShapeDtypeStruct((bsz, c, t), dt)
        alias_in = tuple(stacked)
    n_alias = len(alias_in)
    n_in = 15
    cosa, sina, cosi, sini, coss, sins = tables
    return pl.pallas_call(
        functools.partial(_proj_kernel, nb=nb, tt=tt, n_alias=n_alias),
        grid=(n_t, bsz // nb),
        in_specs=[tok(d), pl.BlockSpec((nb, 6, d), lambda ti, bi: (bi, 0, 0)), const(g1), const(w_t),
                  tab(BRANCH), tab(BRANCH), tab(BRANCH), tab(BRANCH), tab(SLAB), tab(SLAB),
                  const(fb), const(lng), const(lnb), const(ws), const(bs)]
                 + [pl.BlockSpec(memory_space=pl.ANY)] * n_alias,
        out_specs=[kv_spec] * 6 + [q_spec(BRANCH)] * 4 + [q_spec(SLAB), tok(BRANCH), tok(BRANCH)],
        out_shape=[kv_shape] * 6 + [q_shape(BRANCH, BF16)] * 3 + [q_shape(BRANCH, F32), q_shape(SLAB, F32),
                                                                 jax.ShapeDtypeStruct((bsz, t, BRANCH), BF16),
                                                                 jax.ShapeDtypeStruct((bsz, t, BRANCH), F32)],
        input_output_aliases={n_in + i: i for i in range(n_alias)},
        compiler_params=_params(2),
        name="proj",
    )(x, mod, g1, w_t, cosa, sina, cosi, sini, coss, sins, fb, lng, lnb, ws, bs, *alias_in)


def _load2d(ref):
    return ref[(0,) * (len(ref.shape) - 2)]


def _fill_token_major(dst_sc, srcs):
    off = 0
    for s in srcs:
        n = s.shape[1]
        dst_sc[off:off + n, :] = s.T.astype(BF16)
        off += n
    lp = dst_sc.shape[0]
    if off < lp:
        dst_sc[off:lp, :] = jnp.zeros((lp - off, dst_sc.shape[1]), BF16)


def _fill_channel_major(dst_sc, srcs):
    off = 0
    for s in srcs:
        n = s.shape[1]
        dst_sc[:, off:off + n] = s.astype(BF16)
        off += n
    lp = dst_sc.shape[1]
    if off < lp:
        dst_sc[:, off:lp] = jnp.zeros((dst_sc.shape[0], lp - off), BF16)


def _flash_init(tq):
    return tuple((jnp.full((1, tq), NEG, F32), jnp.zeros((1, tq), F32), jnp.zeros((HEAD_DIM, tq), F32))
                 for _ in range(N_HEADS))


def _flash_step(carry, scores, vt_sc, off, tk):
    ms = [jnp.maximum(carry[h][0], jnp.max(scores[h], axis=0, keepdims=True)) for h in range(N_HEADS)]
    ps = [jnp.exp2(scores[h] - ms[h]) for h in range(N_HEADS)]
    pvs = [_dot(vt_sc[h * HEAD_DIM:(h + 1) * HEAD_DIM, pl.ds(off, tk)], ps[h].astype(BF16))
           for h in range(N_HEADS)]
    new = []
    for h in range(N_HEADS):
        m_run, l_run, acc = carry[h]
        alpha = jnp.exp2(m_run - ms[h])
        new.append((ms[h], alpha * l_run + jnp.sum(ps[h], axis=0, keepdims=True), alpha * acc + pvs[h]))
    return tuple(new)


def _flash_finish(carry):
    out_t = jnp.concatenate([acc / l_run for (_, l_run, acc) in carry], axis=0)
    return out_t.T


def _masked_heads_t(q_t):
    head = lax.broadcasted_iota(I32, q_t.shape, 0) // HEAD_DIM
    return [jnp.where(head == h, q_t, jnp.zeros_like(q_t)) for h in range(N_HEADS)]


def _src_spec(arr, layer):
    if arr.ndim == 4:
        return pl.BlockSpec((1, 1) + arr.shape[2:], lambda b, i: (layer, b, 0, 0))
    return pl.BlockSpec((1,) + arr.shape[1:], lambda b, i: (b, 0, 0))


def _q_spec(c, tq):
    return pl.BlockSpec((1, c, tq), lambda b, i: (b, 0, i))


def _o_spec(tq):
    return pl.BlockSpec((1, tq, BRANCH), lambda b, i: (b, i, 0))


def _dsa_kernel(qa_ref, qi_ref, qsm_ref, *refs, n_src, tq, tk, past_len, l_valid, n_sel, t_valid, single_q):
    k_refs, v_refs, ki_refs = refs[:n_src], refs[n_src:2 * n_src], refs[2 * n_src:3 * n_src]
    o_ref, kb_sc, vt_sc, kic_sc, lhs_sc, key_sc, hi_sc, lo_sc, thr_sc, room_sc, tie_sc = refs[3 * n_src:]
    qb = pl.program_id(1)

    @pl.when(qb == 0)
    def _():
        _fill_token_major(kb_sc, [_load2d(r) for r in k_refs])
        _fill_channel_major(vt_sc, [_load2d(r) for r in v_refs])
        cats = []
        for r in ki_refs:
            ki = _load2d(r)[SLAB_KI:SLAB_KI + IDX_DIM, :]
            hi = ki.astype(BF16).astype(F32)
            cats.append(jnp.concatenate([hi, ki - hi, hi, jnp.zeros_like(hi)], axis=0))
        _fill_token_major(kic_sc, cats)

    qi = qi_ref[0]
    for h in range(IDX_HEADS):
        piece = qi[h * IDX_DIM:(h + 1) * IDX_DIM, :]
        hi = piece.astype(BF16).astype(F32)
        lhs_sc[h] = jnp.concatenate([hi, hi, piece - hi, jnp.zeros_like(hi)], axis=0).astype(BF16)
    w_t = qsm_ref[0]

    qlane = lax.broadcasted_iota(I32, (1, tq), 1)
    if single_q:
        row0 = past_len
        top_limit = min(((row0 + tq - 1) // CHUNK + 1) * CHUNK, l_valid)
        count_loop = functools.partial(lax.fori_loop, unroll=True)
    else:
        row0 = past_len + qb * tq
        top_limit = jnp.minimum(((row0 + tq - 1) // CHUNK + 1) * CHUNK, l_valid)
        count_loop = lax.fori_loop
    pos = row0 + qlane
    limit = jnp.minimum((pos // CHUNK + 1) * CHUNK, l_valid)
    nblk = (top_limit + tk - 1) // tk

    def score_blk(j, _):
        off = pl.multiple_of(j * tk, tk)
        kc = kic_sc[pl.ds(off, tk), :]
        dots = [_dot(kc, lhs_sc[h]) for h in range(IDX_HEADS)]
        acc = jnp.zeros((tk, tq), F32)
        for h in range(IDX_HEADS):
            acc = acc + w_t[SLAB_W + h:SLAB_W + h + 1, :] * jnp.maximum(dots[h], 0.0)
        acc = jnp.where(acc == 0.0, 0.0, acc)
        bits = pltpu.bitcast(acc, I32)
        key = jnp.where(bits < 0, bits ^ 0x7FFFFFFF, bits)
        kidx = off + lax.broadcasted_iota(I32, (tk, tq), 0)
        key = jnp.where(kidx < limit, key, INT_MIN)
        key_sc[pl.ds(off, tk), :] = key
        hi_sc[pl.ds(off, tk), :] = (key >> 16).astype(I16)
        lo_sc[pl.ds(off, tk), :] = ((key & 0xFFFF) + I16_MIN).astype(I16)
        return 0

    lax.fori_loop(0, nblk, score_blk, 0)

    def count16(ref, cand):
        cand_b = jnp.broadcast_to(cand, (16, tq)).astype(I16)

        def body(j, part):
            off = pl.multiple_of(j * tk, tk)
            kb = ref[pl.ds(off, tk), :]
            for c in range(tk // 16):
                part = part + jnp.where(kb[c * 16:(c + 1) * 16, :] >= cand_b, jnp.int16(1), jnp.int16(0))
            return part

        part = count_loop(0, nblk, body, jnp.zeros((16, tq), I16))
        return jnp.sum(part.astype(F32), axis=0, keepdims=True)

    def count32(cand):
        cand_b = jnp.broadcast_to(cand, (8, tq))

        def body(j, part):
            off = pl.multiple_of(j * tk, tk)
            kb = key_sc[pl.ds(off, tk), :]
            for c in range(tk // 8):
                part = part + jnp.where(kb[c * 8:(c + 1) * 8, :] >= cand_b, 1.0, 0.0)
            return part

        part = count_loop(0, nblk, body, jnp.zeros((8, tq), F32))
        return jnp.sum(part, axis=0, keepdims=True)

    def kth16(ref, want):
        t = jnp.where(count16(ref, jnp.zeros((1, tq), I32)) >= want, 0, I16_MIN).astype(I32)

        def search(i, t):
            cand = t | jnp.left_shift(jnp.int32(1), 14 - i)
            return jnp.where(count16(ref, cand) >= want, cand, t)

        return lax.fori_loop(0, 15, search, t)

    kf = float(n_sel)
    thr_sc[...] = jnp.full(thr_sc.shape, INT_MIN + 1, I32)
    room_sc[...] = jnp.zeros(room_sc.shape, F32)
    tie_sc[...] = jnp.zeros(tie_sc.shape, F32)

    @pl.when(top_limit > n_sel)
    def _():
        t_hi = kth16(hi_sc, kf)
        above = jnp.where(t_hi >= -I16_MIN - 1, 0.0, count16(hi_sc, t_hi + 1))
        t_hi_b = jnp.broadcast_to(t_hi, (16, tq)).astype(I16)

        def keep_equal_hi(j, _):
            off = pl.multiple_of(j * tk, tk)
            hi_blk = hi_sc[pl.ds(off, tk), :]
            lo_blk = lo_sc[pl.ds(off, tk), :]
            kept = [jnp.where(hi_blk[c * 16:(c + 1) * 16, :] == t_hi_b, lo_blk[c * 16:(c + 1) * 16, :],
                              jnp.int16(I16_MIN)) for c in range(tk // 16)]
            lo_sc[pl.ds(off, tk), :] = jnp.concatenate(kept, axis=0)
            return 0

        count_loop(0, nblk, keep_equal_hi, 0)
        t_lo = kth16(lo_sc, kf - above)
        t_full = jnp.maximum(t_hi * 65536 + (t_lo - I16_MIN), INT_MIN + 1)
        cnt_ge = count32(t_full)
        cnt_gt = count32(t_full + 1)
        real_q = qlane < (t_valid - qb * tq)
        thr_sc[...] = jnp.broadcast_to(t_full, thr_sc.shape)
        room_sc[...] = jnp.broadcast_to(kf - cnt_gt, room_sc.shape)
        tie_sc[...] = jnp.broadcast_to(jnp.where(real_q, jnp.where(cnt_ge > kf, 1.0, 0.0), 0.0), tie_sc.shape)

    thr = thr_sc[0:1, :]
    room = room_sc[0:1, :]
    any_tie = jnp.max(tie_sc[0:1, :]) > 0.0

    @pl.when(any_tie)
    def _():
        ri = lax.broadcasted_iota(I32, (tk, tk), 0)
        ci = lax.broadcasted_iota(I32, (tk, tk), 1)
        upto = jnp.where(ci <= ri, 1.0, 0.0).astype(BF16)

        def drop_late_ties(j, seen):
            off = pl.multiple_of(j * tk, tk)
            kb = key_sc[pl.ds(off, tk), :]
            eq = jnp.where(kb == thr, 1.0, 0.0)
            rank = _dot(upto, eq.astype(BF16)) + seen
            key_sc[pl.ds(off, tk), :] = jnp.where(eq * rank > room, INT_MIN, kb)
            return seen + jnp.sum(eq, axis=0, keepdims=True)

        lax.fori_loop(0, nblk, drop_late_ties, jnp.zeros((1, tq), F32))

    qh = _masked_heads_t(qa_ref[0])

    def attend(j, carry):
        off = pl.multiple_of(j * tk, tk)
        kblk = kb_sc[pl.ds(off, tk), :]
        keep = key_sc[pl.ds(off, tk), :] >= thr
        scores = [jnp.where(keep, _dot(kblk, qh[h]), NEG) for h in range(N_HEADS)]
        return _flash_step(carry, scores, vt_sc, off, tk)

    carry = lax.fori_loop(0, nblk, attend, _flash_init(tq))
    o_ref[0] = _flash_finish(carry).astype(o_ref.dtype)


def _dsa(qa, qi, qsm, k_srcs, v_srcs, ki_srcs, *, layer, lp, tq, tk, past_len, l_valid, n_sel, t_valid):
    bsz, _, t = qa.shape
    n_src = len(k_srcs)
    srcs = list(k_srcs) + list(v_srcs) + list(ki_srcs)
    return pl.pallas_call(
        functools.partial(_dsa_kernel, n_src=n_src, tq=tq, tk=tk, past_len=past_len, l_valid=l_valid,
                          n_sel=n_sel, t_valid=t_valid, single_q=(t == tq)),
        grid=(bsz, t // tq),
        in_specs=[_q_spec(BRANCH, tq), _q_spec(BRANCH, tq), _q_spec(SLAB, tq)] + [_src_spec(a, layer) for a in srcs],
        out_specs=_o_spec(tq),
        out_shape=jax.ShapeDtypeStruct((bsz, t, BRANCH), BF16),
        scratch_shapes=[pltpu.VMEM((lp, BRANCH), BF16), pltpu.VMEM((BRANCH, lp), BF16),
                        pltpu.VMEM((lp, SLAB), BF16), pltpu.VMEM((IDX_HEADS, SLAB, tq), BF16),
                        pltpu.VMEM((lp, tq), I32), pltpu.VMEM((lp, tq), I16), pltpu.VMEM((lp, tq), I16),
                        pltpu.VMEM((SUB, tq), I32), pltpu.VMEM((SUB, tq), F32), pltpu.VMEM((SUB, tq), F32)],
        compiler_params=_params(2),
        name="dsa",
    )(qa, qi, qsm, *srcs)


def _fox_kernel(q_ref, *refs, n_src, f_row, tq, tk, past_len):
    k_refs, v_refs, f_refs = refs[:n_src], refs[n_src:2 * n_src], refs[2 * n_src:3 * n_src]
    o_ref, vt_sc, kaug_sc, f_sc = refs[3 * n_src:]
    qb = pl.program_id(1)
    lp = f_sc.shape[1]

    @pl.when(qb == 0)
    def _():
        _fill_channel_major(vt_sc, [_load2d(r) for r in v_refs])
        ks = [_load2d(r) for r in k_refs]
        fs = []
        for r, row in zip(f_refs, f_row):
            logf = _load2d(r)
            if row + F_ROWS <= logf.shape[0]:
                fs.append(logf[row:row + F_ROWS, :])
            else:
                gates = logf[row:row + N_HEADS, :]
                fs.append(jnp.concatenate([gates, jnp.zeros((F_ROWS - N_HEADS, gates.shape[1]), F32)], axis=0))
        kt = ks[0] if len(ks) == 1 else jnp.concatenate(ks, axis=1)
        ft = fs[0] if len(fs) == 1 else jnp.concatenate(fs, axis=1)
        n_real = kt.shape[1]
        ri = lax.broadcasted_iota(I32, (F_BLK, F_BLK), 0)
        ci = lax.broadcasted_iota(I32, (F_BLK, F_BLK), 1)
        upto = jnp.where(ri <= ci, 1.0, 0.0).astype(BF16)
        rowx = lax.broadcasted_iota(I32, (HEAD_DIM, F_BLK), 0)
        run = jnp.zeros((F_ROWS, 1), F32)
        for b in range(lp // F_BLK):
            lo, hi_ = b * F_BLK, min((b + 1) * F_BLK, n_real)
            if hi_ <= lo:
                kaug_sc[lo:lo + F_BLK, :] = jnp.zeros((F_BLK, N_HEADS * AUG), BF16)
                f_sc[:, lo:lo + F_BLK] = jnp.broadcast_to(run * LOG2E, (F_ROWS, F_BLK))
                continue
            k_blk, f_blk = kt[:, lo:hi_], ft[:, lo:hi_]
            if hi_ - lo < F_BLK:
                k_blk = jnp.concatenate([k_blk, jnp.zeros((BRANCH, F_BLK - (hi_ - lo)), F32)], axis=1)
                f_blk = jnp.concatenate([f_blk, jnp.zeros((F_ROWS, F_BLK - (hi_ - lo)), F32)], axis=1)
            s1, s2, s3 = _split3(f_blk)
            fsum = _dot(s1, upto) + _dot(s2, upto) + _dot(s3, upto) + run
            run = fsum[:, F_BLK - 1:F_BLK]
            fsum = fsum * LOG2E
            f_sc[:, lo:lo + F_BLK] = fsum
            f1, f2, f3 = (p.astype(F32) for p in _split3(fsum))
            for h in range(N_HEADS):
                extra = jnp.where(rowx < 3, 1.0,
                                  jnp.where(rowx == 3, -f1[h:h + 1, :],
                                            jnp.where(rowx == 4, -f2[h:h + 1, :],
                                                      jnp.where(rowx == 5, -f3[h:h + 1, :], 0.0))))
                aug_t = jnp.concatenate([k_blk[h * HEAD_DIM:(h + 1) * HEAD_DIM, :], extra], axis=0)
                kaug_sc[lo:lo + F_BLK, h * AUG:(h + 1) * AUG] = aug_t.T.astype(BF16)

    row0 = past_len + qb * tq
    pos = row0 + lax.broadcasted_iota(I32, (1, tq), 1)
    n_full = row0 // tk
    n_all = (row0 + tq + tk - 1) // tk
    q_t = q_ref[0].astype(F32)
    f1, f2, f3 = (p.astype(F32) for p in _split3(f_sc[:, pl.ds(pl.multiple_of(row0, Q_PAD), tq)]))
    rowq = lax.broadcasted_iota(I32, (HEAD_DIM, tq), 0)
    qh = []
    for h in range(N_HEADS):
        extra = jnp.where(rowq == 0, f1[h:h + 1, :],
                          jnp.where(rowq == 1, f2[h:h + 1, :],
                                    jnp.where(rowq == 2, f3[h:h + 1, :], jnp.where(rowq < 6, 1.0, 0.0))))
        qh.append(jnp.concatenate([q_t[h * HEAD_DIM:(h + 1) * HEAD_DIM, :], extra], axis=0).astype(BF16))

    def step(j, carry, masked):
        off = pl.multiple_of(j * tk, tk)
        scores = [_dot(kaug_sc[pl.ds(off, tk), h * AUG:(h + 1) * AUG], qh[h]) for h in range(N_HEADS)]
        if masked:
            causal = off + lax.broadcasted_iota(I32, (tk, tq), 0) <= pos
            scores = [jnp.where(causal, s, NEG) for s in scores]
        return _flash_step(carry, scores, vt_sc, off, tk)

    carry = lax.fori_loop(0, n_full, lambda j, c: step(j, c, False), _flash_init(tq))
    carry = lax.fori_loop(n_full, n_all, lambda j, c: step(j, c, True), carry)
    o_ref[0] = _flash_finish(carry).astype(o_ref.dtype)


def _fox(q, k_srcs, v_srcs, f_srcs, f_row, *, layer, lp, tq, tk, past_len):
    bsz, _, t = q.shape
    n_src = len(k_srcs)
    srcs = list(k_srcs) + list(v_srcs) + list(f_srcs)
    return pl.pallas_call(
        functools.partial(_fox_kernel, n_src=n_src, f_row=tuple(f_row), tq=tq, tk=tk, past_len=past_len),
        grid=(bsz, t // tq),
        in_specs=[_q_spec(BRANCH, tq)] + [_src_spec(a, layer) for a in srcs],
        out_specs=_o_spec(tq),
        out_shape=jax.ShapeDtypeStruct((bsz, t, BRANCH), BF16),
        scratch_shapes=[pltpu.VMEM((BRANCH, lp), BF16), pltpu.VMEM((lp, N_HEADS * AUG), BF16),
                        pltpu.VMEM((F_ROWS, lp), F32)],
        compiler_params=_params(2),
        name="fox",
    )(q, *srcs)


def _sb_kernel(q_ref, *refs, n_src, tq, tk, past_len):
    k_refs, v_refs = refs[:n_src], refs[n_src:2 * n_src]
    o_ref, kb_sc, vt_sc = refs[2 * n_src:]
    qb = pl.program_id(1)

    @pl.when(qb == 0)
    def _():
        _fill_token_major(kb_sc, [_load2d(r) for r in k_refs])
        _fill_channel_major(vt_sc, [_load2d(r) for r in v_refs])

    qh = _masked_heads_t(q_ref[0])
    row0 = past_len + qb * tq
    pos = row0 + lax.broadcasted_iota(I32, (1, tq), 1)
    j_top = (row0 + tq - 1) // tk
    ri = lax.broadcasted_iota(I32, (tk, tk), 0)
    ci = lax.broadcasted_iota(I32, (tk, tk), 1)
    after = jnp.where(ci > ri, 1.0, 0.0).astype(BF16)

    def cond(carry):
        j, state = carry
        live = state[0][0]
        for h in range(1, N_HEADS):
            live = jnp.maximum(live, state[h][0])
        return jnp.logical_and(j >= 0, jnp.max(live) > -SB_CUTOFF)

    def body(carry):
        j, state = carry
        off = pl.multiple_of(j * tk, tk)
        kblk = kb_sc[pl.ds(off, tk), :]
        zs = [_dot(kblk, qh[h]) for h in range(N_HEADS)]
        strict = off + lax.broadcasted_iota(I32, (tk, tq), 0) < pos
        sps = [jnp.maximum(z, 0.0) + jnp.log2(1.0 + jnp.exp2(-jnp.abs(z))) for z in zs]
        keeps = [jnp.where(strict, -sp, 0.0) for sp in sps]
        laters = []
        for h in range(N_HEADS):
            k_hi, k_lo = _split2(keeps[h])
            laters.append(_dot(after, k_hi) + _dot(after, k_lo) + state[h][0])
        ws = [jnp.where(strict, jnp.exp2(zs[h] - sps[h] + laters[h]), 0.0) for h in range(N_HEADS)]
        pvs = [_dot(vt_sc[h * HEAD_DIM:(h + 1) * HEAD_DIM, pl.ds(off, tk)], ws[h].astype(BF16))
               for h in range(N_HEADS)]
        new = tuple((state[h][0] + jnp.sum(keeps[h], axis=0, keepdims=True), state[h][1] + pvs[h])
                    for h in range(N_HEADS))
        return j - 1, new

    init = tuple((jnp.zeros((1, tq), F32), jnp.zeros((HEAD_DIM, tq), F32)) for _ in range(N_HEADS))
    _, state = lax.while_loop(cond, body, (j_top, init))
    out_t = jnp.concatenate([acc for (_, acc) in state], axis=0)
    o_ref[0] = out_t.T.astype(o_ref.dtype)


def _sb(q, k_srcs, v_srcs, *, layer, lp, tq, tk, past_len):
    bsz, _, t = q.shape
    n_src = len(k_srcs)
    srcs = list(k_srcs) + list(v_srcs)
    return pl.pallas_call(
        functools.partial(_sb_kernel, n_src=n_src, tq=tq, tk=tk, past_len=past_len),
        grid=(bsz, t // tq),
        in_specs=[_q_spec(BRANCH, tq)] + [_src_spec(a, layer) for a in srcs],
        out_specs=_o_spec(tq),
        out_shape=jax.ShapeDtypeStruct((bsz, t, BRANCH), BF16),
        scratch_shapes=[pltpu.VMEM((lp, BRANCH), BF16), pltpu.VMEM((BRANCH, lp), BF16)],
        compiler_params=_params(2),
        name="sb",
    )(q, *srcs)


def _merge_kernel(x_ref, mod_ref, g1_ref, oa_ref, ob_ref, oc_ref, od_ref, wg_ref, bg_ref, wb_ref, wo_ref,
                  xo_ref, *, nb, tt):
    m = nb * tt
    d = x_ref.shape[-1]
    x = x_ref[...]
    mod = mod_ref[...]
    h = _rms(x, g1_ref[...]) * (1.0 + mod[:, 1:2, :]) + mod[:, 0:1, :]
    hb = h.reshape(m, d).astype(BF16)
    o_refs = (oa_ref, ob_ref, oc_ref, od_ref)

    def pre(i):
        return (_dot(hb, wg_ref[:, i * d:(i + 1) * d]), _dot(o_refs[i][...].reshape(m, BRANCH), wb_ref[i]))

    merged = None
    cur = pre(0)
    for i in range(N_BRANCH):
        nxt = pre(i + 1) if i + 1 < N_BRANCH else None
        term = jax.nn.sigmoid(cur[0] + bg_ref[:, i * d:(i + 1) * d]) * cur[1]
        merged = term if merged is None else merged + term
        cur = nxt
    y = _dot(merged.astype(BF16), wo_ref[...])
    xo_ref[...] = x + mod[:, 2:3, :] * y.reshape(nb, tt, d)


def _merge(x, mod, g1, oa, ob, oc, od, w_gate, b_gate, w_branch, w_out, *, nb, tt):
    bsz, t, d = x.shape
    tok = lambda w: pl.BlockSpec((nb, tt, w), lambda bi, ti: (bi, ti, 0))
    const = lambda a: pl.BlockSpec(a.shape, lambda bi, ti: (0,) * a.ndim)
    return pl.pallas_call(
        functools.partial(_merge_kernel, nb=nb, tt=tt),
        grid=(bsz // nb, t // tt),
        in_specs=[tok(d), pl.BlockSpec((nb, 6, d), lambda bi, ti: (bi, 0, 0)), const(g1),
                  tok(BRANCH), tok(BRANCH), tok(BRANCH), tok(BRANCH),
                  const(w_gate), const(b_gate), const(w_branch), const(w_out)],
        out_specs=tok(d),
        out_shape=jax.ShapeDtypeStruct((bsz, t, d), F32),
        compiler_params=_params(2),
        name="merge",
    )(x, mod, g1, oa, ob, oc, od, w_gate, b_gate, w_branch, w_out)


def _ffn_kernel(x_ref, mod_ref, g2_ref, pfx_ref, wu_ref, wc_ref, bc_ref, wd_ref, fg_ref,
                xo_ref, st_ref, carry_sc, ext_sc, *, nb, tt, cw, final_norm):
    m = nb * tt
    d = x_ref.shape[-1]
    dff = wd_ref.shape[0]
    ti = pl.program_id(1)
    keep = CONV_W - 1

    @pl.when(ti == 0)
    def _():
        carry_sc[...] = jnp.zeros_like(carry_sc)
        carry_sc[:, SUB - keep:, :] = pfx_ref[...]

    x = x_ref[...]
    mod = mod_ref[...]
    h = _rms(x, g2_ref[...]) * (1.0 + mod[:, 4:5, :]) + mod[:, 3:4, :]
    hb = h.reshape(m, d).astype(BF16)

    def up_cols(c0):
        return _dot(hb, wu_ref[:, c0:c0 + cw]).reshape(nb, tt, cw)

    def stage(up, c0, slot):
        ext_sc[slot, :, 0:SUB, :] = carry_sc[:, :, c0:c0 + cw]
        ext_sc[slot, :, SUB:, :] = up
        carry_sc[:, :, c0:c0 + cw] = up[:, tt - SUB:, :]

    def conv_rows(c0, slot, r0, rows):
        wc = wc_ref[:, c0:c0 + cw]
        win = lambda back: ext_sc[slot, :, SUB - back + r0:SUB - back + r0 + rows, :]
        conv = win(2) * wc[0:1, :] + win(1) * wc[1:2, :] + win(0) * wc[2:3, :] + bc_ref[:, c0:c0 + cw]
        return conv.reshape(nb * rows, cw)

    n_half = 2 if (nb == 1 and tt % (2 * SUB) == 0) else 1
    rows = tt // n_half
    accs = [jnp.zeros((nb * rows, d), F32) for _ in range(n_half)]
    n_chunks = dff // cw
    ups = (up_cols(0), up_cols(dff))
    for j in range(n_chunks):
        nxt = (up_cols((j + 1) * cw), up_cols(dff + (j + 1) * cw)) if j + 1 < n_chunks else None
        slot = 2 * (j % 2)
        stage(ups[0], j * cw, slot)
        stage(ups[1], dff + j * cw, slot + 1)
        for hh in range(n_half):
            gate = conv_rows(j * cw, slot, hh * rows, rows)
            val = conv_rows(dff + j * cw, slot + 1, hh * rows, rows)
            act = (gate * jax.nn.sigmoid(gate) * val).astype(BF16)
            accs[hh] = accs[hh] + _dot(act, wd_ref[j * cw:(j + 1) * cw, :])
        ups = nxt
    acc = accs[0] if n_half == 1 else jnp.concatenate(accs, axis=0)
    xo = x + mod[:, 5:6, :] * acc.reshape(nb, tt, d)
    if final_norm:
        xo = _rms(xo, fg_ref[...])
    xo_ref[...] = xo

    @pl.when(ti == pl.num_programs(1) - 1)
    def _():
        st_ref[...] = carry_sc[:, SUB - keep:, :]


def _ffn(x, mod, g2, prefix, w_up, w_conv, b_conv, w_down, final_g, *, nb, tt, final_norm):
    bsz, t, d = x.shape
    dff = w_down.shape[0]
    cw = 256
    tok = pl.BlockSpec((nb, tt, d), lambda bi, ti: (bi, ti, 0))
    const = lambda a: pl.BlockSpec(a.shape, lambda bi, ti: (0,) * a.ndim)
    state = pl.BlockSpec((nb, CONV_W - 1, 2 * dff), lambda bi, ti: (bi, 0, 0))
    return pl.pallas_call(
        functools.partial(_ffn_kernel, nb=nb, tt=tt, cw=cw, final_norm=final_norm),
        grid=(bsz // nb, t // tt),
        in_specs=[tok, pl.BlockSpec((nb, 6, d), lambda bi, ti: (bi, 0, 0)), const(g2), state,
                  const(w_up), const(w_conv), const(b_conv), const(w_down), const(final_g)],
        out_specs=[tok, state],
        out_shape=[jax.ShapeDtypeStruct((bsz, t, d), F32),
                   jax.ShapeDtypeStruct((bsz, CONV_W - 1, 2 * dff), F32)],
        scratch_shapes=[pltpu.VMEM((nb, SUB, 2 * dff), F32), pltpu.VMEM((4, nb, tt + SUB, cw), F32)],
        compiler_params=_params(2),
        name="ffn",
    )(x, mod, g2, prefix, w_up, w_conv, b_conv, w_down, final_g)


def _rope_tables_t(pos):
    def tab(half, heads, pad):
        inv = ROPE_THETA ** (-jnp.arange(half, dtype=F32) / half)
        ang = inv[:, None] * pos.astype(F32)[None, :]
        cos, sin = jnp.cos(ang), jnp.sin(ang)
        cos_t = jnp.tile(jnp.concatenate([cos, cos], axis=0), (heads, 1))
        sin_t = jnp.tile(jnp.concatenate([-sin, sin], axis=0), (heads, 1))
        if pad:
            cos_t = jnp.concatenate([cos_t, jnp.ones((pad, pos.shape[0]), F32)], axis=0)
            sin_t = jnp.concatenate([sin_t, jnp.zeros((pad, pos.shape[0]), F32)], axis=0)
        return cos_t, sin_t

    cosa, sina = tab(HEAD_DIM // 2, N_HEADS, 0)
    cosi, sini = tab(IDX_DIM // 2, IDX_HEADS, 0)
    coss, sins = tab(IDX_DIM // 2, 1, SLAB - IDX_DIM)
    return cosa, sina, cosi, sini, coss, sins


def _layout_w_in(w_in_t, layer):
    sizes = (BRANCH, BRANCH, BRANCH, IDX_HEADS * IDX_DIM, IDX_DIM, IDX_HEADS, BRANCH, BRANCH,
             BRANCH, BRANCH, BRANCH, N_HEADS, BRANCH, BRANCH, BRANCH)
    offs = [0]
    for s in sizes:
        offs.append(offs[-1] + s)
    piece = lambda i: w_in_t[offs[i]:offs[i + 1], layer, :]
    a_q, a_k, a_v, a_qi, a_ki, a_w, b_u, b_v, c_q, c_k, c_v, c_f, d_q, d_k, d_v = (piece(i) for i in range(15))
    pad = jnp.zeros((SLAB - IDX_DIM - IDX_HEADS - N_HEADS, w_in_t.shape[-1]), w_in_t.dtype)
    return jnp.concatenate([a_q, a_k, a_v, a_qi, c_q, c_k, c_v, d_q, d_k, d_v, b_u, b_v,
                            a_ki, a_w, c_f, pad], axis=0).astype(BF16)


def _block_diag_tril_t(w_s, n, reps):
    tri = jnp.tril(jnp.ones((n, n), dtype=bool))
    w = jnp.where(tri[None], w_s[:, :n, :n], 0)
    eye = jnp.eye(reps, dtype=w.dtype)
    return jnp.einsum("ab,gts->gbsat", eye, w).reshape(w.shape[0], reps * n, reps * n).astype(BF16)


def _round_up(a, b):
    return (a + b - 1) // b * b


def _channel_major(cache):
    depth, bsz, length = cache.shape[:3]
    flat = cache.reshape(depth, bsz, length, -1)
    return jnp.swapaxes(flat, 2, 3)


def _layer(x, mod, lp, past, prefix, final_g, stacked, *, layer, depth, final_norm, past_len):
    bsz, t, d = x.shape
    is_prompt = past is None
    n = min(t, B_CHUNK)
    if is_prompt:
        nb, tt = 1, min(t, 512)
        reps = tt // n
        pos_rows = jnp.arange(t)
    else:
        nb, tt = bsz, t
        reps = (nb * tt) // n
        pos_rows = jnp.tile(past_len + jnp.arange(t), nb)
    m = nb * tt
    tables = _rope_tables_t(pos_rows)
    ws = _block_diag_tril_t(lp["w_s"], n, reps)
    bs = jnp.tile(jnp.repeat(lp["b_s"][:, :n], HEAD_DIM, axis=0), (1, reps))
    fb = jnp.broadcast_to(lp["fb"], (SLAB, m))
    lng = jnp.broadcast_to(lp["lng"], (BRANCH, m))
    lnb = jnp.broadcast_to(lp["lnb"], (BRANCH, m))
    outs = _proj(x, mod, lp["g1"], lp["w_t"], tables, fb, lng, lnb, ws, bs,
                 nb=nb, tt=tt, stacked=stacked if is_prompt else None, layer=layer, depth=depth)
    kv, (qa, qc, qd, qi, sm, ob, vb) = outs[:6], outs[6:]

    l_valid = past_len + t
    tqp = _round_up(t, Q_PAD)
    if is_prompt:
        tk = min(512, t)
        tk_d = min(256, t)
        lpad = t
        tq = min(256, tqp)
        n_sel = min(TOPK_MAX, t // 4)
        ka, va, kc, vc, kd, vd = ([a] for a in kv)
        ki_srcs, f_srcs, f_row = [sm], [sm], (SLAB_F,)
        att_layer = layer
    else:
        tk = tk_d = 256
        tq = tqp
        lpad = _round_up(past_len + tqp, tk)
        n_sel = min(TOPK_MAX, l_valid // 4)

        def tokens(a):
            a = jnp.swapaxes(a.reshape(a.shape[0], bsz, t), 0, 1)
            return jnp.pad(a, ((0, 0), (0, 0), (0, tqp - t)))

        pk_a, pv_a, pki, pk_c, pv_c, plogf, pk_d, pv_d = past
        qa, qc, qd, qi, sm_tok = (tokens(a) for a in (qa, qc, qd, qi, sm))
        new = [tokens(a) for a in kv]
        ka, va, kc, vc, kd, vd = ([p_, n_] for p_, n_ in zip((pk_a, pv_a, pk_c, pv_c, pk_d, pv_d), new))
        ki_srcs, f_srcs, f_row = [pki, sm_tok], [plogf, sm_tok], (0, SLAB_F)
        att_layer = layer
    sm_q = sm if is_prompt else sm_tok

    oa = _dsa(qa, qi, sm_q, ka, va, ki_srcs, layer=att_layer, lp=lpad, tq=tq, tk=tk, past_len=past_len,
              l_valid=l_valid, n_sel=n_sel, t_valid=t)[:, :t]
    oc = _fox(qc, kc, vc, f_srcs, f_row, layer=att_layer, lp=lpad, tq=tq, tk=tk, past_len=past_len)[:, :t]
    od = _sb(qd, kd, vd, layer=att_layer, lp=lpad, tq=tq, tk=tk_d, past_len=past_len)[:, :t]

    mtt = min(t, 512) if is_prompt else tt
    x = _merge(x, mod, lp["g1"], oa, ob, oc, od, lp["w_gate"], lp["b_gate"], lp["w_branch"], lp["w_out"],
               nb=nb, tt=mtt)
    x, conv_state = _ffn(x, mod, lp["g2"], prefix, lp["w_up"], lp["w_conv"], lp["b_conv"], lp["w_down"],
                         final_g, nb=nb, tt=tt, final_norm=final_norm)
    if is_prompt:
        rows = (sm, conv_state)
    else:
        heads = lambda a: jnp.transpose(a.reshape(N_HEADS, HEAD_DIM, bsz, t), (2, 3, 0, 1))
        small = lambda a, r0, r1: jnp.transpose(a[r0:r1].reshape(r1 - r0, bsz, t), (1, 2, 0))
        ka_n, va_n, kc_n, vc_n, kd_n, vd_n = kv
        rows = (heads(ka_n), heads(va_n), small(sm, SLAB_KI, SLAB_KI + IDX_DIM), vb, heads(kc_n), heads(vc_n),
                small(sm, SLAB_F, SLAB_F + N_HEADS), heads(kd_n), heads(vd_n), conv_state)
    return x, rows, kv


def kernel(x_prompt, x_sample, cache_a_k, cache_a_v, cache_a_kidx, cache_c_k, cache_c_v, cache_c_logf, cache_d_k, cache_d_v, state_ffn_conv, c_prompt, c_sample, norm1_g, norm2_g, w_mod, b_mod, w_in, f_bias, lnv_g, lnv_b, w_spatial, b_spatial, w_branch, w_gate, b_gate, w_out, w_up, w_conv, b_conv, w_down, final_g):
    depth = w_in.shape[0]
    bsz, t_prompt, d = x_prompt.shape
    dbsz = x_sample.shape[0]
    past_len = cache_a_k.shape[2]
    mods = _modulation(jnp.concatenate([c_prompt, c_sample], axis=0), w_mod, b_mod)
    mods = mods.reshape(depth, bsz + dbsz, 6, d)
    prefix = jnp.zeros((bsz, CONV_W - 1, w_up.shape[-1]), x_prompt.dtype)
    fg = final_g.reshape(1, d)
    past = tuple(_channel_major(c) for c in (cache_a_k, cache_a_v, cache_a_kidx, cache_c_k, cache_c_v,
                                              cache_c_logf.astype(F32), cache_d_k, cache_d_v))
    w_in_t = jnp.transpose(w_in, (2, 0, 1))
    xp, xs = x_prompt, x_sample
    rows_p, rows_s = [], []
    stacked = ()
    for l in range(depth):
        fb = jnp.zeros((SLAB, 1), F32).at[SLAB_F:SLAB_F + N_HEADS, 0].set(f_bias[l].astype(F32))
        lp = dict(g1=norm1_g[l].reshape(1, d), g2=norm2_g[l].reshape(1, d), w_t=_layout_w_in(w_in_t, l), fb=fb,
                  lng=lnv_g[l].reshape(BRANCH, 1), lnb=lnv_b[l].reshape(BRANCH, 1),
                  w_s=w_spatial[l], b_s=b_spatial[l], w_branch=w_branch[l].astype(BF16),
                  w_gate=w_gate[l].astype(BF16), b_gate=b_gate[l].reshape(1, -1), w_out=w_out[l].astype(BF16),
                  w_up=w_up[l].astype(BF16), w_conv=w_conv[l], b_conv=b_conv[l].reshape(1, -1),
                  w_down=w_down[l].astype(BF16))
        last = l == depth - 1
        xp, new_p, stacked = _layer(xp, mods[l, :bsz], lp, None, prefix, fg, stacked, layer=l, depth=depth,
                                    final_norm=last, past_len=0)
        xs, new_s, _ = _layer(xs, mods[l, bsz:], lp, past, state_ffn_conv[l], fg, None, layer=l, depth=depth,
                              final_norm=last, past_len=past_len)
        rows_p.append(new_p)
        rows_s.append(new_s)

    def stacked_s(i):
        return jnp.stack([r[i] for r in rows_s], axis=0)

    def heads_p(a):
        return jnp.transpose(a.reshape(depth, bsz, N_HEADS, HEAD_DIM, t_prompt), (0, 1, 4, 2, 3))

    sm_p = jnp.stack([r[0] for r in rows_p], axis=0)
    small_p = lambda r0, r1: jnp.swapaxes(sm_p[:, :, r0:r1, :], 2, 3)
    ka_p, va_p, kc_p, vc_p, kd_p, vd_p = (heads_p(a) for a in stacked)
    conv_p = jnp.stack([r[1] for r in rows_p], axis=0)
    return (xp, xs,
            ka_p, stacked_s(0),
            va_p, stacked_s(1),
            small_p(SLAB_KI, SLAB_KI + IDX_DIM), stacked_s(2),
            stacked_s(3),
            kc_p, stacked_s(4),
            vc_p, stacked_s(5),
            small_p(SLAB_F, SLAB_F + N_HEADS), stacked_s(6),
            kd_p, stacked_s(7),
            vd_p, stacked_s(8),
            conv_p, stacked_s(9))
```

```python
import functools

import jax
import jax.numpy as jnp
from jax import lax
from jax.experimental import pallas as pl
from jax.experimental.pallas import tpu as pltpu

F32 = jnp.float32
BF16 = jnp.bfloat16
I32 = jnp.int32
I16 = jnp.int16

HEAD_DIM = 64
N_HEADS = 4
BRANCH = N_HEADS * HEAD_DIM
IDX_HEADS = 8
IDX_DIM = 32
CHUNK = 64
TOPK_MAX = 256
B_CHUNK = 128
N_BRANCH = 4
CONV_W = 3
ROPE_THETA = 10000.0
EPS = 1e-6
LOG2E = 1.4426950408889634
Q_SCALE = HEAD_DIM ** -0.5 * LOG2E

SLAB = 128
SLAB_KI = 0
SLAB_W = 32
SLAB_F = 40
AUG = 128
F_ROWS = 8
F_BLK = 256
N_T_GROUPS = 10
NEG = -1e30
INT_MIN = -2 ** 31
I16_MIN = -2 ** 15
SB_CUTOFF = 220.0
Q_PAD = 128
SUB = 8
VMEM_LIMIT = 56 * 1024 * 1024


def _params(n_axes, vmem=VMEM_LIMIT):
    return pltpu.CompilerParams(dimension_semantics=("arbitrary",) * n_axes, vmem_limit_bytes=vmem)


def _dot(a, b):
    return jnp.dot(a, b, preferred_element_type=F32)


def _split2(x):
    hi = x.astype(BF16)
    lo = (x - hi.astype(F32)).astype(BF16)
    return hi, lo


def _split3(x):
    x1 = x.astype(BF16)
    r = x - x1.astype(F32)
    x2 = r.astype(BF16)
    x3 = (r - x2.astype(F32)).astype(BF16)
    return x1, x2, x3


def _softplus(z):
    return jnp.maximum(z, 0.0) + jnp.log1p(jnp.exp(-jnp.abs(z)))


def _rms(x, g):
    ms = jnp.mean(x * x, axis=-1, keepdims=True)
    return x * lax.rsqrt(ms + EPS) * g


def _rope_t(x, cos, sin_signed, half):
    rows = x.shape[0]
    row = lax.broadcasted_iota(I32, x.shape, 0)
    fwd = pltpu.roll(x, half, axis=0)
    bwd = pltpu.roll(x, rows - half, axis=0)
    rot = jnp.where((row % (2 * half)) < half, bwd, fwd)
    return x * cos + rot * sin_signed


def _mod_kernel(c_ref, w_ref, b_ref, o_ref):
    c = c_ref[...]
    a = c * jax.nn.sigmoid(c)
    a1, a2 = _split2(a)
    w1, w2 = _split2(w_ref[0])
    o_ref[0] = _dot(a1, w1) + _dot(a1, w2) + _dot(a2, w1) + b_ref[0]


def _modulation(c_all, w_mod, b_mod):
    depth, d, n = w_mod.shape
    rows = c_all.shape[0]
    tn = 1536
    return pl.pallas_call(
        _mod_kernel,
        grid=(depth, n // tn),
        in_specs=[pl.BlockSpec((rows, d), lambda l, j: (0, 0)),
                  pl.BlockSpec((1, d, tn), lambda l, j: (l, 0, j)),
                  pl.BlockSpec((1, 1, tn), lambda l, j: (l, 0, j))],
        out_specs=pl.BlockSpec((1, rows, tn), lambda l, j: (l, 0, j)),
        out_shape=jax.ShapeDtypeStruct((depth, rows, n), F32),
        compiler_params=_params(2),
        name="modulation",
    )(c_all, w_mod, b_mod.reshape(depth, 1, n))


def _proj_kernel(x_ref, mod_ref, g1_ref, wt_ref, cosa_ref, sina_ref, cosi_ref, sini_ref, coss_ref,
                 sins_ref, fb_ref, lng_ref, lnb_ref, ws_ref, bs_ref, *rest, nb, tt, n_alias):
    (ka_ref, va_ref, kc_ref, vc_ref, kd_ref, vd_ref,
     qa_ref, qc_ref, qd_ref, qi_ref, sm_ref, ob_ref, vb_ref) = rest[n_alias:]
    m = nb * tt
    d = x_ref.shape[-1]
    x = x_ref[...]
    mod = mod_ref[...]
    h = (_rms(x, g1_ref[...]) * (1.0 + mod[:, 1:2, :]) + mod[:, 0:1, :]).reshape(m, d)
    ht = h.T.astype(BF16)

    def col_t(j):
        return _dot(wt_ref[j * BRANCH:(j + 1) * BRANCH, :], ht)

    def put_t(ref, val):
        ref[...] = val.reshape(ref.shape).astype(ref.dtype)

    def put(ref, val):
        ref[...] = val.reshape(nb, tt, val.shape[-1]).astype(ref.dtype)

    cosa, sina = cosa_ref[...], sina_ref[...]
    put_t(qa_ref, _rope_t(col_t(0), cosa, sina, HEAD_DIM // 2) * Q_SCALE)
    put_t(ka_ref, _rope_t(col_t(1), cosa, sina, HEAD_DIM // 2))
    put_t(va_ref, col_t(2))
    put_t(qi_ref, _rope_t(col_t(3), cosi_ref[...], sini_ref[...], IDX_DIM // 2))
    put_t(qc_ref, col_t(4) * Q_SCALE)
    put_t(kc_ref, col_t(5))
    put_t(vc_ref, col_t(6))
    put_t(qd_ref, col_t(7) * Q_SCALE)
    put_t(kd_ref, col_t(8))
    put_t(vd_ref, col_t(9))
    sm = _dot(wt_ref[(N_T_GROUPS + 2) * BRANCH:(N_T_GROUPS + 2) * BRANCH + SLAB, :], ht)
    row = lax.broadcasted_iota(I32, sm.shape, 0)
    roped = _rope_t(sm, coss_ref[...], sins_ref[...], IDX_DIM // 2)
    logf = -_softplus(-(sm + fb_ref[...]))
    is_f = jnp.where(row >= SLAB_F, jnp.where(row < SLAB_F + N_HEADS, 1, 0), 0)
    put_t(sm_ref, jnp.where(is_f == 1, logf, roped))
    u = jax.nn.gelu(col_t(N_T_GROUPS))
    vg = jax.nn.gelu(col_t(N_T_GROUPS + 1))
    mu = jnp.mean(vg, axis=0, keepdims=True)
    vc_ = vg - mu
    var = jnp.mean(vc_ * vc_, axis=0, keepdims=True)
    vb = vc_ * lax.rsqrt(var + EPS) * lng_ref[...] + lnb_ref[...]
    put(vb_ref, vb.T)
    vbb = vb.astype(BF16)
    mixed = jnp.concatenate([_dot(vbb[g * HEAD_DIM:(g + 1) * HEAD_DIM, :], ws_ref[g]) for g in range(N_HEADS)],
                            axis=0)
    put(ob_ref, (u * (mixed + bs_ref[...])).T)


def _proj(x, mod, g1, w_t, tables, fb, lng, lnb, ws, bs, *, nb, tt, stacked, layer, depth):
    bsz, t, d = x.shape
    m = nb * tt
    n_t = t // tt
    tok = lambda w: pl.BlockSpec((nb, tt, w), lambda ti, bi: (bi, ti, 0))
    const = lambda a: pl.BlockSpec(a.shape, lambda ti, bi: (0,) * a.ndim)
    tab = lambda c: pl.BlockSpec((c, m), lambda ti, bi: (0, ti))
    if stacked is None:
        assert n_t == 1 and nb == bsz
        kv_spec = pl.BlockSpec((BRANCH, m), lambda ti, bi: (0, 0))
        kv_shape = jax.ShapeDtypeStruct((BRANCH, m), F32)
        q_spec = lambda c: pl.BlockSpec((c, m), lambda ti, bi: (0, 0))
        q_shape = lambda c, dt: jax.ShapeDtypeStruct((c, m), dt)
        alias_in = ()
    else:
        assert nb == 1
        kv_spec = pl.BlockSpec((1, 1, BRANCH, tt), lambda ti, bi: (layer, bi, 0, ti))
        kv_shape = jax.ShapeDtypeStruct((depth, bsz, BRANCH, t), F32)
        q_spec = lambda c: pl.BlockSpec((1, c, tt), lambda ti, bi: (bi, 0, ti))
        q_shape = lambda c, dt: jax.ShapeDtypeStruct((bsz, c, t), dt)
        alias_in = tuple(stacked)
    n_alias = len(alias_in)
    n_in = 15
    cosa, sina, cosi, sini, coss, sins = tables
    return pl.pallas_call(
        functools.partial(_proj_kernel, nb=nb, tt=tt, n_alias=n_alias),
        grid=(n_t, bsz // nb),
        in_specs=[tok(d), pl.BlockSpec((nb, 6, d), lambda ti, bi: (bi, 0, 0)), const(g1), const(w_t),
                  tab(BRANCH), tab(BRANCH), tab(BRANCH), tab(BRANCH), tab(SLAB), tab(SLAB),
                  const(fb), const(lng), const(lnb), const(ws), const(bs)]
                 + [pl.BlockSpec(memory_space=pl.ANY)] * n_alias,
        out_specs=[kv_spec] * 6 + [q_spec(BRANCH)] * 4 + [q_spec(SLAB), tok(BRANCH), tok(BRANCH)],
        out_shape=[kv_shape] * 6 + [q_shape(BRANCH, BF16)] * 3 + [q_shape(BRANCH, F32), q_shape(SLAB, F32),
                                                                 jax.ShapeDtypeStruct((bsz, t, BRANCH), BF16),
                                                                 jax.ShapeDtypeStruct((bsz, t, BRANCH), F32)],
        input_output_aliases={n_in + i: i for i in range(n_alias)},
        compiler_params=_params(2),
        name="proj",
    )(x, mod, g1, w_t, cosa, sina, cosi, sini, coss, sins, fb, lng, lnb, ws, bs, *alias_in)


def _load2d(ref):
    return ref[(0,) * (len(ref.shape) - 2)]


def _fill_token_major(dst_sc, srcs):
    off = 0
    for s in srcs:
        n = s.shape[1]
        dst_sc[off:off + n, :] = s.T.astype(BF16)
        off += n
    lp = dst_sc.shape[0]
    if off < lp:
        dst_sc[off:lp, :] = jnp.zeros((lp - off, dst_sc.shape[1]), BF16)


def _fill_channel_major(dst_sc, srcs):
    off = 0
    for s in srcs:
        n = s.shape[1]
        dst_sc[:, off:off + n] = s.astype(BF16)
        off += n
    lp = dst_sc.shape[1]
    if off < lp:
        dst_sc[:, off:lp] = jnp.zeros((dst_sc.shape[0], lp - off), BF16)


def _flash_init(tq):
    return tuple((jnp.full((1, tq), NEG, F32), jnp.zeros((1, tq), F32), jnp.zeros((HEAD_DIM, tq), F32))
                 for _ in range(N_HEADS))


def _flash_step(carry, scores, vt_sc, off, tk):
    ms = [jnp.maximum(carry[h][0], jnp.max(scores[h], axis=0, keepdims=True)) for h in range(N_HEADS)]
    ps = [jnp.exp2(scores[h] - ms[h]) for h in range(N_HEADS)]
    pvs = [_dot(vt_sc[h * HEAD_DIM:(h + 1) * HEAD_DIM, pl.ds(off, tk)], ps[h].astype(BF16))
           for h in range(N_HEADS)]
    new = []
    for h in range(N_HEADS):
        m_run, l_run, acc = carry[h]
        alpha = jnp.exp2(m_run - ms[h])
        new.append((ms[h], alpha * l_run + jnp.sum(ps[h], axis=0, keepdims=True), alpha * acc + pvs[h]))
    return tuple(new)


def _flash_finish(carry):
    out_t = jnp.concatenate([acc / l_run for (_, l_run, acc) in carry], axis=0)
    return out_t.T


def _tree_sum(terms):
    while len(terms) > 1:
        terms = [a + b for a, b in zip(terms[::2], terms[1::2])] + ([terms[-1]] if len(terms) % 2 else [])
    return terms[0]


def _masked_heads_t(q_t):
    head = lax.broadcasted_iota(I32, q_t.shape, 0) // HEAD_DIM
    return [jnp.where(head == h, q_t, jnp.zeros_like(q_t)) for h in range(N_HEADS)]


def _src_spec(arr, layer):
    if arr.ndim == 4:
        return pl.BlockSpec((1, 1) + arr.shape[2:], lambda b, i: (layer, b, 0, 0))
    return pl.BlockSpec((1,) + arr.shape[1:], lambda b, i: (b, 0, 0))


def _q_spec(c, tq):
    return pl.BlockSpec((1, c, tq), lambda b, i: (b, 0, i))


def _o_spec(tq):
    return pl.BlockSpec((1, tq, BRANCH), lambda b, i: (b, i, 0))


def _dsa_kernel(qa_ref, qi_ref, qsm_ref, *refs, n_src, tq, tk, past_len, l_valid, n_sel, t_valid, single_q):
    k_refs, v_refs, ki_refs = refs[:n_src], refs[n_src:2 * n_src], refs[2 * n_src:3 * n_src]
    o_ref, kb_sc, vt_sc, kic_sc, lhs_sc, key_sc, hi_sc, lo_sc, thr_sc, room_sc, tie_sc = refs[3 * n_src:]
    qb = pl.program_id(1)

    @pl.when(qb == 0)
    def _():
        _fill_token_major(kb_sc, [_load2d(r) for r in k_refs])
        _fill_channel_major(vt_sc, [_load2d(r) for r in v_refs])
        cats = []
        for r in ki_refs:
            ki = _load2d(r)[SLAB_KI:SLAB_KI + IDX_DIM, :]
            hi = ki.astype(BF16).astype(F32)
            cats.append(jnp.concatenate([hi, ki - hi, hi, jnp.zeros_like(hi)], axis=0))
        _fill_token_major(kic_sc, cats)

    qi = qi_ref[0]
    for h in range(IDX_HEADS):
        piece = qi[h * IDX_DIM:(h + 1) * IDX_DIM, :]
        hi = piece.astype(BF16).astype(F32)
        lhs_sc[h] = jnp.concatenate([hi, hi, piece - hi, jnp.zeros_like(hi)], axis=0).astype(BF16)
    w_t = qsm_ref[0]

    qlane = lax.broadcasted_iota(I32, (1, tq), 1)
    if single_q:
        row0 = past_len
        top_limit = min(((row0 + tq - 1) // CHUNK + 1) * CHUNK, l_valid)
        count_loop = functools.partial(lax.fori_loop, unroll=True)
        block_loop = functools.partial(lax.fori_loop, unroll=2)
    else:
        row0 = past_len + qb * tq
        top_limit = jnp.minimum(((row0 + tq - 1) // CHUNK + 1) * CHUNK, l_valid)
        count_loop = block_loop = lax.fori_loop
    pos = row0 + qlane
    limit = jnp.minimum((pos // CHUNK + 1) * CHUNK, l_valid)
    nblk = (top_limit + tk - 1) // tk

    def score_blk(j, _):
        off = pl.multiple_of(j * tk, tk)
        kc = kic_sc[pl.ds(off, tk), :]
        dots = [_dot(kc, lhs_sc[h]) for h in range(IDX_HEADS)]
        acc = jnp.zeros((tk, tq), F32)
        for h in range(IDX_HEADS):
            acc = acc + w_t[SLAB_W + h:SLAB_W + h + 1, :] * jnp.maximum(dots[h], 0.0)
        acc = jnp.where(acc == 0.0, 0.0, acc)
        bits = pltpu.bitcast(acc, I32)
        key = jnp.where(bits < 0, bits ^ 0x7FFFFFFF, bits)
        kidx = off + lax.broadcasted_iota(I32, (tk, tq), 0)
        key = jnp.where(kidx < limit, key, INT_MIN)
        key_sc[pl.ds(off, tk), :] = key
        hi_sc[pl.ds(off, tk), :] = (key >> 16).astype(I16)
        lo_sc[pl.ds(off, tk), :] = ((key & 0xFFFF) + I16_MIN).astype(I16)
        return 0

    block_loop(0, nblk, score_blk, 0)

    def count16(ref, cand):
        cand_b = jnp.broadcast_to(cand, (16, tq)).astype(I16)

        def body(j, part):
            off = pl.multiple_of(j * tk, tk)
            kb = ref[pl.ds(off, tk), :]
            return part + _tree_sum([jnp.where(kb[c * 16:(c + 1) * 16, :] >= cand_b, jnp.int16(1), jnp.int16(0))
                                     for c in range(tk // 16)])

        part = count_loop(0, nblk, body, jnp.zeros((16, tq), I16))
        return jnp.sum(part.astype(F32), axis=0, keepdims=True)

    def count32(cand):
        cand_b = jnp.broadcast_to(cand, (8, tq))

        def body(j, part):
            off = pl.multiple_of(j * tk, tk)
            kb = key_sc[pl.ds(off, tk), :]
            return part + _tree_sum([jnp.where(kb[c * 8:(c + 1) * 8, :] >= cand_b, 1.0, 0.0)
                                     for c in range(tk // 8)])

        part = count_loop(0, nblk, body, jnp.zeros((8, tq), F32))
        return jnp.sum(part, axis=0, keepdims=True)

    def kth16(ref, want):
        t = jnp.where(count16(ref, jnp.zeros((1, tq), I32)) >= want, 0, I16_MIN).astype(I32)

        def search(i, t):
            cand = t | jnp.left_shift(jnp.int32(1), 14 - i)
            return jnp.where(count16(ref, cand) >= want, cand, t)

        return lax.fori_loop(0, 15, search, t)

    kf = float(n_sel)
    thr_sc[...] = jnp.full(thr_sc.shape, INT_MIN + 1, I32)
    room_sc[...] = jnp.zeros(room_sc.shape, F32)
    tie_sc[...] = jnp.zeros(tie_sc.shape, F32)

    @pl.when(top_limit > n_sel)
    def _():
        t_hi = kth16(hi_sc, kf)
        above = jnp.where(t_hi >= -I16_MIN - 1, 0.0, count16(hi_sc, t_hi + 1))
        t_hi_b = jnp.broadcast_to(t_hi, (16, tq)).astype(I16)

        def keep_equal_hi(j, _):
            off = pl.multiple_of(j * tk, tk)
            hi_blk = hi_sc[pl.ds(off, tk), :]
            lo_blk = lo_sc[pl.ds(off, tk), :]
            kept = [jnp.where(hi_blk[c * 16:(c + 1) * 16, :] == t_hi_b, lo_blk[c * 16:(c + 1) * 16, :],
                              jnp.int16(I16_MIN)) for c in range(tk // 16)]
            lo_sc[pl.ds(off, tk), :] = jnp.concatenate(kept, axis=0)
            return 0

        count_loop(0, nblk, keep_equal_hi, 0)
        t_lo = kth16(lo_sc, kf - above)
        t_full = jnp.maximum(t_hi * 65536 + (t_lo - I16_MIN), INT_MIN + 1)
        cnt_ge = count32(t_full)
        cnt_gt = count32(t_full + 1)
        real_q = qlane < (t_valid - qb * tq)
        thr_sc[...] = jnp.broadcast_to(t_full, thr_sc.shape)
        room_sc[...] = jnp.broadcast_to(kf - cnt_gt, room_sc.shape)
        tie_sc[...] = jnp.broadcast_to(jnp.where(real_q, jnp.where(cnt_ge > kf, 1.0, 0.0), 0.0), tie_sc.shape)

    thr = thr_sc[0:1, :]
    room = room_sc[0:1, :]
    any_tie = jnp.max(tie_sc[0:1, :]) > 0.0

    @pl.when(any_tie)
    def _():
        ri = lax.broadcasted_iota(I32, (tk, tk), 0)
        ci = lax.broadcasted_iota(I32, (tk, tk), 1)
        upto = jnp.where(ci <= ri, 1.0, 0.0).astype(BF16)

        def drop_late_ties(j, seen):
            off = pl.multiple_of(j * tk, tk)
            kb = key_sc[pl.ds(off, tk), :]
            eq = jnp.where(kb == thr, 1.0, 0.0)
            rank = _dot(upto, eq.astype(BF16)) + seen
            key_sc[pl.ds(off, tk), :] = jnp.where(eq * rank > room, INT_MIN, kb)
            return seen + jnp.sum(eq, axis=0, keepdims=True)

        lax.fori_loop(0, nblk, drop_late_ties, jnp.zeros((1, tq), F32))

    qh = _masked_heads_t(qa_ref[0])

    def attend(j, carry):
        off = pl.multiple_of(j * tk, tk)
        kblk = kb_sc[pl.ds(off, tk), :]
        keep = key_sc[pl.ds(off, tk), :] >= thr
        scores = [jnp.where(keep, _dot(kblk, qh[h]), NEG) for h in range(N_HEADS)]
        return _flash_step(carry, scores, vt_sc, off, tk)

    carry = block_loop(0, nblk, attend, _flash_init(tq))
    o_ref[0] = _flash_finish(carry).astype(o_ref.dtype)


def _dsa(qa, qi, qsm, k_srcs, v_srcs, ki_srcs, *, layer, lp, tq, tk, past_len, l_valid, n_sel, t_valid):
    bsz, _, t = qa.shape
    n_src = len(k_srcs)
    srcs = list(k_srcs) + list(v_srcs) + list(ki_srcs)
    return pl.pallas_call(
        functools.partial(_dsa_kernel, n_src=n_src, tq=tq, tk=tk, past_len=past_len, l_valid=l_valid,
                          n_sel=n_sel, t_valid=t_valid, single_q=(t == tq)),
        grid=(bsz, t // tq),
        in_specs=[_q_spec(BRANCH, tq), _q_spec(BRANCH, tq), _q_spec(SLAB, tq)] + [_src_spec(a, layer) for a in srcs],
        out_specs=_o_spec(tq),
        out_shape=jax.ShapeDtypeStruct((bsz, t, BRANCH), BF16),
        scratch_shapes=[pltpu.VMEM((lp, BRANCH), BF16), pltpu.VMEM((BRANCH, lp), BF16),
                        pltpu.VMEM((lp, SLAB), BF16), pltpu.VMEM((IDX_HEADS, SLAB, tq), BF16),
                        pltpu.VMEM((lp, tq), I32), pltpu.VMEM((lp, tq), I16), pltpu.VMEM((lp, tq), I16),
                        pltpu.VMEM((SUB, tq), I32), pltpu.VMEM((SUB, tq), F32), pltpu.VMEM((SUB, tq), F32)],
        compiler_params=_params(2),
        name="dsa",
    )(qa, qi, qsm, *srcs)


def _fox_kernel(q_ref, *refs, n_src, f_row, tq, tk, past_len):
    k_refs, v_refs, f_refs = refs[:n_src], refs[n_src:2 * n_src], refs[2 * n_src:3 * n_src]
    o_ref, vt_sc, kaug_sc, f_sc = refs[3 * n_src:]
    qb = pl.program_id(1)
    lp = f_sc.shape[1]

    @pl.when(qb == 0)
    def _():
        _fill_channel_major(vt_sc, [_load2d(r) for r in v_refs])
        ks = [_load2d(r) for r in k_refs]
        fs = []
        for r, row in zip(f_refs, f_row):
            logf = _load2d(r)
            if row + F_ROWS <= logf.shape[0]:
                fs.append(logf[row:row + F_ROWS, :])
            else:
                gates = logf[row:row + N_HEADS, :]
                fs.append(jnp.concatenate([gates, jnp.zeros((F_ROWS - N_HEADS, gates.shape[1]), F32)], axis=0))
        kt = ks[0] if len(ks) == 1 else jnp.concatenate(ks, axis=1)
        ft = fs[0] if len(fs) == 1 else jnp.concatenate(fs, axis=1)
        n_real = kt.shape[1]
        ri = lax.broadcasted_iota(I32, (F_BLK, F_BLK), 0)
        ci = lax.broadcasted_iota(I32, (F_BLK, F_BLK), 1)
        upto = jnp.where(ri <= ci, 1.0, 0.0).astype(BF16)
        rowx = lax.broadcasted_iota(I32, (HEAD_DIM, F_BLK), 0)
        run = jnp.zeros((F_ROWS, 1), F32)
        for b in range(lp // F_BLK):
            lo, hi_ = b * F_BLK, min((b + 1) * F_BLK, n_real)
            if hi_ <= lo:
                kaug_sc[lo:lo + F_BLK, :] = jnp.zeros((F_BLK, N_HEADS * AUG), BF16)
                f_sc[:, lo:lo + F_BLK] = jnp.broadcast_to(run * LOG2E, (F_ROWS, F_BLK))
                continue
            k_blk, f_blk = kt[:, lo:hi_], ft[:, lo:hi_]
            if hi_ - lo < F_BLK:
                k_blk = jnp.concatenate([k_blk, jnp.zeros((BRANCH, F_BLK - (hi_ - lo)), F32)], axis=1)
                f_blk = jnp.concatenate([f_blk, jnp.zeros((F_ROWS, F_BLK - (hi_ - lo)), F32)], axis=1)
            s1, s2, s3 = _split3(f_blk)
            fsum = _dot(s1, upto) + _dot(s2, upto) + _dot(s3, upto) + run
            run = fsum[:, F_BLK - 1:F_BLK]
            fsum = fsum * LOG2E
            f_sc[:, lo:lo + F_BLK] = fsum
            f1, f2, f3 = (p.astype(F32) for p in _split3(fsum))
            for h in range(N_HEADS):
                extra = jnp.where(rowx < 3, 1.0,
                                  jnp.where(rowx == 3, -f1[h:h + 1, :],
                                            jnp.where(rowx == 4, -f2[h:h + 1, :],
                                                      jnp.where(rowx == 5, -f3[h:h + 1, :], 0.0))))
                aug_t = jnp.concatenate([k_blk[h * HEAD_DIM:(h + 1) * HEAD_DIM, :], extra], axis=0)
                kaug_sc[lo:lo + F_BLK, h * AUG:(h + 1) * AUG] = aug_t.T.astype(BF16)

    row0 = past_len + qb * tq
    pos = row0 + lax.broadcasted_iota(I32, (1, tq), 1)
    n_full = row0 // tk
    n_all = (row0 + tq + tk - 1) // tk
    q_t = q_ref[0].astype(F32)
    f1, f2, f3 = (p.astype(F32) for p in _split3(f_sc[:, pl.ds(pl.multiple_of(row0, Q_PAD), tq)]))
    rowq = lax.broadcasted_iota(I32, (HEAD_DIM, tq), 0)
    qh = []
    for h in range(N_HEADS):
        extra = jnp.where(rowq == 0, f1[h:h + 1, :],
                          jnp.where(rowq == 1, f2[h:h + 1, :],
                                    jnp.where(rowq == 2, f3[h:h + 1, :], jnp.where(rowq < 6, 1.0, 0.0))))
        qh.append(jnp.concatenate([q_t[h * HEAD_DIM:(h + 1) * HEAD_DIM, :], extra], axis=0).astype(BF16))

    def step(j, carry, masked):
        off = pl.multiple_of(j * tk, tk)
        scores = [_dot(kaug_sc[pl.ds(off, tk), h * AUG:(h + 1) * AUG], qh[h]) for h in range(N_HEADS)]
        if masked:
            causal = off + lax.broadcasted_iota(I32, (tk, tq), 0) <= pos
            scores = [jnp.where(causal, s, NEG) for s in scores]
        return _flash_step(carry, scores, vt_sc, off, tk)

    carry = lax.fori_loop(0, n_full, lambda j, c: step(j, c, False), _flash_init(tq))
    carry = lax.fori_loop(n_full, n_all, lambda j, c: step(j, c, True), carry)
    o_ref[0] = _flash_finish(carry).astype(o_ref.dtype)


def _fox(q, k_srcs, v_srcs, f_srcs, f_row, *, layer, lp, tq, tk, past_len):
    bsz, _, t = q.shape
    n_src = len(k_srcs)
    srcs = list(k_srcs) + list(v_srcs) + list(f_srcs)
    return pl.pallas_call(
        functools.partial(_fox_kernel, n_src=n_src, f_row=tuple(f_row), tq=tq, tk=tk, past_len=past_len),
        grid=(bsz, t // tq),
        in_specs=[_q_spec(BRANCH, tq)] + [_src_spec(a, layer) for a in srcs],
        out_specs=_o_spec(tq),
        out_shape=jax.ShapeDtypeStruct((bsz, t, BRANCH), BF16),
        scratch_shapes=[pltpu.VMEM((BRANCH, lp), BF16), pltpu.VMEM((lp, N_HEADS * AUG), BF16),
                        pltpu.VMEM((F_ROWS, lp), F32)],
        compiler_params=_params(2),
        name="fox",
    )(q, *srcs)


def _sb_kernel(q_ref, *refs, n_src, tq, tk, past_len):
    k_refs, v_refs = refs[:n_src], refs[n_src:2 * n_src]
    o_ref, kb_sc, vt_sc = refs[2 * n_src:]
    qb = pl.program_id(1)

    @pl.when(qb == 0)
    def _():
        _fill_token_major(kb_sc, [_load2d(r) for r in k_refs])
        _fill_channel_major(vt_sc, [_load2d(r) for r in v_refs])

    qh = _masked_heads_t(q_ref[0])
    row0 = past_len + qb * tq
    pos = row0 + lax.broadcasted_iota(I32, (1, tq), 1)
    j_top = (row0 + tq - 1) // tk
    ri = lax.broadcasted_iota(I32, (tk, tk), 0)
    ci = lax.broadcasted_iota(I32, (tk, tk), 1)
    after = jnp.where(ci > ri, 1.0, 0.0).astype(BF16)

    def cond(carry):
        j, state = carry
        live = state[0][0]
        for h in range(1, N_HEADS):
            live = jnp.maximum(live, state[h][0])
        return jnp.logical_and(j >= 0, jnp.max(live) > -SB_CUTOFF)

    def body(carry, masked=False):
        j, state = carry
        off = pl.multiple_of(j * tk, tk)
        kblk = kb_sc[pl.ds(off, tk), :]
        zs = [_dot(kblk, qh[h]) for h in range(N_HEADS)]
        keeps = [-(jnp.maximum(z, 0.0) + jnp.log2(1.0 + jnp.exp2(-jnp.abs(z)))) for z in zs]
        if masked:
            strict = off + lax.broadcasted_iota(I32, (tk, tq), 0) < pos
            keeps = [jnp.where(strict, kp, 0.0) for kp in keeps]
        laters = []
        for h in range(N_HEADS):
            k_hi, k_lo = _split2(keeps[h])
            laters.append(_dot(after, k_hi) + _dot(after, k_lo) + state[h][0])
        ws = [jnp.exp2(zs[h] + keeps[h] + laters[h]) for h in range(N_HEADS)]
        if masked:
            ws = [jnp.where(strict, w, 0.0) for w in ws]
        pvs = [_dot(vt_sc[h * HEAD_DIM:(h + 1) * HEAD_DIM, pl.ds(off, tk)], ws[h].astype(BF16))
               for h in range(N_HEADS)]
        new = tuple((state[h][0] + jnp.sum(keeps[h], axis=0, keepdims=True), state[h][1] + pvs[h])
                    for h in range(N_HEADS))
        return j - 1, new

    init = tuple((jnp.zeros((1, tq), F32), jnp.zeros((HEAD_DIM, tq), F32)) for _ in range(N_HEADS))
    first = body((j_top, init), masked=True)
    _, state = lax.while_loop(cond, body, first)
    out_t = jnp.concatenate([acc for (_, acc) in state], axis=0)
    o_ref[0] = out_t.T.astype(o_ref.dtype)


def _sb(q, k_srcs, v_srcs, *, layer, lp, tq, tk, past_len):
    bsz, _, t = q.shape
    assert tk % tq == 0 and past_len % tq == 0
    n_src = len(k_srcs)
    srcs = list(k_srcs) + list(v_srcs)
    return pl.pallas_call(
        functools.partial(_sb_kernel, n_src=n_src, tq=tq, tk=tk, past_len=past_len),
        grid=(bsz, t // tq),
        in_specs=[_q_spec(BRANCH, tq)] + [_src_spec(a, layer) for a in srcs],
        out_specs=_o_spec(tq),
        out_shape=jax.ShapeDtypeStruct((bsz, t, BRANCH), BF16),
        scratch_shapes=[pltpu.VMEM((lp, BRANCH), BF16), pltpu.VMEM((BRANCH, lp), BF16)],
        compiler_params=_params(2),
        name="sb",
    )(q, *srcs)


def _merge_kernel(x_ref, mod_ref, g1_ref, oa_ref, ob_ref, oc_ref, od_ref, wg_ref, bg_ref, wb_ref, wo_ref,
                  xo_ref, *, nb, tt):
    m = nb * tt
    d = x_ref.shape[-1]
    x = x_ref[...]
    mod = mod_ref[...]
    h = _rms(x, g1_ref[...]) * (1.0 + mod[:, 1:2, :]) + mod[:, 0:1, :]
    hb = h.reshape(m, d).astype(BF16)
    o_refs = (oa_ref, ob_ref, oc_ref, od_ref)

    def pre(i):
        return (_dot(hb, wg_ref[:, i * d:(i + 1) * d]), _dot(o_refs[i][...].reshape(m, BRANCH), wb_ref[i]))

    merged = None
    cur = pre(0)
    for i in range(N_BRANCH):
        nxt = pre(i + 1) if i + 1 < N_BRANCH else None
        term = jax.nn.sigmoid(cur[0] + bg_ref[:, i * d:(i + 1) * d]) * cur[1]
        merged = term if merged is None else merged + term
        cur = nxt
    y = _dot(merged.astype(BF16), wo_ref[...])
    xo_ref[...] = x + mod[:, 2:3, :] * y.reshape(nb, tt, d)


def _merge(x, mod, g1, oa, ob, oc, od, w_gate, b_gate, w_branch, w_out, *, nb, tt):
    bsz, t, d = x.shape
    tok = lambda w: pl.BlockSpec((nb, tt, w), lambda bi, ti: (bi, ti, 0))
    const = lambda a: pl.BlockSpec(a.shape, lambda bi, ti: (0,) * a.ndim)
    return pl.pallas_call(
        functools.partial(_merge_kernel, nb=nb, tt=tt),
        grid=(bsz // nb, t // tt),
        in_specs=[tok(d), pl.BlockSpec((nb, 6, d), lambda bi, ti: (bi, 0, 0)), const(g1),
                  tok(BRANCH), tok(BRANCH), tok(BRANCH), tok(BRANCH),
                  const(w_gate), const(b_gate), const(w_branch), const(w_out)],
        out_specs=tok(d),
        out_shape=jax.ShapeDtypeStruct((bsz, t, d), F32),
        compiler_params=_params(2),
        name="merge",
    )(x, mod, g1, oa, ob, oc, od, w_gate, b_gate, w_branch, w_out)


def _ffn_kernel(x_ref, mod_ref, g2_ref, pfx_ref, wu_ref, wc_ref, bc_ref, wd_ref, fg_ref,
                xo_ref, st_ref, carry_sc, ext_sc, *, nb, tt, cw, final_norm):
    m = nb * tt
    d = x_ref.shape[-1]
    dff = wd_ref.shape[0]
    ti = pl.program_id(1)
    keep = CONV_W - 1

    @pl.when(ti == 0)
    def _():
        carry_sc[...] = jnp.zeros_like(carry_sc)
        carry_sc[:, SUB - keep:, :] = pfx_ref[...]

    x = x_ref[...]
    mod = mod_ref[...]
    h = _rms(x, g2_ref[...]) * (1.0 + mod[:, 4:5, :]) + mod[:, 3:4, :]
    hb = h.reshape(m, d).astype(BF16)

    def up_cols(c0):
        return _dot(hb, wu_ref[:, c0:c0 + cw]).reshape(nb, tt, cw)

    def stage(up, c0, slot):
        ext_sc[slot, :, 0:SUB, :] = carry_sc[:, :, c0:c0 + cw]
        ext_sc[slot, :, SUB:, :] = up
        carry_sc[:, :, c0:c0 + cw] = up[:, tt - SUB:, :]

    def conv_rows(c0, slot, r0, rows):
        wc = wc_ref[:, c0:c0 + cw]
        win = lambda back: ext_sc[slot, :, SUB - back + r0:SUB - back + r0 + rows, :]
        conv = win(2) * wc[0:1, :] + win(1) * wc[1:2, :] + win(0) * wc[2:3, :] + bc_ref[:, c0:c0 + cw]
        return conv.reshape(nb * rows, cw)

    n_half = 2 if (nb == 1 and tt % (2 * SUB) == 0) else 1
    rows = tt // n_half
    accs = [jnp.zeros((nb * rows, d), F32) for _ in range(n_half)]
    n_chunks = dff // cw
    ups = (up_cols(0), up_cols(dff))
    acts = None
    for j in range(n_chunks + 1):
        nxt = (up_cols((j + 1) * cw), up_cols(dff + (j + 1) * cw)) if j + 1 < n_chunks else None
        if acts is not None:
            for hh in range(n_half):
                accs[hh] = accs[hh] + _dot(acts[hh], wd_ref[(j - 1) * cw:j * cw, :])
        if j == n_chunks:
            break
        slot = 2 * (j % 2)
        stage(ups[0], j * cw, slot)
        stage(ups[1], dff + j * cw, slot + 1)
        acts = []
        for hh in range(n_half):
            gate = conv_rows(j * cw, slot, hh * rows, rows)
            val = conv_rows(dff + j * cw, slot + 1, hh * rows, rows)
            acts.append((gate * jax.nn.sigmoid(gate) * val).astype(BF16))
        ups = nxt
    acc = accs[0] if n_half == 1 else jnp.concatenate(accs, axis=0)
    xo = x + mod[:, 5:6, :] * acc.reshape(nb, tt, d)
    if final_norm:
        xo = _rms(xo, fg_ref[...])
    xo_ref[...] = xo

    @pl.when(ti == pl.num_programs(1) - 1)
    def _():
        st_ref[...] = carry_sc[:, SUB - keep:, :]


def _ffn(x, mod, g2, prefix, w_up, w_conv, b_conv, w_down, final_g, *, nb, tt, final_norm):
    bsz, t, d = x.shape
    dff = w_down.shape[0]
    cw = 256
    tok = pl.BlockSpec((nb, tt, d), lambda bi, ti: (bi, ti, 0))
    const = lambda a: pl.BlockSpec(a.shape, lambda bi, ti: (0,) * a.ndim)
    state = pl.BlockSpec((nb, CONV_W - 1, 2 * dff), lambda bi, ti: (bi, 0, 0))
    return pl.pallas_call(
        functools.partial(_ffn_kernel, nb=nb, tt=tt, cw=cw, final_norm=final_norm),
        grid=(bsz // nb, t // tt),
        in_specs=[tok, pl.BlockSpec((nb, 6, d), lambda bi, ti: (bi, 0, 0)), const(g2), state,
                  const(w_up), const(w_conv), const(b_conv), const(w_down), const(final_g)],
        out_specs=[tok, state],
        out_shape=[jax.ShapeDtypeStruct((bsz, t, d), F32),
                   jax.ShapeDtypeStruct((bsz, CONV_W - 1, 2 * dff), F32)],
        scratch_shapes=[pltpu.VMEM((nb, SUB, 2 * dff), F32), pltpu.VMEM((4, nb, tt + SUB, cw), F32)],
        compiler_params=_params(2),
        name="ffn",
    )(x, mod, g2, prefix, w_up, w_conv, b_conv, w_down, final_g)


def _rope_tables_t(pos):
    def tab(half, heads, pad):
        inv = ROPE_THETA ** (-jnp.arange(half, dtype=F32) / half)
        ang = inv[:, None] * pos.astype(F32)[None, :]
        cos, sin = jnp.cos(ang), jnp.sin(ang)
        cos_t = jnp.tile(jnp.concatenate([cos, cos], axis=0), (heads, 1))
        sin_t = jnp.tile(jnp.concatenate([-sin, sin], axis=0), (heads, 1))
        if pad:
            cos_t = jnp.concatenate([cos_t, jnp.ones((pad, pos.shape[0]), F32)], axis=0)
            sin_t = jnp.concatenate([sin_t, jnp.zeros((pad, pos.shape[0]), F32)], axis=0)
        return cos_t, sin_t

    cosa, sina = tab(HEAD_DIM // 2, N_HEADS, 0)
    cosi, sini = tab(IDX_DIM // 2, IDX_HEADS, 0)
    coss, sins = tab(IDX_DIM // 2, 1, SLAB - IDX_DIM)
    return cosa, sina, cosi, sini, coss, sins


def _layout_w_in(w_in_t, layer):
    sizes = (BRANCH, BRANCH, BRANCH, IDX_HEADS * IDX_DIM, IDX_DIM, IDX_HEADS, BRANCH, BRANCH,
             BRANCH, BRANCH, BRANCH, N_HEADS, BRANCH, BRANCH, BRANCH)
    offs = [0]
    for s in sizes:
        offs.append(offs[-1] + s)
    piece = lambda i: w_in_t[offs[i]:offs[i + 1], layer, :]
    a_q, a_k, a_v, a_qi, a_ki, a_w, b_u, b_v, c_q, c_k, c_v, c_f, d_q, d_k, d_v = (piece(i) for i in range(15))
    pad = jnp.zeros((SLAB - IDX_DIM - IDX_HEADS - N_HEADS, w_in_t.shape[-1]), w_in_t.dtype)
    return jnp.concatenate([a_q, a_k, a_v, a_qi, c_q, c_k, c_v, d_q, d_k, d_v, b_u, b_v,
                            a_ki, a_w, c_f, pad], axis=0).astype(BF16)


def _block_diag_tril_t(w_s, n, reps):
    tri = jnp.tril(jnp.ones((n, n), dtype=bool))
    w = jnp.where(tri[None], w_s[:, :n, :n], 0)
    eye = jnp.eye(reps, dtype=w.dtype)
    return jnp.einsum("ab,gts->gbsat", eye, w).reshape(w.shape[0], reps * n, reps * n).astype(BF16)


def _round_up(a, b):
    return (a + b - 1) // b * b


def _channel_major(cache):
    depth, bsz, length = cache.shape[:3]
    flat = cache.reshape(depth, bsz, length, -1)
    return jnp.swapaxes(flat, 2, 3)


def _layer(x, mod, lp, past, prefix, final_g, stacked, *, layer, depth, final_norm, past_len):
    bsz, t, d = x.shape
    is_prompt = past is None
    n = min(t, B_CHUNK)
    if is_prompt:
        nb, tt = 1, min(t, 512)
        reps = tt // n
        pos_rows = jnp.arange(t)
    else:
        nb, tt = bsz, t
        reps = (nb * tt) // n
        pos_rows = jnp.tile(past_len + jnp.arange(t), nb)
    m = nb * tt
    tables = _rope_tables_t(pos_rows)
    ws = _block_diag_tril_t(lp["w_s"], n, reps)
    bs = jnp.tile(jnp.repeat(lp["b_s"][:, :n], HEAD_DIM, axis=0), (1, reps))
    fb = jnp.broadcast_to(lp["fb"], (SLAB, m))
    lng = jnp.broadcast_to(lp["lng"], (BRANCH, m))
    lnb = jnp.broadcast_to(lp["lnb"], (BRANCH, m))
    outs = _proj(x, mod, lp["g1"], lp["w_t"], tables, fb, lng, lnb, ws, bs,
                 nb=nb, tt=tt, stacked=stacked if is_prompt else None, layer=layer, depth=depth)
    kv, (qa, qc, qd, qi, sm, ob, vb) = outs[:6], outs[6:]

    l_valid = past_len + t
    tqp = _round_up(t, Q_PAD)
    if is_prompt:
        tk = min(512, t)
        tk_d = min(256, t)
        lpad = t
        tq = min(256, tqp)
        n_sel = min(TOPK_MAX, t // 4)
        ka, va, kc, vc, kd, vd = ([a] for a in kv)
        ki_srcs, f_srcs, f_row = [sm], [sm], (SLAB_F,)
        att_layer = layer
    else:
        tk = tk_d = 256
        tq = tqp
        lpad = _round_up(past_len + tqp, tk)
        n_sel = min(TOPK_MAX, l_valid // 4)

        def tokens(a):
            a = jnp.swapaxes(a.reshape(a.shape[0], bsz, t), 0, 1)
            return jnp.pad(a, ((0, 0), (0, 0), (0, tqp - t)))

        pk_a, pv_a, pki, pk_c, pv_c, plogf, pk_d, pv_d = past
        qa, qc, qd, qi, sm_tok = (tokens(a) for a in (qa, qc, qd, qi, sm))
        new = [tokens(a) for a in kv]
        ka, va, kc, vc, kd, vd = ([p_, n_] for p_, n_ in zip((pk_a, pv_a, pk_c, pv_c, pk_d, pv_d), new))
        ki_srcs, f_srcs, f_row = [pki, sm_tok], [plogf, sm_tok], (0, SLAB_F)
        att_layer = layer
    sm_q = sm if is_prompt else sm_tok

    oa = _dsa(qa, qi, sm_q, ka, va, ki_srcs, layer=att_layer, lp=lpad, tq=tq, tk=tk, past_len=past_len,
              l_valid=l_valid, n_sel=n_sel, t_valid=t)[:, :t]
    oc = _fox(qc, kc, vc, f_srcs, f_row, layer=att_layer, lp=lpad, tq=tq, tk=tk, past_len=past_len)[:, :t]
    od = _sb(qd, kd, vd, layer=att_layer, lp=lpad, tq=tq, tk=tk_d, past_len=past_len)[:, :t]

    mtt = min(t, 512) if is_prompt else tt
    x = _merge(x, mod, lp["g1"], oa, ob, oc, od, lp["w_gate"], lp["b_gate"], lp["w_branch"], lp["w_out"],
               nb=nb, tt=mtt)
    x, conv_state = _ffn(x, mod, lp["g2"], prefix, lp["w_up"], lp["w_conv"], lp["b_conv"], lp["w_down"],
                         final_g, nb=nb, tt=tt, final_norm=final_norm)
    if is_prompt:
        rows = (sm, conv_state)
    else:
        heads = lambda a: jnp.transpose(a.reshape(N_HEADS, HEAD_DIM, bsz, t), (2, 3, 0, 1))
        small = lambda a, r0, r1: jnp.transpose(a[r0:r1].reshape(r1 - r0, bsz, t), (1, 2, 0))
        ka_n, va_n, kc_n, vc_n, kd_n, vd_n = kv
        rows = (heads(ka_n), heads(va_n), small(sm, SLAB_KI, SLAB_KI + IDX_DIM), vb, heads(kc_n), heads(vc_n),
                small(sm, SLAB_F, SLAB_F + N_HEADS), heads(kd_n), heads(vd_n), conv_state)
    return x, rows, kv


def kernel(x_prompt, x_sample, cache_a_k, cache_a_v, cache_a_kidx, cache_c_k, cache_c_v, cache_c_logf, cache_d_k, cache_d_v, state_ffn_conv, c_prompt, c_sample, norm1_g, norm2_g, w_mod, b_mod, w_in, f_bias, lnv_g, lnv_b, w_spatial, b_spatial, w_branch, w_gate, b_gate, w_out, w_up, w_conv, b_conv, w_down, final_g):
    depth = w_in.shape[0]
    bsz, t_prompt, d = x_prompt.shape
    dbsz = x_sample.shape[0]
    past_len = cache_a_k.shape[2]
    mods = _modulation(jnp.concatenate([c_prompt, c_sample], axis=0), w_mod, b_mod)
    mods = mods.reshape(depth, bsz + dbsz, 6, d)
    prefix = jnp.zeros((bsz, CONV_W - 1, w_up.shape[-1]), x_prompt.dtype)
    fg = final_g.reshape(1, d)
    past = tuple(_channel_major(c) for c in (cache_a_k, cache_a_v, cache_a_kidx, cache_c_k, cache_c_v,
                                              cache_c_logf.astype(F32), cache_d_k, cache_d_v))
    w_in_t = jnp.transpose(w_in, (2, 0, 1))
    xp, xs = x_prompt, x_sample
    rows_p, rows_s = [], []
    stacked = ()
    for l in range(depth):
        fb = jnp.zeros((SLAB, 1), F32).at[SLAB_F:SLAB_F + N_HEADS, 0].set(f_bias[l].astype(F32))
        lp = dict(g1=norm1_g[l].reshape(1, d), g2=norm2_g[l].reshape(1, d), w_t=_layout_w_in(w_in_t, l), fb=fb,
                  lng=lnv_g[l].reshape(BRANCH, 1), lnb=lnv_b[l].reshape(BRANCH, 1),
                  w_s=w_spatial[l], b_s=b_spatial[l], w_branch=w_branch[l].astype(BF16),
                  w_gate=w_gate[l].astype(BF16), b_gate=b_gate[l].reshape(1, -1), w_out=w_out[l].astype(BF16),
                  w_up=w_up[l].astype(BF16), w_conv=w_conv[l], b_conv=b_conv[l].reshape(1, -1),
                  w_down=w_down[l].astype(BF16))
        last = l == depth - 1
        xp, new_p, stacked = _layer(xp, mods[l, :bsz], lp, None, prefix, fg, stacked, layer=l, depth=depth,
                                    final_norm=last, past_len=0)
        xs, new_s, _ = _layer(xs, mods[l, bsz:], lp, past, state_ffn_conv[l], fg, None, layer=l, depth=depth,
                              final_norm=last, past_len=past_len)
        rows_p.append(new_p)
        rows_s.append(new_s)

    def stacked_s(i):
        return jnp.stack([r[i] for r in rows_s], axis=0)

    def heads_p(a):
        return jnp.transpose(a.reshape(depth, bsz, N_HEADS, HEAD_DIM, t_prompt), (0, 1, 4, 2, 3))

    sm_p = jnp.stack([r[0] for r in rows_p], axis=0)
    small_p = lambda r0, r1: jnp.swapaxes(sm_p[:, :, r0:r1, :], 2, 3)
    ka_p, va_p, kc_p, vc_p, kd_p, vd_p = (heads_p(a) for a in stacked)
    conv_p = jnp.stack([r[1] for r in rows_p], axis=0)
    return (xp, xs,
            ka_p, stacked_s(0),
            va_p, stacked_s(1),
            small_p(SLAB_KI, SLAB_KI + IDX_DIM), stacked_s(2),
            stacked_s(3),
            kc_p, stacked_s(4),
            vc_p, stacked_s(5),
            small_p(SLAB_F, SLAB_F + N_HEADS), stacked_s(6),
            kd_p, stacked_s(7),
            vd_p, stacked_s(8),
            conv_p, stacked_s(9))
```

```python
import functools

import jax
import jax.numpy as jnp
from jax import lax
from jax.experimental import pallas as pl
from jax.experimental.pallas import tpu as pltpu

F32 = jnp.float32
BF16 = jnp.bfloat16
I32 = jnp.int32
I16 = jnp.int16

HEAD_DIM = 64
N_HEADS = 4
BRANCH = N_HEADS * HEAD_DIM
IDX_HEADS = 8
IDX_DIM = 32
CHUNK = 64
TOPK_MAX = 256
B_CHUNK = 128
N_BRANCH = 4
CONV_W = 3
ROPE_THETA = 10000.0
EPS = 1e-6
LOG2E = 1.4426950408889634
Q_SCALE = HEAD_DIM ** -0.5 * LOG2E

SLAB = 128
SLAB_KI = 0
SLAB_W = 32
SLAB_F = 40
AUG = 128
F_ROWS = 8
F_BLK = 256
N_T_GROUPS = 10
NEG = -1e30
INT_MIN = -2 ** 31
I16_MIN = -2 ** 15
SB_CUTOFF = 220.0
SB_SATURATE = 100.0
Q_PAD = 128
SUB = 8
VMEM_LIMIT = 56 * 1024 * 1024


def _params(n_axes, vmem=VMEM_LIMIT):
    return pltpu.CompilerParams(dimension_semantics=("arbitrary",) * n_axes, vmem_limit_bytes=vmem)


def _dot(a, b):
    return jnp.dot(a, b, preferred_element_type=F32)


def _split2(x):
    hi = x.astype(BF16)
    lo = (x - hi.astype(F32)).astype(BF16)
    return hi, lo


def _split3(x):
    x1 = x.astype(BF16)
    r = x - x1.astype(F32)
    x2 = r.astype(BF16)
    x3 = (r - x2.astype(F32)).astype(BF16)
    return x1, x2, x3


def _softplus(z):
    return jnp.maximum(z, 0.0) + jnp.log1p(jnp.exp(-jnp.abs(z)))


def _rms(x, g):
    ms = jnp.mean(x * x, axis=-1, keepdims=True)
    return x * lax.rsqrt(ms + EPS) * g


def _rope_t(x, cos, sin_signed, half):
    rows = x.shape[0]
    row = lax.broadcasted_iota(I32, x.shape, 0)
    fwd = pltpu.roll(x, half, axis=0)
    bwd = pltpu.roll(x, rows - half, axis=0)
    rot = jnp.where((row % (2 * half)) < half, bwd, fwd)
    return x * cos + rot * sin_signed


def _mod_kernel(c_ref, w_ref, b_ref, o_ref):
    c = c_ref[...]
    a = c * jax.nn.sigmoid(c)
    a1, a2 = _split2(a)
    w1, w2 = _split2(w_ref[0])
    o_ref[0] = _dot(a1, w1) + _dot(a1, w2) + _dot(a2, w1) + b_ref[0]


def _modulation(c_all, w_mod, b_mod):
    depth, d, n = w_mod.shape
    rows = c_all.shape[0]
    tn = 1536
    return pl.pallas_call(
        _mod_kernel,
        grid=(depth, n // tn),
        in_specs=[pl.BlockSpec((rows, d), lambda l, j: (0, 0)),
                  pl.BlockSpec((1, d, tn), lambda l, j: (l, 0, j)),
                  pl.BlockSpec((1, 1, tn), lambda l, j: (l, 0, j))],
        out_specs=pl.BlockSpec((1, rows, tn), lambda l, j: (l, 0, j)),
        out_shape=jax.ShapeDtypeStruct((depth, rows, n), F32),
        compiler_params=_params(2),
        name="modulation",
    )(c_all, w_mod, b_mod.reshape(depth, 1, n))


def _proj_kernel(x_ref, mod_ref, g1_ref, wt_ref, cosa_ref, sina_ref, cosi_ref, sini_ref, coss_ref,
                 sins_ref, fb_ref, lng_ref, lnb_ref, ws_ref, bs_ref, *rest, nb, tt, n_alias):
    (ka_ref, va_ref, kc_ref, vc_ref, kd_ref, vd_ref,
     qa_ref, qc_ref, qd_ref, qi_ref, sm_ref, ob_ref, vb_ref) = rest[n_alias:]
    m = nb * tt
    d = x_ref.shape[-1]
    x = x_ref[...]
    mod = mod_ref[...]
    h = (_rms(x, g1_ref[...]) * (1.0 + mod[:, 1:2, :]) + mod[:, 0:1, :]).reshape(m, d)
    ht = h.T.astype(BF16)

    def col_t(j):
        return _dot(wt_ref[j * BRANCH:(j + 1) * BRANCH, :], ht)

    def put_t(ref, val):
        ref[...] = val.reshape(ref.shape).astype(ref.dtype)

    def put(ref, val):
        ref[...] = val.reshape(nb, tt, val.shape[-1]).astype(ref.dtype)

    cosa, sina = cosa_ref[...], sina_ref[...]
    put_t(qa_ref, _rope_t(col_t(0), cosa, sina, HEAD_DIM // 2) * Q_SCALE)
    put_t(ka_ref, _rope_t(col_t(1), cosa, sina, HEAD_DIM // 2))
    put_t(va_ref, col_t(2))
    put_t(qi_ref, _rope_t(col_t(3), cosi_ref[...], sini_ref[...], IDX_DIM // 2))
    put_t(qc_ref, col_t(4) * Q_SCALE)
    put_t(kc_ref, col_t(5))
    put_t(vc_ref, col_t(6))
    put_t(qd_ref, col_t(7) * Q_SCALE)
    put_t(kd_ref, col_t(8))
    put_t(vd_ref, col_t(9))
    sm = _dot(wt_ref[(N_T_GROUPS + 2) * BRANCH:(N_T_GROUPS + 2) * BRANCH + SLAB, :], ht)
    row = lax.broadcasted_iota(I32, sm.shape, 0)
    roped = _rope_t(sm, coss_ref[...], sins_ref[...], IDX_DIM // 2)
    logf = -_softplus(-(sm + fb_ref[...]))
    is_f = jnp.where(row >= SLAB_F, jnp.where(row < SLAB_F + N_HEADS, 1, 0), 0)
    put_t(sm_ref, jnp.where(is_f == 1, logf, roped))
    u = jax.nn.gelu(col_t(N_T_GROUPS))
    vg = jax.nn.gelu(col_t(N_T_GROUPS + 1))
    mu = jnp.mean(vg, axis=0, keepdims=True)
    vc_ = vg - mu
    var = jnp.mean(vc_ * vc_, axis=0, keepdims=True)
    vb = vc_ * lax.rsqrt(var + EPS) * lng_ref[...] + lnb_ref[...]
    put(vb_ref, vb.T)
    vbb = vb.astype(BF16)
    mixed = jnp.concatenate([_dot(vbb[g * HEAD_DIM:(g + 1) * HEAD_DIM, :], ws_ref[g]) for g in range(N_HEADS)],
                            axis=0)
    put(ob_ref, (u * (mixed + bs_ref[...])).T)


def _proj(x, mod, g1, w_t, tables, fb, lng, lnb, ws, bs, *, nb, tt, stacked, layer, depth):
    bsz, t, d = x.shape
    m = nb * tt
    n_t = t // tt
    tok = lambda w: pl.BlockSpec((nb, tt, w), lambda ti, bi: (bi, ti, 0))
    const = lambda a: pl.BlockSpec(a.shape, lambda ti, bi: (0,) * a.ndim)
    tab = lambda c: pl.BlockSpec((c, m), lambda ti, bi: (0, ti))
    if stacked is None:
        assert n_t == 1 and nb == bsz
        kv_spec = pl.BlockSpec((BRANCH, m), lambda ti, bi: (0, 0))
        kv_shape = jax.ShapeDtypeStruct((BRANCH, m), F32)
        q_spec = lambda c: pl.BlockSpec((c, m), lambda ti, bi: (0, 0))
        q_shape = lambda c, dt: jax.ShapeDtypeStruct((c, m), dt)
        alias_in = ()
    else:
        assert nb == 1
        kv_spec = pl.BlockSpec((1, 1, BRANCH, tt), lambda ti, bi: (layer, bi, 0, ti))
        kv_shape = jax.ShapeDtypeStruct((depth, bsz, BRANCH, t), F32)
        q_spec = lambda c: pl.BlockSpec((1, c, tt), lambda ti, bi: (bi, 0, ti))
        q_shape = lambda c, dt: jax.ShapeDtypeStruct((bsz, c, t), dt)
        alias_in = tuple(stacked)
    n_alias = len(alias_in)
    n_in = 15
    cosa, sina, cosi, sini, coss, sins = tables
    return pl.pallas_call(
        functools.partial(_proj_kernel, nb=nb, tt=tt, n_alias=n_alias),
        grid=(n_t, bsz // nb),
        in_specs=[tok(d), pl.BlockSpec((nb, 6, d), lambda ti, bi: (bi, 0, 0)), const(g1), const(w_t),
                  tab(BRANCH), tab(BRANCH), tab(BRANCH), tab(BRANCH), tab(SLAB), tab(SLAB),
                  const(fb), const(lng), const(lnb), const(ws), const(bs)]
                 + [pl.BlockSpec(memory_space=pl.ANY)] * n_alias,
        out_specs=[kv_spec] * 6 + [q_spec(BRANCH)] * 4 + [q_spec(SLAB), tok(BRANCH), tok(BRANCH)],
        out_shape=[kv_shape] * 6 + [q_shape(BRANCH, BF16)] * 3 + [q_shape(BRANCH, F32), q_shape(SLAB, F32),
                                                                 jax.ShapeDtypeStruct((bsz, t, BRANCH), BF16),
                                                                 jax.ShapeDtypeStruct((bsz, t, BRANCH), F32)],
        input_output_aliases={n_in + i: i for i in range(n_alias)},
        compiler_params=_params(2),
        name="proj",
    )(x, mod, g1, w_t, cosa, sina, cosi, sini, coss, sins, fb, lng, lnb, ws, bs, *alias_in)


def _load2d(ref):
    return ref[(0,) * (len(ref.shape) - 2)]


def _fill_token_major(dst_sc, srcs):
    off = 0
    for s in srcs:
        n = s.shape[1]
        dst_sc[off:off + n, :] = s.T.astype(BF16)
        off += n
    lp = dst_sc.shape[0]
    if off < lp:
        dst_sc[off:lp, :] = jnp.zeros((lp - off, dst_sc.shape[1]), BF16)


def _fill_channel_major(dst_sc, srcs):
    off = 0
    for s in srcs:
        n = s.shape[1]
        dst_sc[:, off:off + n] = s.astype(BF16)
        off += n
    lp = dst_sc.shape[1]
    if off < lp:
        dst_sc[:, off:lp] = jnp.zeros((dst_sc.shape[0], lp - off), BF16)


def _flash_init(tq):
    return tuple((jnp.full((1, tq), NEG, F32), jnp.zeros((1, tq), F32), jnp.zeros((HEAD_DIM, tq), F32))
                 for _ in range(N_HEADS))


def _flash_step(carry, score_fns, vt_sc, off, tk):
    scores = [fn() for fn in score_fns]
    ms = [jnp.maximum(carry[h][0], jnp.max(scores[h], axis=0, keepdims=True)) for h in range(N_HEADS)]
    ps = [jnp.exp2(scores[h] - ms[h]) for h in range(N_HEADS)]
    pvs = [_dot(vt_sc[h * HEAD_DIM:(h + 1) * HEAD_DIM, pl.ds(off, tk)], ps[h].astype(BF16))
           for h in range(N_HEADS)]
    new = []
    for h in range(N_HEADS):
        m_run, l_run, acc = carry[h]
        alpha = jnp.exp2(m_run - ms[h])
        new.append((ms[h], alpha * l_run + jnp.sum(ps[h], axis=0, keepdims=True), alpha * acc + pvs[h]))
    return tuple(new)


def _flash_finish(carry):
    out_t = jnp.concatenate([acc / l_run for (_, l_run, acc) in carry], axis=0)
    return out_t.T


def _tree_sum(terms):
    while len(terms) > 1:
        terms = [a + b for a, b in zip(terms[::2], terms[1::2])] + ([terms[-1]] if len(terms) % 2 else [])
    return terms[0]


def _masked_heads_t(q_t):
    head = lax.broadcasted_iota(I32, q_t.shape, 0) // HEAD_DIM
    return [jnp.where(head == h, q_t, jnp.zeros_like(q_t)) for h in range(N_HEADS)]


def _src_spec(arr, layer):
    if arr.ndim == 4:
        return pl.BlockSpec((1, 1) + arr.shape[2:], lambda b, i: (layer, b, 0, 0))
    return pl.BlockSpec((1,) + arr.shape[1:], lambda b, i: (b, 0, 0))


def _q_spec(c, tq):
    return pl.BlockSpec((1, c, tq), lambda b, i: (b, 0, i))


def _o_spec(tq):
    return pl.BlockSpec((1, tq, BRANCH), lambda b, i: (b, i, 0))


def _dsa_kernel(qa_ref, qi_ref, qsm_ref, *refs, n_src, tq, tk, past_len, l_valid, n_sel, t_valid, single_q):
    k_refs, v_refs, ki_refs = refs[:n_src], refs[n_src:2 * n_src], refs[2 * n_src:3 * n_src]
    o_ref, kb_sc, vt_sc, kic_sc, lhs_sc, key_sc, hi_sc, lo_sc, thr_sc, room_sc, tie_sc = refs[3 * n_src:]
    qb = pl.program_id(1)

    @pl.when(qb == 0)
    def _():
        _fill_token_major(kb_sc, [_load2d(r) for r in k_refs])
        _fill_channel_major(vt_sc, [_load2d(r) for r in v_refs])
        cats = []
        for r in ki_refs:
            ki = _load2d(r)[SLAB_KI:SLAB_KI + IDX_DIM, :]
            hi = ki.astype(BF16).astype(F32)
            cats.append(jnp.concatenate([hi, ki - hi, hi, jnp.zeros_like(hi)], axis=0))
        _fill_token_major(kic_sc, cats)

    qi = qi_ref[0]
    for h in range(IDX_HEADS):
        piece = qi[h * IDX_DIM:(h + 1) * IDX_DIM, :]
        hi = piece.astype(BF16).astype(F32)
        lhs_sc[h] = jnp.concatenate([hi, hi, piece - hi, jnp.zeros_like(hi)], axis=0).astype(BF16)
    w_t = qsm_ref[0]

    qlane = lax.broadcasted_iota(I32, (1, tq), 1)
    if single_q:
        row0 = past_len
        top_limit = min(((row0 + tq - 1) // CHUNK + 1) * CHUNK, l_valid)
        count_loop = functools.partial(lax.fori_loop, unroll=True)
        block_loop = functools.partial(lax.fori_loop, unroll=2)
    else:
        row0 = past_len + qb * tq
        top_limit = jnp.minimum(((row0 + tq - 1) // CHUNK + 1) * CHUNK, l_valid)
        count_loop = block_loop = lax.fori_loop
    pos = row0 + qlane
    limit = jnp.minimum((pos // CHUNK + 1) * CHUNK, l_valid)
    nblk = (top_limit + tk - 1) // tk

    def score_blk(j, _):
        off = pl.multiple_of(j * tk, tk)
        kc = kic_sc[pl.ds(off, tk), :]
        dots = [_dot(kc, lhs_sc[h]) for h in range(IDX_HEADS)]
        acc = jnp.zeros((tk, tq), F32)
        for h in range(IDX_HEADS):
            acc = acc + w_t[SLAB_W + h:SLAB_W + h + 1, :] * jnp.maximum(dots[h], 0.0)
        acc = jnp.where(acc == 0.0, 0.0, acc)
        bits = pltpu.bitcast(acc, I32)
        key = jnp.where(bits < 0, bits ^ 0x7FFFFFFF, bits)
        kidx = off + lax.broadcasted_iota(I32, (tk, tq), 0)
        key = jnp.where(kidx < limit, key, INT_MIN)
        key_sc[pl.ds(off, tk), :] = key
        hi_sc[pl.ds(off, tk), :] = (key >> 16).astype(I16)
        lo_sc[pl.ds(off, tk), :] = ((key & 0xFFFF) + I16_MIN).astype(I16)
        return 0

    block_loop(0, nblk, score_blk, 0)

    def count16(ref, cand):
        cand_b = jnp.broadcast_to(cand, (16, tq)).astype(I16)

        def body(j, part):
            off = pl.multiple_of(j * tk, tk)
            kb = ref[pl.ds(off, tk), :]
            return part + _tree_sum([jnp.where(kb[c * 16:(c + 1) * 16, :] >= cand_b, jnp.int16(1), jnp.int16(0))
                                     for c in range(tk // 16)])

        part = count_loop(0, nblk, body, jnp.zeros((16, tq), I16))
        return jnp.sum(part.astype(F32), axis=0, keepdims=True)

    def count32(cand):
        cand_b = jnp.broadcast_to(cand, (8, tq))

        def body(j, part):
            off = pl.multiple_of(j * tk, tk)
            kb = key_sc[pl.ds(off, tk), :]
            return part + _tree_sum([jnp.where(kb[c * 8:(c + 1) * 8, :] >= cand_b, 1.0, 0.0)
                                     for c in range(tk // 8)])

        part = count_loop(0, nblk, body, jnp.zeros((8, tq), F32))
        return jnp.sum(part, axis=0, keepdims=True)

    def kth16(ref, want):
        t = jnp.where(count16(ref, jnp.zeros((1, tq), I32)) >= want, 0, I16_MIN).astype(I32)

        def search(i, t):
            cand = t | jnp.left_shift(jnp.int32(1), 14 - i)
            return jnp.where(count16(ref, cand) >= want, cand, t)

        return lax.fori_loop(0, 15, search, t)

    kf = float(n_sel)
    thr_sc[...] = jnp.full(thr_sc.shape, INT_MIN + 1, I32)
    room_sc[...] = jnp.zeros(room_sc.shape, F32)
    tie_sc[...] = jnp.zeros(tie_sc.shape, F32)

    @pl.when(top_limit > n_sel)
    def _():
        t_hi = kth16(hi_sc, kf)
        above = jnp.where(t_hi >= -I16_MIN - 1, 0.0, count16(hi_sc, t_hi + 1))
        t_hi_b = jnp.broadcast_to(t_hi, (16, tq)).astype(I16)

        def keep_equal_hi(j, _):
            off = pl.multiple_of(j * tk, tk)
            hi_blk = hi_sc[pl.ds(off, tk), :]
            lo_blk = lo_sc[pl.ds(off, tk), :]
            kept = [jnp.where(hi_blk[c * 16:(c + 1) * 16, :] == t_hi_b, lo_blk[c * 16:(c + 1) * 16, :],
                              jnp.int16(I16_MIN)) for c in range(tk // 16)]
            lo_sc[pl.ds(off, tk), :] = jnp.concatenate(kept, axis=0)
            return 0

        count_loop(0, nblk, keep_equal_hi, 0)
        t_lo = kth16(lo_sc, kf - above)
        t_full = jnp.maximum(t_hi * 65536 + (t_lo - I16_MIN), INT_MIN + 1)
        cnt_ge = count32(t_full)
        cnt_gt = count32(t_full + 1)
        real_q = qlane < (t_valid - qb * tq)
        thr_sc[...] = jnp.broadcast_to(t_full, thr_sc.shape)
        room_sc[...] = jnp.broadcast_to(kf - cnt_gt, room_sc.shape)
        tie_sc[...] = jnp.broadcast_to(jnp.where(real_q, jnp.where(cnt_ge > kf, 1.0, 0.0), 0.0), tie_sc.shape)

    thr = thr_sc[0:1, :]
    room = room_sc[0:1, :]
    any_tie = jnp.max(tie_sc[0:1, :]) > 0.0

    @pl.when(any_tie)
    def _():
        ri = lax.broadcasted_iota(I32, (tk, tk), 0)
        ci = lax.broadcasted_iota(I32, (tk, tk), 1)
        upto = jnp.where(ci <= ri, 1.0, 0.0).astype(BF16)

        def drop_late_ties(j, seen):
            off = pl.multiple_of(j * tk, tk)
            kb = key_sc[pl.ds(off, tk), :]
            eq = jnp.where(kb == thr, 1.0, 0.0)
            rank = _dot(upto, eq.astype(BF16)) + seen
            key_sc[pl.ds(off, tk), :] = jnp.where(eq * rank > room, INT_MIN, kb)
            return seen + jnp.sum(eq, axis=0, keepdims=True)

        lax.fori_loop(0, nblk, drop_late_ties, jnp.zeros((1, tq), F32))

    qh = _masked_heads_t(qa_ref[0])

    def attend(j, carry):
        off = pl.multiple_of(j * tk, tk)
        kblk = kb_sc[pl.ds(off, tk), :]
        keep = key_sc[pl.ds(off, tk), :] >= thr
        score_fns = [functools.partial(lambda h: jnp.where(keep, _dot(kblk, qh[h]), NEG), h) for h in range(N_HEADS)]
        return _flash_step(carry, score_fns, vt_sc, off, tk)

    carry = block_loop(0, nblk, attend, _flash_init(tq))
    o_ref[0] = _flash_finish(carry).astype(o_ref.dtype)


def _dsa(qa, qi, qsm, k_srcs, v_srcs, ki_srcs, *, layer, lp, tq, tk, past_len, l_valid, n_sel, t_valid):
    bsz, _, t = qa.shape
    n_src = len(k_srcs)
    srcs = list(k_srcs) + list(v_srcs) + list(ki_srcs)
    return pl.pallas_call(
        functools.partial(_dsa_kernel, n_src=n_src, tq=tq, tk=tk, past_len=past_len, l_valid=l_valid,
                          n_sel=n_sel, t_valid=t_valid, single_q=(t == tq)),
        grid=(bsz, t // tq),
        in_specs=[_q_spec(BRANCH, tq), _q_spec(BRANCH, tq), _q_spec(SLAB, tq)] + [_src_spec(a, layer) for a in srcs],
        out_specs=_o_spec(tq),
        out_shape=jax.ShapeDtypeStruct((bsz, t, BRANCH), BF16),
        scratch_shapes=[pltpu.VMEM((lp, BRANCH), BF16), pltpu.VMEM((BRANCH, lp), BF16),
                        pltpu.VMEM((lp, SLAB), BF16), pltpu.VMEM((IDX_HEADS, SLAB, tq), BF16),
                        pltpu.VMEM((lp, tq), I32), pltpu.VMEM((lp, tq), I16), pltpu.VMEM((lp, tq), I16),
                        pltpu.VMEM((SUB, tq), I32), pltpu.VMEM((SUB, tq), F32), pltpu.VMEM((SUB, tq), F32)],
        compiler_params=_params(2),
        name="dsa",
    )(qa, qi, qsm, *srcs)


def _fox_kernel(q_ref, *refs, n_src, f_row, tq, tk, past_len):
    k_refs, v_refs, f_refs = refs[:n_src], refs[n_src:2 * n_src], refs[2 * n_src:3 * n_src]
    o_ref, vt_sc, kaug_sc, f_sc = refs[3 * n_src:]
    qb = pl.program_id(1)
    lp = f_sc.shape[1]

    @pl.when(qb == 0)
    def _():
        _fill_channel_major(vt_sc, [_load2d(r) for r in v_refs])
        ks = [_load2d(r) for r in k_refs]
        fs = []
        for r, row in zip(f_refs, f_row):
            logf = _load2d(r)
            if row + F_ROWS <= logf.shape[0]:
                fs.append(logf[row:row + F_ROWS, :])
            else:
                gates = logf[row:row + N_HEADS, :]
                fs.append(jnp.concatenate([gates, jnp.zeros((F_ROWS - N_HEADS, gates.shape[1]), F32)], axis=0))
        kt = ks[0] if len(ks) == 1 else jnp.concatenate(ks, axis=1)
        ft = fs[0] if len(fs) == 1 else jnp.concatenate(fs, axis=1)
        n_real = kt.shape[1]
        ri = lax.broadcasted_iota(I32, (F_BLK, F_BLK), 0)
        ci = lax.broadcasted_iota(I32, (F_BLK, F_BLK), 1)
        upto = jnp.where(ri <= ci, 1.0, 0.0).astype(BF16)
        rowx = lax.broadcasted_iota(I32, (HEAD_DIM, F_BLK), 0)
        run = jnp.zeros((F_ROWS, 1), F32)
        for b in range(lp // F_BLK):
            lo, hi_ = b * F_BLK, min((b + 1) * F_BLK, n_real)
            if hi_ <= lo:
                kaug_sc[lo:lo + F_BLK, :] = jnp.zeros((F_BLK, N_HEADS * AUG), BF16)
                f_sc[:, lo:lo + F_BLK] = jnp.broadcast_to(run * LOG2E, (F_ROWS, F_BLK))
                continue
            k_blk, f_blk = kt[:, lo:hi_], ft[:, lo:hi_]
            if hi_ - lo < F_BLK:
                k_blk = jnp.concatenate([k_blk, jnp.zeros((BRANCH, F_BLK - (hi_ - lo)), F32)], axis=1)
                f_blk = jnp.concatenate([f_blk, jnp.zeros((F_ROWS, F_BLK - (hi_ - lo)), F32)], axis=1)
            s1, s2, s3 = _split3(f_blk)
            fsum = _dot(s1, upto) + _dot(s2, upto) + _dot(s3, upto) + run
            run = fsum[:, F_BLK - 1:F_BLK]
            fsum = fsum * LOG2E
            f_sc[:, lo:lo + F_BLK] = fsum
            f1, f2, f3 = (p.astype(F32) for p in _split3(fsum))
            for h in range(N_HEADS):
                extra = jnp.where(rowx < 3, 1.0,
                                  jnp.where(rowx == 3, -f1[h:h + 1, :],
                                            jnp.where(rowx == 4, -f2[h:h + 1, :],
                                                      jnp.where(rowx == 5, -f3[h:h + 1, :], 0.0))))
                aug_t = jnp.concatenate([k_blk[h * HEAD_DIM:(h + 1) * HEAD_DIM, :], extra], axis=0)
                kaug_sc[lo:lo + F_BLK, h * AUG:(h + 1) * AUG] = aug_t.T.astype(BF16)

    row0 = past_len + qb * tq
    pos = row0 + lax.broadcasted_iota(I32, (1, tq), 1)
    n_full = row0 // tk
    n_all = (row0 + tq + tk - 1) // tk
    q_t = q_ref[0].astype(F32)
    f1, f2, f3 = (p.astype(F32) for p in _split3(f_sc[:, pl.ds(pl.multiple_of(row0, Q_PAD), tq)]))
    rowq = lax.broadcasted_iota(I32, (HEAD_DIM, tq), 0)
    qh = []
    for h in range(N_HEADS):
        extra = jnp.where(rowq == 0, f1[h:h + 1, :],
                          jnp.where(rowq == 1, f2[h:h + 1, :],
                                    jnp.where(rowq == 2, f3[h:h + 1, :], jnp.where(rowq < 6, 1.0, 0.0))))
        qh.append(jnp.concatenate([q_t[h * HEAD_DIM:(h + 1) * HEAD_DIM, :], extra], axis=0).astype(BF16))

    def step(j, carry, masked):
        off = pl.multiple_of(j * tk, tk)
        causal = (off + lax.broadcasted_iota(I32, (tk, tq), 0) <= pos) if masked else None

        def score(h):
            s = _dot(kaug_sc[pl.ds(off, tk), h * AUG:(h + 1) * AUG], qh[h])
            return jnp.where(causal, s, NEG) if masked else s

        return _flash_step(carry, [functools.partial(score, h) for h in range(N_HEADS)], vt_sc, off, tk)

    carry = lax.fori_loop(0, n_full, lambda j, c: step(j, c, False), _flash_init(tq))
    carry = lax.fori_loop(n_full, n_all, lambda j, c: step(j, c, True), carry)
    o_ref[0] = _flash_finish(carry).astype(o_ref.dtype)


def _fox(q, k_srcs, v_srcs, f_srcs, f_row, *, layer, lp, tq, tk, past_len):
    bsz, _, t = q.shape
    n_src = len(k_srcs)
    srcs = list(k_srcs) + list(v_srcs) + list(f_srcs)
    return pl.pallas_call(
        functools.partial(_fox_kernel, n_src=n_src, f_row=tuple(f_row), tq=tq, tk=tk, past_len=past_len),
        grid=(bsz, t // tq),
        in_specs=[_q_spec(BRANCH, tq)] + [_src_spec(a, layer) for a in srcs],
        out_specs=_o_spec(tq),
        out_shape=jax.ShapeDtypeStruct((bsz, t, BRANCH), BF16),
        scratch_shapes=[pltpu.VMEM((BRANCH, lp), BF16), pltpu.VMEM((lp, N_HEADS * AUG), BF16),
                        pltpu.VMEM((F_ROWS, lp), F32)],
        compiler_params=_params(2),
        name="fox",
    )(q, *srcs)


def _sb_kernel(q_ref, *refs, n_src, tq, tk, past_len):
    k_refs, v_refs = refs[:n_src], refs[n_src:2 * n_src]
    o_ref, kb_sc, vt_sc = refs[2 * n_src:]
    qb = pl.program_id(1)

    @pl.when(qb == 0)
    def _():
        _fill_token_major(kb_sc, [_load2d(r) for r in k_refs])
        _fill_channel_major(vt_sc, [_load2d(r) for r in v_refs])

    qh = _masked_heads_t(q_ref[0])
    row0 = past_len + qb * tq
    pos = row0 + lax.broadcasted_iota(I32, (1, tq), 1)
    j_top = (row0 + tq - 1) // tk
    ri = lax.broadcasted_iota(I32, (tk, tk), 0)
    ci = lax.broadcasted_iota(I32, (tk, tk), 1)
    after = jnp.where(ci > ri, 1.0, 0.0).astype(BF16)

    def cond(carry):
        j, state = carry
        live = state[0][0]
        for h in range(1, N_HEADS):
            live = jnp.maximum(live, state[h][0])
        return jnp.logical_and(j >= 0, jnp.max(live) > -SB_CUTOFF)

    def body(carry, masked=False):
        j, state = carry
        off = pl.multiple_of(j * tk, tk)
        kblk = kb_sc[pl.ds(off, tk), :]
        zs = [_dot(kblk, qh[h]) for h in range(N_HEADS)]
        keeps = [-jnp.maximum(jnp.log2(1.0 + jnp.exp2(jnp.minimum(z, SB_SATURATE))), z) for z in zs]
        if masked:
            strict = off + lax.broadcasted_iota(I32, (tk, tq), 0) < pos
            keeps = [jnp.where(strict, kp, 0.0) for kp in keeps]
        laters = []
        for h in range(N_HEADS):
            k_hi, k_lo = _split2(keeps[h])
            laters.append(_dot(after, k_hi) + _dot(after, k_lo) + state[h][0])
        ws = [jnp.exp2(zs[h] + keeps[h] + laters[h]) for h in range(N_HEADS)]
        if masked:
            ws = [jnp.where(strict, w, 0.0) for w in ws]
        pvs = [_dot(vt_sc[h * HEAD_DIM:(h + 1) * HEAD_DIM, pl.ds(off, tk)], ws[h].astype(BF16))
               for h in range(N_HEADS)]
        new = tuple((state[h][0] + jnp.sum(keeps[h], axis=0, keepdims=True), state[h][1] + pvs[h])
                    for h in range(N_HEADS))
        return j - 1, new

    init = tuple((jnp.zeros((1, tq), F32), jnp.zeros((HEAD_DIM, tq), F32)) for _ in range(N_HEADS))
    first = body((j_top, init), masked=True)
    _, state = lax.while_loop(cond, body, first)
    out_t = jnp.concatenate([acc for (_, acc) in state], axis=0)
    o_ref[0] = out_t.T.astype(o_ref.dtype)


def _sb(q, k_srcs, v_srcs, *, layer, lp, tq, tk, past_len):
    bsz, _, t = q.shape
    assert tk % tq == 0 and past_len % tq == 0
    n_src = len(k_srcs)
    srcs = list(k_srcs) + list(v_srcs)
    return pl.pallas_call(
        functools.partial(_sb_kernel, n_src=n_src, tq=tq, tk=tk, past_len=past_len),
        grid=(bsz, t // tq),
        in_specs=[_q_spec(BRANCH, tq)] + [_src_spec(a, layer) for a in srcs],
        out_specs=_o_spec(tq),
        out_shape=jax.ShapeDtypeStruct((bsz, t, BRANCH), BF16),
        scratch_shapes=[pltpu.VMEM((lp, BRANCH), BF16), pltpu.VMEM((BRANCH, lp), BF16)],
        compiler_params=_params(2),
        name="sb",
    )(q, *srcs)


def _merge_kernel(x_ref, mod_ref, g1_ref, oa_ref, ob_ref, oc_ref, od_ref, wg_ref, bg_ref, wb_ref, wo_ref,
                  xo_ref, *, nb, tt):
    m = nb * tt
    d = x_ref.shape[-1]
    x = x_ref[...]
    mod = mod_ref[...]
    h = _rms(x, g1_ref[...]) * (1.0 + mod[:, 1:2, :]) + mod[:, 0:1, :]
    hb = h.reshape(m, d).astype(BF16)
    o_refs = (oa_ref, ob_ref, oc_ref, od_ref)

    def pre(i):
        return (_dot(hb, wg_ref[:, i * d:(i + 1) * d]), _dot(o_refs[i][...].reshape(m, BRANCH), wb_ref[i]))

    merged = None
    cur = pre(0)
    for i in range(N_BRANCH):
        nxt = pre(i + 1) if i + 1 < N_BRANCH else None
        term = jax.nn.sigmoid(cur[0] + bg_ref[:, i * d:(i + 1) * d]) * cur[1]
        merged = term if merged is None else merged + term
        cur = nxt
    y = _dot(merged.astype(BF16), wo_ref[...])
    xo_ref[...] = x + mod[:, 2:3, :] * y.reshape(nb, tt, d)


def _merge(x, mod, g1, oa, ob, oc, od, w_gate, b_gate, w_branch, w_out, *, nb, tt):
    bsz, t, d = x.shape
    tok = lambda w: pl.BlockSpec((nb, tt, w), lambda bi, ti: (bi, ti, 0))
    const = lambda a: pl.BlockSpec(a.shape, lambda bi, ti: (0,) * a.ndim)
    return pl.pallas_call(
        functools.partial(_merge_kernel, nb=nb, tt=tt),
        grid=(bsz // nb, t // tt),
        in_specs=[tok(d), pl.BlockSpec((nb, 6, d), lambda bi, ti: (bi, 0, 0)), const(g1),
                  tok(BRANCH), tok(BRANCH), tok(BRANCH), tok(BRANCH),
                  const(w_gate), const(b_gate), const(w_branch), const(w_out)],
        out_specs=tok(d),
        out_shape=jax.ShapeDtypeStruct((bsz, t, d), F32),
        compiler_params=_params(2),
        name="merge",
    )(x, mod, g1, oa, ob, oc, od, w_gate, b_gate, w_branch, w_out)


def _ffn_kernel(x_ref, mod_ref, g2_ref, pfx_ref, wu_ref, wc_ref, bc_ref, wd_ref, fg_ref,
                xo_ref, st_ref, carry_sc, ext_sc, act_sc, *, nb, tt, cw, final_norm):
    m = nb * tt
    d = x_ref.shape[-1]
    dff = wd_ref.shape[0]
    ti = pl.program_id(1)
    keep = CONV_W - 1

    @pl.when(ti == 0)
    def _():
        carry_sc[...] = jnp.zeros_like(carry_sc)
        carry_sc[:, SUB - keep:, :] = pfx_ref[...]

    x = x_ref[...]
    mod = mod_ref[...]
    h = _rms(x, g2_ref[...]) * (1.0 + mod[:, 4:5, :]) + mod[:, 3:4, :]
    hb = h.reshape(m, d).astype(BF16)

    def up_cols(c0):
        return _dot(hb, wu_ref[:, c0:c0 + cw]).reshape(nb, tt, cw)

    def stage(up, c0, slot):
        ext_sc[slot, :, 0:SUB, :] = carry_sc[:, :, c0:c0 + cw]
        ext_sc[slot, :, SUB:, :] = up
        carry_sc[:, :, c0:c0 + cw] = up[:, tt - SUB:, :]

    def conv_rows(c0, slot, r0, rows):
        wc = wc_ref[:, c0:c0 + cw]
        win = lambda back: ext_sc[slot, :, SUB - back + r0:SUB - back + r0 + rows, :]
        conv = win(2) * wc[0:1, :] + win(1) * wc[1:2, :] + win(0) * wc[2:3, :] + bc_ref[:, c0:c0 + cw]
        return conv.reshape(nb * rows, cw)

    n_chunks = dff // cw
    ups = (up_cols(0), up_cols(dff))
    for j in range(n_chunks):
        nxt = (up_cols((j + 1) * cw), up_cols(dff + (j + 1) * cw)) if j + 1 < n_chunks else None
        slot = 2 * (j % 2)
        stage(ups[0], j * cw, slot)
        stage(ups[1], dff + j * cw, slot + 1)
        gate = conv_rows(j * cw, slot, 0, tt)
        val = conv_rows(dff + j * cw, slot + 1, 0, tt)
        act_sc[:, j * cw:(j + 1) * cw] = (gate * jax.nn.sigmoid(gate) * val).astype(BF16)
        ups = nxt
    acc = _dot(act_sc[...], wd_ref[...])
    xo = x + mod[:, 5:6, :] * acc.reshape(nb, tt, d)
    if final_norm:
        xo = _rms(xo, fg_ref[...])
    xo_ref[...] = xo

    @pl.when(ti == pl.num_programs(1) - 1)
    def _():
        st_ref[...] = carry_sc[:, SUB - keep:, :]


def _ffn(x, mod, g2, prefix, w_up, w_conv, b_conv, w_down, final_g, *, nb, tt, final_norm):
    bsz, t, d = x.shape
    dff = w_down.shape[0]
    cw = 256
    tok = pl.BlockSpec((nb, tt, d), lambda bi, ti: (bi, ti, 0))
    const = lambda a: pl.BlockSpec(a.shape, lambda bi, ti: (0,) * a.ndim)
    state = pl.BlockSpec((nb, CONV_W - 1, 2 * dff), lambda bi, ti: (bi, 0, 0))
    return pl.pallas_call(
        functools.partial(_ffn_kernel, nb=nb, tt=tt, cw=cw, final_norm=final_norm),
        grid=(bsz // nb, t // tt),
        in_specs=[tok, pl.BlockSpec((nb, 6, d), lambda bi, ti: (bi, 0, 0)), const(g2), state,
                  const(w_up), const(w_conv), const(b_conv), const(w_down), const(final_g)],
        out_specs=[tok, state],
        out_shape=[jax.ShapeDtypeStruct((bsz, t, d), F32),
                   jax.ShapeDtypeStruct((bsz, CONV_W - 1, 2 * dff), F32)],
        scratch_shapes=[pltpu.VMEM((nb, SUB, 2 * dff), F32), pltpu.VMEM((4, nb, tt + SUB, cw), F32),
                        pltpu.VMEM((nb * tt, dff), BF16)],
        compiler_params=_params(2),
        name="ffn",
    )(x, mod, g2, prefix, w_up, w_conv, b_conv, w_down, final_g)


def _rope_tables_t(pos):
    def tab(half, heads, pad):
        inv = ROPE_THETA ** (-jnp.arange(half, dtype=F32) / half)
        ang = inv[:, None] * pos.astype(F32)[None, :]
        cos, sin = jnp.cos(ang), jnp.sin(ang)
        cos_t = jnp.tile(jnp.concatenate([cos, cos], axis=0), (heads, 1))
        sin_t = jnp.tile(jnp.concatenate([-sin, sin], axis=0), (heads, 1))
        if pad:
            cos_t = jnp.concatenate([cos_t, jnp.ones((pad, pos.shape[0]), F32)], axis=0)
            sin_t = jnp.concatenate([sin_t, jnp.zeros((pad, pos.shape[0]), F32)], axis=0)
        return cos_t, sin_t

    cosa, sina = tab(HEAD_DIM // 2, N_HEADS, 0)
    cosi, sini = tab(IDX_DIM // 2, IDX_HEADS, 0)
    coss, sins = tab(IDX_DIM // 2, 1, SLAB - IDX_DIM)
    return cosa, sina, cosi, sini, coss, sins


def _layout_w_in(w_in_t, layer):
    sizes = (BRANCH, BRANCH, BRANCH, IDX_HEADS * IDX_DIM, IDX_DIM, IDX_HEADS, BRANCH, BRANCH,
             BRANCH, BRANCH, BRANCH, N_HEADS, BRANCH, BRANCH, BRANCH)
    offs = [0]
    for s in sizes:
        offs.append(offs[-1] + s)
    piece = lambda i: w_in_t[offs[i]:offs[i + 1], layer, :]
    a_q, a_k, a_v, a_qi, a_ki, a_w, b_u, b_v, c_q, c_k, c_v, c_f, d_q, d_k, d_v = (piece(i) for i in range(15))
    pad = jnp.zeros((SLAB - IDX_DIM - IDX_HEADS - N_HEADS, w_in_t.shape[-1]), w_in_t.dtype)
    return jnp.concatenate([a_q, a_k, a_v, a_qi, c_q, c_k, c_v, d_q, d_k, d_v, b_u, b_v,
                            a_ki, a_w, c_f, pad], axis=0).astype(BF16)


def _block_diag_tril_t(w_s, n, reps):
    tri = jnp.tril(jnp.ones((n, n), dtype=bool))
    w = jnp.where(tri[None], w_s[:, :n, :n], 0)
    eye = jnp.eye(reps, dtype=w.dtype)
    return jnp.einsum("ab,gts->gbsat", eye, w).reshape(w.shape[0], reps * n, reps * n).astype(BF16)


def _round_up(a, b):
    return (a + b - 1) // b * b


def _channel_major(cache):
    depth, bsz, length = cache.shape[:3]
    flat = cache.reshape(depth, bsz, length, -1)
    return jnp.swapaxes(flat, 2, 3)


def _layer(x, mod, lp, past, prefix, final_g, stacked, *, layer, depth, final_norm, past_len):
    bsz, t, d = x.shape
    is_prompt = past is None
    n = min(t, B_CHUNK)
    if is_prompt:
        nb, tt = 1, min(t, 512)
        reps = tt // n
        pos_rows = jnp.arange(t)
    else:
        nb, tt = bsz, t
        reps = (nb * tt) // n
        pos_rows = jnp.tile(past_len + jnp.arange(t), nb)
    m = nb * tt
    tables = _rope_tables_t(pos_rows)
    ws = _block_diag_tril_t(lp["w_s"], n, reps)
    bs = jnp.tile(jnp.repeat(lp["b_s"][:, :n], HEAD_DIM, axis=0), (1, reps))
    fb = jnp.broadcast_to(lp["fb"], (SLAB, m))
    lng = jnp.broadcast_to(lp["lng"], (BRANCH, m))
    lnb = jnp.broadcast_to(lp["lnb"], (BRANCH, m))
    outs = _proj(x, mod, lp["g1"], lp["w_t"], tables, fb, lng, lnb, ws, bs,
                 nb=nb, tt=tt, stacked=stacked if is_prompt else None, layer=layer, depth=depth)
    kv, (qa, qc, qd, qi, sm, ob, vb) = outs[:6], outs[6:]

    l_valid = past_len + t
    tqp = _round_up(t, Q_PAD)
    if is_prompt:
        tk = min(512, t)
        tk_d = min(256, t)
        lpad = t
        tq = min(256, tqp)
        n_sel = min(TOPK_MAX, t // 4)
        ka, va, kc, vc, kd, vd = ([a] for a in kv)
        ki_srcs, f_srcs, f_row = [sm], [sm], (SLAB_F,)
        att_layer = layer
    else:
        tk = tk_d = 256
        tq = tqp
        lpad = _round_up(past_len + tqp, tk)
        n_sel = min(TOPK_MAX, l_valid // 4)

        def tokens(a):
            a = jnp.swapaxes(a.reshape(a.shape[0], bsz, t), 0, 1)
            return jnp.pad(a, ((0, 0), (0, 0), (0, tqp - t)))

        pk_a, pv_a, pki, pk_c, pv_c, plogf, pk_d, pv_d = past
        qa, qc, qd, qi, sm_tok = (tokens(a) for a in (qa, qc, qd, qi, sm))
        new = [tokens(a) for a in kv]
        ka, va, kc, vc, kd, vd = ([p_, n_] for p_, n_ in zip((pk_a, pv_a, pk_c, pv_c, pk_d, pv_d), new))
        ki_srcs, f_srcs, f_row = [pki, sm_tok], [plogf, sm_tok], (0, SLAB_F)
        att_layer = layer
    sm_q = sm if is_prompt else sm_tok

    oa = _dsa(qa, qi, sm_q, ka, va, ki_srcs, layer=att_layer, lp=lpad, tq=tq, tk=tk, past_len=past_len,
              l_valid=l_valid, n_sel=n_sel, t_valid=t)[:, :t]
    oc = _fox(qc, kc, vc, f_srcs, f_row, layer=att_layer, lp=lpad, tq=tq, tk=tk, past_len=past_len)[:, :t]
    od = _sb(qd, kd, vd, layer=att_layer, lp=lpad, tq=tq, tk=tk_d, past_len=past_len)[:, :t]

    mtt = min(t, 512) if is_prompt else tt
    x = _merge(x, mod, lp["g1"], oa, ob, oc, od, lp["w_gate"], lp["b_gate"], lp["w_branch"], lp["w_out"],
               nb=nb, tt=mtt)
    x, conv_state = _ffn(x, mod, lp["g2"], prefix, lp["w_up"], lp["w_conv"], lp["b_conv"], lp["w_down"],
                         final_g, nb=nb, tt=tt, final_norm=final_norm)
    if is_prompt:
        rows = (sm, conv_state)
    else:
        heads = lambda a: jnp.transpose(a.reshape(N_HEADS, HEAD_DIM, bsz, t), (2, 3, 0, 1))
        small = lambda a, r0, r1: jnp.transpose(a[r0:r1].reshape(r1 - r0, bsz, t), (1, 2, 0))
        ka_n, va_n, kc_n, vc_n, kd_n, vd_n = kv
        rows = (heads(ka_n), heads(va_n), small(sm, SLAB_KI, SLAB_KI + IDX_DIM), vb, heads(kc_n), heads(vc_n),
                small(sm, SLAB_F, SLAB_F + N_HEADS), heads(kd_n), heads(vd_n), conv_state)
    return x, rows, kv


def kernel(x_prompt, x_sample, cache_a_k, cache_a_v, cache_a_kidx, cache_c_k, cache_c_v, cache_c_logf, cache_d_k, cache_d_v, state_ffn_conv, c_prompt, c_sample, norm1_g, norm2_g, w_mod, b_mod, w_in, f_bias, lnv_g, lnv_b, w_spatial, b_spatial, w_branch, w_gate, b_gate, w_out, w_up, w_conv, b_conv, w_down, final_g):
    depth = w_in.shape[0]
    bsz, t_prompt, d = x_prompt.shape
    dbsz = x_sample.shape[0]
    past_len = cache_a_k.shape[2]
    mods = _modulation(jnp.concatenate([c_prompt, c_sample], axis=0), w_mod, b_mod)
    mods = mods.reshape(depth, bsz + dbsz, 6, d)
    prefix = jnp.zeros((bsz, CONV_W - 1, w_up.shape[-1]), x_prompt.dtype)
    fg = final_g.reshape(1, d)
    past = tuple(_channel_major(c) for c in (cache_a_k, cache_a_v, cache_a_kidx, cache_c_k, cache_c_v,
                                              cache_c_logf.astype(F32), cache_d_k, cache_d_v))
    w_in_t = jnp.transpose(w_in, (2, 0, 1))
    xp, xs = x_prompt, x_sample
    rows_p, rows_s = [], []
    stacked = ()
    for l in range(depth):
        fb = jnp.zeros((SLAB, 1), F32).at[SLAB_F:SLAB_F + N_HEADS, 0].set(f_bias[l].astype(F32))
        lp = dict(g1=norm1_g[l].reshape(1, d), g2=norm2_g[l].reshape(1, d), w_t=_layout_w_in(w_in_t, l), fb=fb,
                  lng=lnv_g[l].reshape(BRANCH, 1), lnb=lnv_b[l].reshape(BRANCH, 1),
                  w_s=w_spatial[l], b_s=b_spatial[l], w_branch=w_branch[l].astype(BF16),
                  w_gate=w_gate[l].astype(BF16), b_gate=b_gate[l].reshape(1, -1), w_out=w_out[l].astype(BF16),
                  w_up=w_up[l].astype(BF16), w_conv=w_conv[l], b_conv=b_conv[l].reshape(1, -1),
                  w_down=w_down[l].astype(BF16))
        last = l == depth - 1
        xp, new_p, stacked = _layer(xp, mods[l, :bsz], lp, None, prefix, fg, stacked, layer=l, depth=depth,
                                    final_norm=last, past_len=0)
        xs, new_s, _ = _layer(xs, mods[l, bsz:], lp, past, state_ffn_conv[l], fg, None, layer=l, depth=depth,
                              final_norm=last, past_len=past_len)
        rows_p.append(new_p)
        rows_s.append(new_s)

    def stacked_s(i):
        return jnp.stack([r[i] for r in rows_s], axis=0)

    def heads_p(a):
        return jnp.transpose(a.reshape(depth, bsz, N_HEADS, HEAD_DIM, t_prompt), (0, 1, 4, 2, 3))

    sm_p = jnp.stack([r[0] for r in rows_p], axis=0)
    small_p = lambda r0, r1: jnp.swapaxes(sm_p[:, :, r0:r1, :], 2, 3)
    ka_p, va_p, kc_p, vc_p, kd_p, vd_p = (heads_p(a) for a in stacked)
    conv_p = jnp.stack([r[1] for r in rows_p], axis=0)
    return (xp, xs,
            ka_p, stacked_s(0),
            va_p, stacked_s(1),
            small_p(SLAB_KI, SLAB_KI + IDX_DIM), stacked_s(2),
            stacked_s(3),
            kc_p, stacked_s(4),
            vc_p, stacked_s(5),
            small_p(SLAB_F, SLAB_F + N_HEADS), stacked_s(6),
            kd_p, stacked_s(7),
            vd_p, stacked_s(8),
            conv_p, stacked_s(9))
```

```python
import functools

import jax
import jax.numpy as jnp
from jax import lax
from jax.experimental import pallas as pl
from jax.experimental.pallas import tpu as pltpu

F32 = jnp.float32
BF16 = jnp.bfloat16
I32 = jnp.int32
I16 = jnp.int16

HEAD_DIM = 64
N_HEADS = 4
BRANCH = N_HEADS * HEAD_DIM
IDX_HEADS = 8
IDX_DIM = 32
CHUNK = 64
TOPK_MAX = 256
B_CHUNK = 128
N_BRANCH = 4
CONV_W = 3
ROPE_THETA = 10000.0
EPS = 1e-6
LOG2E = 1.4426950408889634
Q_SCALE = HEAD_DIM ** -0.5 * LOG2E

SLAB = 128
SLAB_KI = 0
SLAB_W = 32
SLAB_F = 40
AUG = 128
F_ROWS = 8
F_BLK = 256
N_T_GROUPS = 10
NEG = -1e30
INT_MIN = -2 ** 31
I16_MIN = -2 ** 15
SB_CUTOFF = 220.0
SB_SATURATE = 100.0
Q_PAD = 128
SUB = 8
VMEM_LIMIT = 56 * 1024 * 1024


def _params(n_axes, vmem=VMEM_LIMIT):
    return pltpu.CompilerParams(dimension_semantics=("arbitrary",) * n_axes, vmem_limit_bytes=vmem)


def _dot(a, b):
    return jnp.dot(a, b, preferred_element_type=F32)


def _split2(x):
    hi = x.astype(BF16)
    lo = (x - hi.astype(F32)).astype(BF16)
    return hi, lo


def _split3(x):
    x1 = x.astype(BF16)
    r = x - x1.astype(F32)
    x2 = r.astype(BF16)
    x3 = (r - x2.astype(F32)).astype(BF16)
    return x1, x2, x3


def _softplus(z):
    return jnp.maximum(z, 0.0) + jnp.log1p(jnp.exp(-jnp.abs(z)))


def _rms(x, g):
    ms = jnp.mean(x * x, axis=-1, keepdims=True)
    return x * lax.rsqrt(ms + EPS) * g


def _rope_t(x, cos, sin_signed, half):
    rows = x.shape[0]
    row = lax.broadcasted_iota(I32, x.shape, 0)
    fwd = pltpu.roll(x, half, axis=0)
    bwd = pltpu.roll(x, rows - half, axis=0)
    rot = jnp.where((row % (2 * half)) < half, bwd, fwd)
    return x * cos + rot * sin_signed


def _mod_kernel(c_ref, w_ref, b_ref, o_ref):
    c = c_ref[...]
    a = c * jax.nn.sigmoid(c)
    a1, a2 = _split2(a)
    w1, w2 = _split2(w_ref[0])
    o_ref[0] = _dot(a1, w1) + _dot(a1, w2) + _dot(a2, w1) + b_ref[0]


def _modulation(c_all, w_mod, b_mod):
    depth, d, n = w_mod.shape
    rows = c_all.shape[0]
    tn = 1536
    return pl.pallas_call(
        _mod_kernel,
        grid=(depth, n // tn),
        in_specs=[pl.BlockSpec((rows, d), lambda l, j: (0, 0)),
                  pl.BlockSpec((1, d, tn), lambda l, j: (l, 0, j)),
                  pl.BlockSpec((1, 1, tn), lambda l, j: (l, 0, j))],
        out_specs=pl.BlockSpec((1, rows, tn), lambda l, j: (l, 0, j)),
        out_shape=jax.ShapeDtypeStruct((depth, rows, n), F32),
        compiler_params=_params(2),
        name="modulation",
    )(c_all, w_mod, b_mod.reshape(depth, 1, n))


def _proj_kernel(x_ref, mod_ref, g1_ref, wt_ref, cosa_ref, sina_ref, cosi_ref, sini_ref, coss_ref,
                 sins_ref, fb_ref, lng_ref, lnb_ref, ws_ref, bs_ref, *rest, nb, tt, n_alias):
    (ka_ref, va_ref, kc_ref, vc_ref, kd_ref, vd_ref,
     qa_ref, qc_ref, qd_ref, qi_ref, sm_ref, ob_ref, vb_ref) = rest[n_alias:]
    m = nb * tt
    d = x_ref.shape[-1]
    x = x_ref[...]
    mod = mod_ref[...]
    h = (_rms(x, g1_ref[...]) * (1.0 + mod[:, 1:2, :]) + mod[:, 0:1, :]).reshape(m, d)
    ht = h.T.astype(BF16)

    def col_t(j):
        return _dot(wt_ref[j * BRANCH:(j + 1) * BRANCH, :], ht)

    def put_t(ref, val):
        ref[...] = val.reshape(ref.shape).astype(ref.dtype)

    def put(ref, val):
        ref[...] = val.reshape(nb, tt, val.shape[-1]).astype(ref.dtype)

    cosa, sina = cosa_ref[...], sina_ref[...]
    put_t(qa_ref, _rope_t(col_t(0), cosa, sina, HEAD_DIM // 2) * Q_SCALE)
    put_t(ka_ref, _rope_t(col_t(1), cosa, sina, HEAD_DIM // 2))
    put_t(va_ref, col_t(2))
    put_t(qi_ref, _rope_t(col_t(3), cosi_ref[...], sini_ref[...], IDX_DIM // 2))
    put_t(qc_ref, col_t(4) * Q_SCALE)
    put_t(kc_ref, col_t(5))
    put_t(vc_ref, col_t(6))
    put_t(qd_ref, col_t(7) * Q_SCALE)
    put_t(kd_ref, col_t(8))
    put_t(vd_ref, col_t(9))
    sm = _dot(wt_ref[(N_T_GROUPS + 2) * BRANCH:(N_T_GROUPS + 2) * BRANCH + SLAB, :], ht)
    row = lax.broadcasted_iota(I32, sm.shape, 0)
    roped = _rope_t(sm, coss_ref[...], sins_ref[...], IDX_DIM // 2)
    logf = -_softplus(-(sm + fb_ref[...]))
    is_f = jnp.where(row >= SLAB_F, jnp.where(row < SLAB_F + N_HEADS, 1, 0), 0)
    put_t(sm_ref, jnp.where(is_f == 1, logf, roped))
    u = jax.nn.gelu(col_t(N_T_GROUPS))
    vg = jax.nn.gelu(col_t(N_T_GROUPS + 1))
    mu = jnp.mean(vg, axis=0, keepdims=True)
    vc_ = vg - mu
    var = jnp.mean(vc_ * vc_, axis=0, keepdims=True)
    vb = vc_ * lax.rsqrt(var + EPS) * lng_ref[...] + lnb_ref[...]
    put(vb_ref, vb.T)
    vbb = vb.astype(BF16)
    mixed = jnp.concatenate([_dot(vbb[g * HEAD_DIM:(g + 1) * HEAD_DIM, :], ws_ref[g]) for g in range(N_HEADS)],
                            axis=0)
    put(ob_ref, (u * (mixed + bs_ref[...])).T)


def _proj(x, mod, g1, w_t, tables, fb, lng, lnb, ws, bs, *, nb, tt, stacked, layer, depth):
    bsz, t, d = x.shape
    m = nb * tt
    n_t = t // tt
    tok = lambda w: pl.BlockSpec((nb, tt, w), lambda ti, bi: (bi, ti, 0))
    const = lambda a: pl.BlockSpec(a.shape, lambda ti, bi: (0,) * a.ndim)
    tab = lambda c: pl.BlockSpec((c, m), lambda ti, bi: (0, ti))
    if stacked is None:
        assert n_t == 1 and nb == bsz
        kv_spec = pl.BlockSpec((BRANCH, m), lambda ti, bi: (0, 0))
        kv_shape = jax.ShapeDtypeStruct((BRANCH, m), F32)
        q_spec = lambda c: pl.BlockSpec((c, m), lambda ti, bi: (0, 0))
        q_shape = lambda c, dt: jax.ShapeDtypeStruct((c, m), dt)
        alias_in = ()
    else:
        assert nb == 1
        kv_spec = pl.BlockSpec((1, 1, BRANCH, tt), lambda ti, bi: (layer, bi, 0, ti))
        kv_shape = jax.ShapeDtypeStruct((depth, bsz, BRANCH, t), F32)
        q_spec = lambda c: pl.BlockSpec((1, c, tt), lambda ti, bi: (bi, 0, ti))
        q_shape = lambda c, dt: jax.ShapeDtypeStruct((bsz, c, t), dt)
        alias_in = tuple(stacked)
    n_alias = len(alias_in)
    n_in = 15
    cosa, sina, cosi, sini, coss, sins = tables
    return pl.pallas_call(
        functools.partial(_proj_kernel, nb=nb, tt=tt, n_alias=n_alias),
        grid=(n_t, bsz // nb),
        in_specs=[tok(d), pl.BlockSpec((nb, 6, d), lambda ti, bi: (bi, 0, 0)), const(g1), const(w_t),
                  tab(BRANCH), tab(BRANCH), tab(BRANCH), tab(BRANCH), tab(SLAB), tab(SLAB),
                  const(fb), const(lng), const(lnb), const(ws), const(bs)]
                 + [pl.BlockSpec(memory_space=pl.ANY)] * n_alias,
        out_specs=[kv_spec] * 6 + [q_spec(BRANCH)] * 4 + [q_spec(SLAB), tok(BRANCH), tok(BRANCH)],
        out_shape=[kv_shape] * 6 + [q_shape(BRANCH, BF16)] * 3 + [q_shape(BRANCH, F32), q_shape(SLAB, F32),
                                                                 jax.ShapeDtypeStruct((bsz, t, BRANCH), BF16),
                                                                 jax.ShapeDtypeStruct((bsz, t, BRANCH), F32)],
        input_output_aliases={n_in + i: i for i in range(n_alias)},
        compiler_params=_params(2),
        name="proj",
    )(x, mod, g1, w_t, cosa, sina, cosi, sini, coss, sins, fb, lng, lnb, ws, bs, *alias_in)


def _load2d(ref):
    return ref[(0,) * (len(ref.shape) - 2)]


def _fill_token_major(dst_sc, srcs):
    off = 0
    for s in srcs:
        n = s.shape[1]
        dst_sc[off:off + n, :] = s.T.astype(BF16)
        off += n
    lp = dst_sc.shape[0]
    if off < lp:
        dst_sc[off:lp, :] = jnp.zeros((lp - off, dst_sc.shape[1]), BF16)


def _fill_channel_major(dst_sc, srcs):
    off = 0
    for s in srcs:
        n = s.shape[1]
        dst_sc[:, off:off + n] = s.astype(BF16)
        off += n
    lp = dst_sc.shape[1]
    if off < lp:
        dst_sc[:, off:lp] = jnp.zeros((dst_sc.shape[0], lp - off), BF16)


def _flash_init(tq):
    return tuple((jnp.full((1, tq), NEG, F32), jnp.zeros((1, tq), F32), jnp.zeros((HEAD_DIM, tq), F32))
                 for _ in range(N_HEADS))


def _flash_step(carry, score_fns, vt_sc, off, tk):
    scores = [fn() for fn in score_fns]
    ms = [jnp.maximum(carry[h][0], jnp.max(scores[h], axis=0, keepdims=True)) for h in range(N_HEADS)]
    ps = [jnp.exp2(scores[h] - ms[h]) for h in range(N_HEADS)]
    pvs = [_dot(vt_sc[h * HEAD_DIM:(h + 1) * HEAD_DIM, pl.ds(off, tk)], ps[h].astype(BF16))
           for h in range(N_HEADS)]
    new = []
    for h in range(N_HEADS):
        m_run, l_run, acc = carry[h]
        alpha = jnp.exp2(m_run - ms[h])
        new.append((ms[h], alpha * l_run + jnp.sum(ps[h], axis=0, keepdims=True), alpha * acc + pvs[h]))
    return tuple(new)


def _flash_finish(carry):
    out_t = jnp.concatenate([acc / l_run for (_, l_run, acc) in carry], axis=0)
    return out_t.T


def _tree_sum(terms):
    while len(terms) > 1:
        terms = [a + b for a, b in zip(terms[::2], terms[1::2])] + ([terms[-1]] if len(terms) % 2 else [])
    return terms[0]


def _masked_heads_t(q_t):
    head = lax.broadcasted_iota(I32, q_t.shape, 0) // HEAD_DIM
    return [jnp.where(head == h, q_t, jnp.zeros_like(q_t)) for h in range(N_HEADS)]


def _src_spec(arr, layer):
    if arr.ndim == 4:
        return pl.BlockSpec((1, 1) + arr.shape[2:], lambda b, i: (layer, b, 0, 0))
    return pl.BlockSpec((1,) + arr.shape[1:], lambda b, i: (b, 0, 0))


def _q_spec(c, tq):
    return pl.BlockSpec((1, c, tq), lambda b, i: (b, 0, i))


def _o_spec(tq):
    return pl.BlockSpec((1, tq, BRANCH), lambda b, i: (b, i, 0))


def _dsa_kernel(qa_ref, qi_ref, qsm_ref, *refs, n_src, tq, tk, past_len, l_valid, n_sel, t_valid, single_q):
    k_refs, v_refs, ki_refs = refs[:n_src], refs[n_src:2 * n_src], refs[2 * n_src:3 * n_src]
    o_ref, kb_sc, vt_sc, kic_sc, lhs_sc, key_sc, hi_sc, lo_sc, thr_sc, room_sc, tie_sc = refs[3 * n_src:]
    qb = pl.program_id(1)

    @pl.when(qb == 0)
    def _():
        _fill_token_major(kb_sc, [_load2d(r) for r in k_refs])
        _fill_channel_major(vt_sc, [_load2d(r) for r in v_refs])
        cats = []
        for r in ki_refs:
            ki = _load2d(r)[SLAB_KI:SLAB_KI + IDX_DIM, :]
            hi = ki.astype(BF16).astype(F32)
            cats.append(jnp.concatenate([hi, ki - hi, hi, jnp.zeros_like(hi)], axis=0))
        _fill_token_major(kic_sc, cats)

    qi = qi_ref[0]
    for h in range(IDX_HEADS):
        piece = qi[h * IDX_DIM:(h + 1) * IDX_DIM, :]
        hi = piece.astype(BF16).astype(F32)
        lhs_sc[h] = jnp.concatenate([hi, hi, piece - hi, jnp.zeros_like(hi)], axis=0).astype(BF16)
    w_t = qsm_ref[0]

    qlane = lax.broadcasted_iota(I32, (1, tq), 1)
    if single_q:
        row0 = past_len
        top_limit = min(((row0 + tq - 1) // CHUNK + 1) * CHUNK, l_valid)
        count_loop = functools.partial(lax.fori_loop, unroll=True)
        block_loop = functools.partial(lax.fori_loop, unroll=2)
    else:
        row0 = past_len + qb * tq
        top_limit = jnp.minimum(((row0 + tq - 1) // CHUNK + 1) * CHUNK, l_valid)
        count_loop = block_loop = lax.fori_loop
    pos = row0 + qlane
    limit = jnp.minimum((pos // CHUNK + 1) * CHUNK, l_valid)
    nblk = (top_limit + tk - 1) // tk

    def score_blk(j, _):
        off = pl.multiple_of(j * tk, tk)
        kc = kic_sc[pl.ds(off, tk), :]
        dots = [_dot(kc, lhs_sc[h]) for h in range(IDX_HEADS)]
        acc = jnp.zeros((tk, tq), F32)
        for h in range(IDX_HEADS):
            acc = acc + w_t[SLAB_W + h:SLAB_W + h + 1, :] * jnp.maximum(dots[h], 0.0)
        acc = jnp.where(acc == 0.0, 0.0, acc)
        bits = pltpu.bitcast(acc, I32)
        key = jnp.where(bits < 0, bits ^ 0x7FFFFFFF, bits)
        kidx = off + lax.broadcasted_iota(I32, (tk, tq), 0)
        key = jnp.where(kidx < limit, key, INT_MIN)
        key_sc[pl.ds(off, tk), :] = key
        hi_sc[pl.ds(off, tk), :] = (key >> 16).astype(I16)
        lo_sc[pl.ds(off, tk), :] = ((key & 0xFFFF) + I16_MIN).astype(I16)
        return 0

    block_loop(0, nblk, score_blk, 0)

    def count16(ref, cand):
        cand_b = jnp.broadcast_to(cand, (16, tq)).astype(I16)

        def body(j, part):
            off = pl.multiple_of(j * tk, tk)
            kb = ref[pl.ds(off, tk), :]
            return part + _tree_sum([jnp.where(kb[c * 16:(c + 1) * 16, :] >= cand_b, jnp.int16(1), jnp.int16(0))
                                     for c in range(tk // 16)])

        part = count_loop(0, nblk, body, jnp.zeros((16, tq), I16))
        return jnp.sum(part.astype(F32), axis=0, keepdims=True)

    def count32(cand):
        cand_b = jnp.broadcast_to(cand, (8, tq))

        def body(j, part):
            off = pl.multiple_of(j * tk, tk)
            kb = key_sc[pl.ds(off, tk), :]
            return part + _tree_sum([jnp.where(kb[c * 8:(c + 1) * 8, :] >= cand_b, 1.0, 0.0)
                                     for c in range(tk // 8)])

        part = count_loop(0, nblk, body, jnp.zeros((8, tq), F32))
        return jnp.sum(part, axis=0, keepdims=True)

    def kth16(ref, want):
        t = jnp.where(count16(ref, jnp.zeros((1, tq), I32)) >= want, 0, I16_MIN).astype(I32)

        def search(i, t):
            cand = t | jnp.left_shift(jnp.int32(1), 14 - i)
            return jnp.where(count16(ref, cand) >= want, cand, t)

        return lax.fori_loop(0, 15, search, t)

    kf = float(n_sel)
    thr_sc[...] = jnp.full(thr_sc.shape, INT_MIN + 1, I32)
    room_sc[...] = jnp.zeros(room_sc.shape, F32)
    tie_sc[...] = jnp.zeros(tie_sc.shape, F32)

    @pl.when(top_limit > n_sel)
    def _():
        t_hi = kth16(hi_sc, kf)
        above = jnp.where(t_hi >= -I16_MIN - 1, 0.0, count16(hi_sc, t_hi + 1))
        t_hi_b = jnp.broadcast_to(t_hi, (16, tq)).astype(I16)

        def keep_equal_hi(j, _):
            off = pl.multiple_of(j * tk, tk)
            hi_blk = hi_sc[pl.ds(off, tk), :]
            lo_blk = lo_sc[pl.ds(off, tk), :]
            kept = [jnp.where(hi_blk[c * 16:(c + 1) * 16, :] == t_hi_b, lo_blk[c * 16:(c + 1) * 16, :],
                              jnp.int16(I16_MIN)) for c in range(tk // 16)]
            lo_sc[pl.ds(off, tk), :] = jnp.concatenate(kept, axis=0)
            return 0

        count_loop(0, nblk, keep_equal_hi, 0)
        t_lo = kth16(lo_sc, kf - above)
        t_full = jnp.maximum(t_hi * 65536 + (t_lo - I16_MIN), INT_MIN + 1)
        cnt_ge = count32(t_full)
        cnt_gt = count32(t_full + 1)
        real_q = qlane < (t_valid - qb * tq)
        thr_sc[...] = jnp.broadcast_to(t_full, thr_sc.shape)
        room_sc[...] = jnp.broadcast_to(kf - cnt_gt, room_sc.shape)
        tie_sc[...] = jnp.broadcast_to(jnp.where(real_q, jnp.where(cnt_ge > kf, 1.0, 0.0), 0.0), tie_sc.shape)

    thr = thr_sc[0:1, :]
    room = room_sc[0:1, :]
    any_tie = jnp.max(tie_sc[0:1, :]) > 0.0

    @pl.when(any_tie)
    def _():
        ri = lax.broadcasted_iota(I32, (tk, tk), 0)
        ci = lax.broadcasted_iota(I32, (tk, tk), 1)
        upto = jnp.where(ci <= ri, 1.0, 0.0).astype(BF16)

        def drop_late_ties(j, seen):
            off = pl.multiple_of(j * tk, tk)
            kb = key_sc[pl.ds(off, tk), :]
            eq = jnp.where(kb == thr, 1.0, 0.0)
            rank = _dot(upto, eq.astype(BF16)) + seen
            key_sc[pl.ds(off, tk), :] = jnp.where(eq * rank > room, INT_MIN, kb)
            return seen + jnp.sum(eq, axis=0, keepdims=True)

        lax.fori_loop(0, nblk, drop_late_ties, jnp.zeros((1, tq), F32))

    qh = _masked_heads_t(qa_ref[0])

    def attend(j, carry):
        off = pl.multiple_of(j * tk, tk)
        kblk = kb_sc[pl.ds(off, tk), :]
        keep = key_sc[pl.ds(off, tk), :] >= thr
        score_fns = [functools.partial(lambda h: jnp.where(keep, _dot(kblk, qh[h]), NEG), h) for h in range(N_HEADS)]
        return _flash_step(carry, score_fns, vt_sc, off, tk)

    carry = block_loop(0, nblk, attend, _flash_init(tq))
    o_ref[0] = _flash_finish(carry).astype(o_ref.dtype)


def _dsa(qa, qi, qsm, k_srcs, v_srcs, ki_srcs, *, layer, lp, tq, tk, past_len, l_valid, n_sel, t_valid):
    bsz, _, t = qa.shape
    n_src = len(k_srcs)
    srcs = list(k_srcs) + list(v_srcs) + list(ki_srcs)
    return pl.pallas_call(
        functools.partial(_dsa_kernel, n_src=n_src, tq=tq, tk=tk, past_len=past_len, l_valid=l_valid,
                          n_sel=n_sel, t_valid=t_valid, single_q=(t == tq)),
        grid=(bsz, t // tq),
        in_specs=[_q_spec(BRANCH, tq), _q_spec(BRANCH, tq), _q_spec(SLAB, tq)] + [_src_spec(a, layer) for a in srcs],
        out_specs=_o_spec(tq),
        out_shape=jax.ShapeDtypeStruct((bsz, t, BRANCH), BF16),
        scratch_shapes=[pltpu.VMEM((lp, BRANCH), BF16), pltpu.VMEM((BRANCH, lp), BF16),
                        pltpu.VMEM((lp, SLAB), BF16), pltpu.VMEM((IDX_HEADS, SLAB, tq), BF16),
                        pltpu.VMEM((lp, tq), I32), pltpu.VMEM((lp, tq), I16), pltpu.VMEM((lp, tq), I16),
                        pltpu.VMEM((SUB, tq), I32), pltpu.VMEM((SUB, tq), F32), pltpu.VMEM((SUB, tq), F32)],
        compiler_params=_params(2),
        name="dsa",
    )(qa, qi, qsm, *srcs)


def _fox_kernel(q_ref, *refs, n_src, f_row, tq, tk, past_len):
    k_refs, v_refs, f_refs = refs[:n_src], refs[n_src:2 * n_src], refs[2 * n_src:3 * n_src]
    o_ref, vt_sc, kaug_sc, f_sc = refs[3 * n_src:]
    qb = pl.program_id(1)
    lp = f_sc.shape[1]

    @pl.when(qb == 0)
    def _():
        _fill_channel_major(vt_sc, [_load2d(r) for r in v_refs])
        ks = [_load2d(r) for r in k_refs]
        fs = []
        for r, row in zip(f_refs, f_row):
            logf = _load2d(r)
            if row + F_ROWS <= logf.shape[0]:
                fs.append(logf[row:row + F_ROWS, :])
            else:
                gates = logf[row:row + N_HEADS, :]
                fs.append(jnp.concatenate([gates, jnp.zeros((F_ROWS - N_HEADS, gates.shape[1]), F32)], axis=0))
        kt = ks[0] if len(ks) == 1 else jnp.concatenate(ks, axis=1)
        ft = fs[0] if len(fs) == 1 else jnp.concatenate(fs, axis=1)
        n_real = kt.shape[1]
        ri = lax.broadcasted_iota(I32, (F_BLK, F_BLK), 0)
        ci = lax.broadcasted_iota(I32, (F_BLK, F_BLK), 1)
        upto = jnp.where(ri <= ci, 1.0, 0.0).astype(BF16)
        rowx = lax.broadcasted_iota(I32, (HEAD_DIM, F_BLK), 0)
        run = jnp.zeros((F_ROWS, 1), F32)
        for b in range(lp // F_BLK):
            lo, hi_ = b * F_BLK, min((b + 1) * F_BLK, n_real)
            if hi_ <= lo:
                kaug_sc[lo:lo + F_BLK, :] = jnp.zeros((F_BLK, N_HEADS * AUG), BF16)
                f_sc[:, lo:lo + F_BLK] = jnp.broadcast_to(run * LOG2E, (F_ROWS, F_BLK))
                continue
            k_blk, f_blk = kt[:, lo:hi_], ft[:, lo:hi_]
            if hi_ - lo < F_BLK:
                k_blk = jnp.concatenate([k_blk, jnp.zeros((BRANCH, F_BLK - (hi_ - lo)), F32)], axis=1)
                f_blk = jnp.concatenate([f_blk, jnp.zeros((F_ROWS, F_BLK - (hi_ - lo)), F32)], axis=1)
            s1, s2, s3 = _split3(f_blk)
            fsum = _dot(s1, upto) + _dot(s2, upto) + _dot(s3, upto) + run
            run = fsum[:, F_BLK - 1:F_BLK]
            fsum = fsum * LOG2E
            f_sc[:, lo:lo + F_BLK] = fsum
            f1, f2, f3 = (p.astype(F32) for p in _split3(fsum))
            for h in range(N_HEADS):
                extra = jnp.where(rowx < 3, 1.0,
                                  jnp.where(rowx == 3, -f1[h:h + 1, :],
                                            jnp.where(rowx == 4, -f2[h:h + 1, :],
                                                      jnp.where(rowx == 5, -f3[h:h + 1, :], 0.0))))
                aug_t = jnp.concatenate([k_blk[h * HEAD_DIM:(h + 1) * HEAD_DIM, :], extra], axis=0)
                kaug_sc[lo:lo + F_BLK, h * AUG:(h + 1) * AUG] = aug_t.T.astype(BF16)

    row0 = past_len + qb * tq
    pos = row0 + lax.broadcasted_iota(I32, (1, tq), 1)
    n_full = row0 // tk
    n_all = (row0 + tq + tk - 1) // tk
    q_t = q_ref[0].astype(F32)
    f1, f2, f3 = (p.astype(F32) for p in _split3(f_sc[:, pl.ds(pl.multiple_of(row0, Q_PAD), tq)]))
    rowq = lax.broadcasted_iota(I32, (HEAD_DIM, tq), 0)
    qh = []
    for h in range(N_HEADS):
        extra = jnp.where(rowq == 0, f1[h:h + 1, :],
                          jnp.where(rowq == 1, f2[h:h + 1, :],
                                    jnp.where(rowq == 2, f3[h:h + 1, :], jnp.where(rowq < 6, 1.0, 0.0))))
        qh.append(jnp.concatenate([q_t[h * HEAD_DIM:(h + 1) * HEAD_DIM, :], extra], axis=0).astype(BF16))

    def step(j, carry, masked):
        off = pl.multiple_of(j * tk, tk)
        causal = (off + lax.broadcasted_iota(I32, (tk, tq), 0) <= pos) if masked else None

        def score(h):
            s = _dot(kaug_sc[pl.ds(off, tk), h * AUG:(h + 1) * AUG], qh[h])
            return jnp.where(causal, s, NEG) if masked else s

        return _flash_step(carry, [functools.partial(score, h) for h in range(N_HEADS)], vt_sc, off, tk)

    carry = lax.fori_loop(0, n_full, lambda j, c: step(j, c, False), _flash_init(tq))
    carry = lax.fori_loop(n_full, n_all, lambda j, c: step(j, c, True), carry)
    o_ref[0] = _flash_finish(carry).astype(o_ref.dtype)


def _fox(q, k_srcs, v_srcs, f_srcs, f_row, *, layer, lp, tq, tk, past_len):
    bsz, _, t = q.shape
    n_src = len(k_srcs)
    srcs = list(k_srcs) + list(v_srcs) + list(f_srcs)
    return pl.pallas_call(
        functools.partial(_fox_kernel, n_src=n_src, f_row=tuple(f_row), tq=tq, tk=tk, past_len=past_len),
        grid=(bsz, t // tq),
        in_specs=[_q_spec(BRANCH, tq)] + [_src_spec(a, layer) for a in srcs],
        out_specs=_o_spec(tq),
        out_shape=jax.ShapeDtypeStruct((bsz, t, BRANCH), BF16),
        scratch_shapes=[pltpu.VMEM((BRANCH, lp), BF16), pltpu.VMEM((lp, N_HEADS * AUG), BF16),
                        pltpu.VMEM((F_ROWS, lp), F32)],
        compiler_params=_params(2),
        name="fox",
    )(q, *srcs)


def _sb_kernel(q_ref, *refs, n_src, tq, tk, past_len):
    k_refs, v_refs = refs[:n_src], refs[n_src:2 * n_src]
    o_ref, kb_sc, vt_sc = refs[2 * n_src:]
    qb = pl.program_id(1)

    @pl.when(qb == 0)
    def _():
        _fill_token_major(kb_sc, [_load2d(r) for r in k_refs])
        _fill_channel_major(vt_sc, [_load2d(r) for r in v_refs])

    qh = _masked_heads_t(q_ref[0])
    row0 = past_len + qb * tq
    pos = row0 + lax.broadcasted_iota(I32, (1, tq), 1)
    j_top = (row0 + tq - 1) // tk
    ri = lax.broadcasted_iota(I32, (tk, tk), 0)
    ci = lax.broadcasted_iota(I32, (tk, tk), 1)
    after = jnp.where(ci > ri, 1.0, 0.0).astype(BF16)

    def cond(carry):
        j, state = carry
        live = state[0][0]
        for h in range(1, N_HEADS):
            live = jnp.maximum(live, state[h][0])
        return jnp.logical_and(j >= 0, jnp.max(live) > -SB_CUTOFF)

    def body(carry, masked=False):
        j, state = carry
        off = pl.multiple_of(j * tk, tk)
        kblk = kb_sc[pl.ds(off, tk), :]
        zs = [_dot(kblk, qh[h]) for h in range(N_HEADS)]
        keeps = [-jnp.maximum(jnp.log2(1.0 + jnp.exp2(jnp.minimum(z, SB_SATURATE))), z) for z in zs]
        if masked:
            strict = off + lax.broadcasted_iota(I32, (tk, tq), 0) < pos
            keeps = [jnp.where(strict, kp, 0.0) for kp in keeps]
        laters = []
        for h in range(N_HEADS):
            k_hi, k_lo = _split2(keeps[h])
            laters.append(_dot(after, k_hi) + _dot(after, k_lo) + state[h][0])
        ws = [jnp.exp2(zs[h] + keeps[h] + laters[h]) for h in range(N_HEADS)]
        if masked:
            ws = [jnp.where(strict, w, 0.0) for w in ws]
        pvs = [_dot(vt_sc[h * HEAD_DIM:(h + 1) * HEAD_DIM, pl.ds(off, tk)], ws[h].astype(BF16))
               for h in range(N_HEADS)]
        new = tuple((state[h][0] + jnp.sum(keeps[h], axis=0, keepdims=True), state[h][1] + pvs[h])
                    for h in range(N_HEADS))
        return j - 1, new

    init = tuple((jnp.zeros((1, tq), F32), jnp.zeros((HEAD_DIM, tq), F32)) for _ in range(N_HEADS))
    first = body((j_top, init), masked=True)
    _, state = lax.while_loop(cond, body, first)
    out_t = jnp.concatenate([acc for (_, acc) in state], axis=0)
    o_ref[0] = out_t.T.astype(o_ref.dtype)


def _sb(q, k_srcs, v_srcs, *, layer, lp, tq, tk, past_len):
    bsz, _, t = q.shape
    assert tk % tq == 0 and past_len % tq == 0
    n_src = len(k_srcs)
    srcs = list(k_srcs) + list(v_srcs)
    return pl.pallas_call(
        functools.partial(_sb_kernel, n_src=n_src, tq=tq, tk=tk, past_len=past_len),
        grid=(bsz, t // tq),
        in_specs=[_q_spec(BRANCH, tq)] + [_src_spec(a, layer) for a in srcs],
        out_specs=_o_spec(tq),
        out_shape=jax.ShapeDtypeStruct((bsz, t, BRANCH), BF16),
        scratch_shapes=[pltpu.VMEM((lp, BRANCH), BF16), pltpu.VMEM((BRANCH, lp), BF16)],
        compiler_params=_params(2),
        name="sb",
    )(q, *srcs)


def _merge_kernel(x_ref, mod_ref, g1_ref, oa_ref, ob_ref, oc_ref, od_ref, wg_ref, bg_ref, wb_ref, wo_ref,
                  xo_ref, *, nb, tt):
    m = nb * tt
    d = x_ref.shape[-1]
    x = x_ref[...]
    mod = mod_ref[...]
    h = _rms(x, g1_ref[...]) * (1.0 + mod[:, 1:2, :]) + mod[:, 0:1, :]
    hb = h.reshape(m, d).astype(BF16)
    o_refs = (oa_ref, ob_ref, oc_ref, od_ref)

    def pre(i):
        return (_dot(hb, wg_ref[:, i * d:(i + 1) * d]), _dot(o_refs[i][...].reshape(m, BRANCH), wb_ref[i]))

    merged = None
    cur = pre(0)
    for i in range(N_BRANCH):
        nxt = pre(i + 1) if i + 1 < N_BRANCH else None
        term = jax.nn.sigmoid(cur[0] + bg_ref[:, i * d:(i + 1) * d]) * cur[1]
        merged = term if merged is None else merged + term
        cur = nxt
    y = _dot(merged.astype(BF16), wo_ref[...])
    xo_ref[...] = x + mod[:, 2:3, :] * y.reshape(nb, tt, d)


def _merge(x, mod, g1, oa, ob, oc, od, w_gate, b_gate, w_branch, w_out, *, nb, tt):
    bsz, t, d = x.shape
    tok = lambda w: pl.BlockSpec((nb, tt, w), lambda bi, ti: (bi, ti, 0))
    const = lambda a: pl.BlockSpec(a.shape, lambda bi, ti: (0,) * a.ndim)
    return pl.pallas_call(
        functools.partial(_merge_kernel, nb=nb, tt=tt),
        grid=(bsz // nb, t // tt),
        in_specs=[tok(d), pl.BlockSpec((nb, 6, d), lambda bi, ti: (bi, 0, 0)), const(g1),
                  tok(BRANCH), tok(BRANCH), tok(BRANCH), tok(BRANCH),
                  const(w_gate), const(b_gate), const(w_branch), const(w_out)],
        out_specs=tok(d),
        out_shape=jax.ShapeDtypeStruct((bsz, t, d), F32),
        compiler_params=_params(2),
        name="merge",
    )(x, mod, g1, oa, ob, oc, od, w_gate, b_gate, w_branch, w_out)


def _ffn_kernel(x_ref, mod_ref, g2_ref, pfx_ref, wu_ref, wc_ref, bc_ref, wd_ref, fg_ref,
                xo_ref, st_ref, carry_sc, ext_sc, act_sc, *, nb, tt, cw, final_norm):
    m = nb * tt
    d = x_ref.shape[-1]
    dff = wd_ref.shape[0]
    ti = pl.program_id(1)
    keep = CONV_W - 1

    @pl.when(ti == 0)
    def _():
        carry_sc[...] = jnp.zeros_like(carry_sc)
        carry_sc[:, SUB - keep:, :] = pfx_ref[...]

    x = x_ref[...]
    mod = mod_ref[...]
    h = _rms(x, g2_ref[...]) * (1.0 + mod[:, 4:5, :]) + mod[:, 3:4, :]
    hb = h.reshape(m, d).astype(BF16)

    def up_cols(c0):
        return _dot(hb, wu_ref[:, c0:c0 + cw]).reshape(nb, tt, cw)

    def stage(up, c0, slot):
        ext_sc[slot, :, 0:SUB, :] = carry_sc[:, :, c0:c0 + cw]
        ext_sc[slot, :, SUB:, :] = up
        carry_sc[:, :, c0:c0 + cw] = up[:, tt - SUB:, :]

    def conv(c0, slot):
        wc = wc_ref[:, c0:c0 + cw]
        win = lambda back: ext_sc[slot, :, SUB - back:SUB - back + tt, :]
        out = win(2) * wc[0:1, :] + win(1) * wc[1:2, :] + win(0) * wc[2:3, :] + bc_ref[:, c0:c0 + cw]
        return out.reshape(m, cw)

    n_chunks = dff // cw
    split = (n_chunks + 1) // 2 * cw
    ups = (up_cols(0), up_cols(dff))
    acc = None
    for j in range(n_chunks):
        nxt = (up_cols((j + 1) * cw), up_cols(dff + (j + 1) * cw)) if j + 1 < n_chunks else None
        slot = 2 * (j % 2)
        stage(ups[0], j * cw, slot)
        stage(ups[1], dff + j * cw, slot + 1)
        gate = conv(j * cw, slot)
        val = conv(dff + j * cw, slot + 1)
        act_sc[:, j * cw:(j + 1) * cw] = (gate * jax.nn.sigmoid(gate) * val).astype(BF16)
        if (j + 1) * cw == split:
            acc = _dot(act_sc[:, :split], wd_ref[:split, :])
        ups = nxt
    acc = acc + _dot(act_sc[:, split:], wd_ref[split:, :])
    xo = x + mod[:, 5:6, :] * acc.reshape(nb, tt, d)
    if final_norm:
        xo = _rms(xo, fg_ref[...])
    xo_ref[...] = xo

    @pl.when(ti == pl.num_programs(1) - 1)
    def _():
        st_ref[...] = carry_sc[:, SUB - keep:, :]


def _ffn(x, mod, g2, prefix, w_up, w_conv, b_conv, w_down, final_g, *, nb, tt, final_norm):
    bsz, t, d = x.shape
    dff = w_down.shape[0]
    cw = 256
    tok = pl.BlockSpec((nb, tt, d), lambda bi, ti: (bi, ti, 0))
    const = lambda a: pl.BlockSpec(a.shape, lambda bi, ti: (0,) * a.ndim)
    state = pl.BlockSpec((nb, CONV_W - 1, 2 * dff), lambda bi, ti: (bi, 0, 0))
    return pl.pallas_call(
        functools.partial(_ffn_kernel, nb=nb, tt=tt, cw=cw, final_norm=final_norm),
        grid=(bsz // nb, t // tt),
        in_specs=[tok, pl.BlockSpec((nb, 6, d), lambda bi, ti: (bi, 0, 0)), const(g2), state,
                  const(w_up), const(w_conv), const(b_conv), const(w_down), const(final_g)],
        out_specs=[tok, state],
        out_shape=[jax.ShapeDtypeStruct((bsz, t, d), F32),
                   jax.ShapeDtypeStruct((bsz, CONV_W - 1, 2 * dff), F32)],
        scratch_shapes=[pltpu.VMEM((nb, SUB, 2 * dff), F32), pltpu.VMEM((4, nb, tt + SUB, cw), F32),
                        pltpu.VMEM((nb * tt, dff), BF16)],
        compiler_params=_params(2),
        name="ffn",
    )(x, mod, g2, prefix, w_up, w_conv, b_conv, w_down, final_g)


def _rope_tables_t(pos):
    def tab(half, heads, pad):
        inv = ROPE_THETA ** (-jnp.arange(half, dtype=F32) / half)
        ang = inv[:, None] * pos.astype(F32)[None, :]
        cos, sin = jnp.cos(ang), jnp.sin(ang)
        cos_t = jnp.tile(jnp.concatenate([cos, cos], axis=0), (heads, 1))
        sin_t = jnp.tile(jnp.concatenate([-sin, sin], axis=0), (heads, 1))
        if pad:
            cos_t = jnp.concatenate([cos_t, jnp.ones((pad, pos.shape[0]), F32)], axis=0)
            sin_t = jnp.concatenate([sin_t, jnp.zeros((pad, pos.shape[0]), F32)], axis=0)
        return cos_t, sin_t

    cosa, sina = tab(HEAD_DIM // 2, N_HEADS, 0)
    cosi, sini = tab(IDX_DIM // 2, IDX_HEADS, 0)
    coss, sins = tab(IDX_DIM // 2, 1, SLAB - IDX_DIM)
    return cosa, sina, cosi, sini, coss, sins


def _layout_w_in(w_in_t, layer):
    sizes = (BRANCH, BRANCH, BRANCH, IDX_HEADS * IDX_DIM, IDX_DIM, IDX_HEADS, BRANCH, BRANCH,
             BRANCH, BRANCH, BRANCH, N_HEADS, BRANCH, BRANCH, BRANCH)
    offs = [0]
    for s in sizes:
        offs.append(offs[-1] + s)
    piece = lambda i: w_in_t[offs[i]:offs[i + 1], layer, :]
    a_q, a_k, a_v, a_qi, a_ki, a_w, b_u, b_v, c_q, c_k, c_v, c_f, d_q, d_k, d_v = (piece(i) for i in range(15))
    pad = jnp.zeros((SLAB - IDX_DIM - IDX_HEADS - N_HEADS, w_in_t.shape[-1]), w_in_t.dtype)
    return jnp.concatenate([a_q, a_k, a_v, a_qi, c_q, c_k, c_v, d_q, d_k, d_v, b_u, b_v,
                            a_ki, a_w, c_f, pad], axis=0).astype(BF16)


def _block_diag_tril_t(w_s, n, reps):
    tri = jnp.tril(jnp.ones((n, n), dtype=bool))
    w = jnp.where(tri[None], w_s[:, :n, :n], 0)
    eye = jnp.eye(reps, dtype=w.dtype)
    return jnp.einsum("ab,gts->gbsat", eye, w).reshape(w.shape[0], reps * n, reps * n).astype(BF16)


def _round_up(a, b):
    return (a + b - 1) // b * b


def _channel_major(cache):
    depth, bsz, length = cache.shape[:3]
    flat = cache.reshape(depth, bsz, length, -1)
    return jnp.swapaxes(flat, 2, 3)


def _layer(x, mod, lp, past, prefix, final_g, stacked, *, layer, depth, final_norm, past_len):
    bsz, t, d = x.shape
    is_prompt = past is None
    n = min(t, B_CHUNK)
    if is_prompt:
        nb, tt = 1, min(t, 512)
        reps = tt // n
        pos_rows = jnp.arange(t)
    else:
        nb, tt = bsz, t
        reps = (nb * tt) // n
        pos_rows = jnp.tile(past_len + jnp.arange(t), nb)
    m = nb * tt
    tables = _rope_tables_t(pos_rows)
    ws = _block_diag_tril_t(lp["w_s"], n, reps)
    bs = jnp.tile(jnp.repeat(lp["b_s"][:, :n], HEAD_DIM, axis=0), (1, reps))
    fb = jnp.broadcast_to(lp["fb"], (SLAB, m))
    lng = jnp.broadcast_to(lp["lng"], (BRANCH, m))
    lnb = jnp.broadcast_to(lp["lnb"], (BRANCH, m))
    outs = _proj(x, mod, lp["g1"], lp["w_t"], tables, fb, lng, lnb, ws, bs,
                 nb=nb, tt=tt, stacked=stacked if is_prompt else None, layer=layer, depth=depth)
    kv, (qa, qc, qd, qi, sm, ob, vb) = outs[:6], outs[6:]

    l_valid = past_len + t
    tqp = _round_up(t, Q_PAD)
    if is_prompt:
        tk = min(512, t)
        tk_d = min(256, t)
        lpad = t
        tq = min(256, tqp)
        n_sel = min(TOPK_MAX, t // 4)
        ka, va, kc, vc, kd, vd = ([a] for a in kv)
        ki_srcs, f_srcs, f_row = [sm], [sm], (SLAB_F,)
        att_layer = layer
    else:
        tk = tk_d = 256
        tq = tqp
        lpad = _round_up(past_len + tqp, tk)
        n_sel = min(TOPK_MAX, l_valid // 4)

        def tokens(a):
            a = jnp.swapaxes(a.reshape(a.shape[0], bsz, t), 0, 1)
            return jnp.pad(a, ((0, 0), (0, 0), (0, tqp - t)))

        pk_a, pv_a, pki, pk_c, pv_c, plogf, pk_d, pv_d = past
        qa, qc, qd, qi, sm_tok = (tokens(a) for a in (qa, qc, qd, qi, sm))
        new = [tokens(a) for a in kv]
        ka, va, kc, vc, kd, vd = ([p_, n_] for p_, n_ in zip((pk_a, pv_a, pk_c, pv_c, pk_d, pv_d), new))
        ki_srcs, f_srcs, f_row = [pki, sm_tok], [plogf, sm_tok], (0, SLAB_F)
        att_layer = layer
    sm_q = sm if is_prompt else sm_tok

    tq_a = min(512, tqp) if is_prompt else tq
    oa = _dsa(qa, qi, sm_q, ka, va, ki_srcs, layer=att_layer, lp=lpad, tq=tq_a, tk=tk, past_len=past_len,
              l_valid=l_valid, n_sel=n_sel, t_valid=t)[:, :t]
    oc = _fox(qc, kc, vc, f_srcs, f_row, layer=att_layer, lp=lpad, tq=tq, tk=tk, past_len=past_len)[:, :t]
    od = _sb(qd, kd, vd, layer=att_layer, lp=lpad, tq=tq, tk=tk_d, past_len=past_len)[:, :t]

    mtt = min(t, 512) if is_prompt else tt
    x = _merge(x, mod, lp["g1"], oa, ob, oc, od, lp["w_gate"], lp["b_gate"], lp["w_branch"], lp["w_out"],
               nb=nb, tt=mtt)
    x, conv_state = _ffn(x, mod, lp["g2"], prefix, lp["w_up"], lp["w_conv"], lp["b_conv"], lp["w_down"],
                         final_g, nb=nb, tt=tt, final_norm=final_norm)
    if is_prompt:
        rows = (sm, conv_state)
    else:
        heads = lambda a: jnp.transpose(a.reshape(N_HEADS, HEAD_DIM, bsz, t), (2, 3, 0, 1))
        small = lambda a, r0, r1: jnp.transpose(a[r0:r1].reshape(r1 - r0, bsz, t), (1, 2, 0))
        ka_n, va_n, kc_n, vc_n, kd_n, vd_n = kv
        rows = (heads(ka_n), heads(va_n), small(sm, SLAB_KI, SLAB_KI + IDX_DIM), vb, heads(kc_n), heads(vc_n),
                small(sm, SLAB_F, SLAB_F + N_HEADS), heads(kd_n), heads(vd_n), conv_state)
    return x, rows, kv


def kernel(x_prompt, x_sample, cache_a_k, cache_a_v, cache_a_kidx, cache_c_k, cache_c_v, cache_c_logf, cache_d_k, cache_d_v, state_ffn_conv, c_prompt, c_sample, norm1_g, norm2_g, w_mod, b_mod, w_in, f_bias, lnv_g, lnv_b, w_spatial, b_spatial, w_branch, w_gate, b_gate, w_out, w_up, w_conv, b_conv, w_down, final_g):
    depth = w_in.shape[0]
    bsz, t_prompt, d = x_prompt.shape
    dbsz = x_sample.shape[0]
    past_len = cache_a_k.shape[2]
    mods = _modulation(jnp.concatenate([c_prompt, c_sample], axis=0), w_mod, b_mod)
    mods = mods.reshape(depth, bsz + dbsz, 6, d)
    prefix = jnp.zeros((bsz, CONV_W - 1, w_up.shape[-1]), x_prompt.dtype)
    fg = final_g.reshape(1, d)
    past = tuple(_channel_major(c) for c in (cache_a_k, cache_a_v, cache_a_kidx, cache_c_k, cache_c_v,
                                              cache_c_logf.astype(F32), cache_d_k, cache_d_v))
    w_in_t = jnp.transpose(w_in, (2, 0, 1))
    xp, xs = x_prompt, x_sample
    rows_p, rows_s = [], []
    stacked = ()
    for l in range(depth):
        fb = jnp.zeros((SLAB, 1), F32).at[SLAB_F:SLAB_F + N_HEADS, 0].set(f_bias[l].astype(F32))
        lp = dict(g1=norm1_g[l].reshape(1, d), g2=norm2_g[l].reshape(1, d), w_t=_layout_w_in(w_in_t, l), fb=fb,
                  lng=lnv_g[l].reshape(BRANCH, 1), lnb=lnv_b[l].reshape(BRANCH, 1),
                  w_s=w_spatial[l], b_s=b_spatial[l], w_branch=w_branch[l].astype(BF16),
                  w_gate=w_gate[l].astype(BF16), b_gate=b_gate[l].reshape(1, -1), w_out=w_out[l].astype(BF16),
                  w_up=w_up[l].astype(BF16), w_conv=w_conv[l], b_conv=b_conv[l].reshape(1, -1),
                  w_down=w_down[l].astype(BF16))
        last = l == depth - 1
        xp, new_p, stacked = _layer(xp, mods[l, :bsz], lp, None, prefix, fg, stacked, layer=l, depth=depth,
                                    final_norm=last, past_len=0)
        xs, new_s, _ = _layer(xs, mods[l, bsz:], lp, past, state_ffn_conv[l], fg, None, layer=l, depth=depth,
                              final_norm=last, past_len=past_len)
        rows_p.append(new_p)
        rows_s.append(new_s)

    def stacked_s(i):
        return jnp.stack([r[i] for r in rows_s], axis=0)

    def heads_p(a):
        return jnp.transpose(a.reshape(depth, bsz, N_HEADS, HEAD_DIM, t_prompt), (0, 1, 4, 2, 3))

    sm_p = jnp.stack([r[0] for r in rows_p], axis=0)
    small_p = lambda r0, r1: jnp.swapaxes(sm_p[:, :, r0:r1, :], 2, 3)
    ka_p, va_p, kc_p, vc_p, kd_p, vd_p = (heads_p(a) for a in stacked)
    conv_p = jnp.stack([r[1] for r in rows_p], axis=0)
    return (xp, xs,
            ka_p, stacked_s(0),
            va_p, stacked_s(1),
            small_p(SLAB_KI, SLAB_KI + IDX_DIM), stacked_s(2),
            stacked_s(3),
            kc_p, stacked_s(4),
            vc_p, stacked_s(5),
            small_p(SLAB_F, SLAB_F + N_HEADS), stacked_s(6),
            kd_p, stacked_s(7),
            vd_p, stacked_s(8),
            conv_p, stacked_s(9))
```

```python
import functools

import jax
import jax.numpy as jnp
from jax import lax
from jax.experimental import pallas as pl
from jax.experimental.pallas import tpu as pltpu

F32 = jnp.float32
BF16 = jnp.bfloat16
I32 = jnp.int32
I16 = jnp.int16

HEAD_DIM = 64
N_HEADS = 4
BRANCH = N_HEADS * HEAD_DIM
IDX_HEADS = 8
IDX_DIM = 32
CHUNK = 64
TOPK_MAX = 256
B_CHUNK = 128
N_BRANCH = 4
CONV_W = 3
ROPE_THETA = 10000.0
EPS = 1e-6
LOG2E = 1.4426950408889634
Q_SCALE = HEAD_DIM ** -0.5 * LOG2E

SLAB = 128
SLAB_KI = 0
SLAB_W = 32
SLAB_F = 40
AUG = 128
F_ROWS = 8
F_BLK = 256
N_T_GROUPS = 10
NEG = -1e30
INT_MIN = -2 ** 31
I16_MIN = -2 ** 15
SB_CUTOFF = 220.0
SB_SATURATE = 100.0
Q_PAD = 128
SUB = 8
VMEM_LIMIT = 56 * 1024 * 1024


def _params(n_axes, vmem=VMEM_LIMIT):
    return pltpu.CompilerParams(dimension_semantics=("arbitrary",) * n_axes, vmem_limit_bytes=vmem)


def _dot(a, b):
    return jnp.dot(a, b, preferred_element_type=F32)


def _split2(x):
    hi = x.astype(BF16)
    lo = (x - hi.astype(F32)).astype(BF16)
    return hi, lo


def _split3(x):
    x1 = x.astype(BF16)
    r = x - x1.astype(F32)
    x2 = r.astype(BF16)
    x3 = (r - x2.astype(F32)).astype(BF16)
    return x1, x2, x3


def _softplus(z):
    return jnp.maximum(z, 0.0) + jnp.log1p(jnp.exp(-jnp.abs(z)))


def _rms(x, g):
    ms = jnp.mean(x * x, axis=-1, keepdims=True)
    return x * lax.rsqrt(ms + EPS) * g


def _rope_t(x, cos, sin_signed, half):
    rows = x.shape[0]
    row = lax.broadcasted_iota(I32, x.shape, 0)
    fwd = pltpu.roll(x, half, axis=0)
    bwd = pltpu.roll(x, rows - half, axis=0)
    rot = jnp.where((row % (2 * half)) < half, bwd, fwd)
    return x * cos + rot * sin_signed


def _mod_kernel(c_ref, w_ref, b_ref, o_ref):
    c = c_ref[...]
    a = c * jax.nn.sigmoid(c)
    a1, a2 = _split2(a)
    w1, w2 = _split2(w_ref[0])
    o_ref[0] = _dot(a1, w1) + _dot(a1, w2) + _dot(a2, w1) + b_ref[0]


def _modulation(c_all, w_mod, b_mod):
    depth, d, n = w_mod.shape
    rows = c_all.shape[0]
    tn = 1536
    return pl.pallas_call(
        _mod_kernel,
        grid=(depth, n // tn),
        in_specs=[pl.BlockSpec((rows, d), lambda l, j: (0, 0)),
                  pl.BlockSpec((1, d, tn), lambda l, j: (l, 0, j)),
                  pl.BlockSpec((1, 1, tn), lambda l, j: (l, 0, j))],
        out_specs=pl.BlockSpec((1, rows, tn), lambda l, j: (l, 0, j)),
        out_shape=jax.ShapeDtypeStruct((depth, rows, n), F32),
        compiler_params=_params(2),
        name="modulation",
    )(c_all, w_mod, b_mod.reshape(depth, 1, n))


def _proj_kernel(x_ref, mod_ref, g1_ref, wt_ref, cosa_ref, sina_ref, cosi_ref, sini_ref, coss_ref,
                 sins_ref, fb_ref, lng_ref, lnb_ref, ws_ref, bs_ref, *rest, nb, tt, n_alias):
    (ka_ref, va_ref, kc_ref, vc_ref, kd_ref, vd_ref,
     qa_ref, qc_ref, qd_ref, qi_ref, sm_ref, ob_ref, vb_ref) = rest[n_alias:]
    m = nb * tt
    d = x_ref.shape[-1]
    x = x_ref[...]
    mod = mod_ref[...]
    h = (_rms(x, g1_ref[...]) * (1.0 + mod[:, 1:2, :]) + mod[:, 0:1, :]).reshape(m, d)
    ht = h.T.astype(BF16)

    def col_t(j):
        return _dot(wt_ref[j * BRANCH:(j + 1) * BRANCH, :], ht)

    def put_t(ref, val):
        ref[...] = val.reshape(ref.shape).astype(ref.dtype)

    def put(ref, val):
        ref[...] = val.reshape(nb, tt, val.shape[-1]).astype(ref.dtype)

    cosa, sina = cosa_ref[...], sina_ref[...]
    put_t(qa_ref, _rope_t(col_t(0), cosa, sina, HEAD_DIM // 2) * Q_SCALE)
    put_t(ka_ref, _rope_t(col_t(1), cosa, sina, HEAD_DIM // 2))
    put_t(va_ref, col_t(2))
    put_t(qi_ref, _rope_t(col_t(3), cosi_ref[...], sini_ref[...], IDX_DIM // 2))
    put_t(qc_ref, col_t(4) * Q_SCALE)
    put_t(kc_ref, col_t(5))
    put_t(vc_ref, col_t(6))
    put_t(qd_ref, col_t(7) * Q_SCALE)
    put_t(kd_ref, col_t(8))
    put_t(vd_ref, col_t(9))
    sm = _dot(wt_ref[(N_T_GROUPS + 2) * BRANCH:(N_T_GROUPS + 2) * BRANCH + SLAB, :], ht)
    row = lax.broadcasted_iota(I32, sm.shape, 0)
    roped = _rope_t(sm, coss_ref[...], sins_ref[...], IDX_DIM // 2)
    logf = -_softplus(-(sm + fb_ref[...]))
    is_f = jnp.where(row >= SLAB_F, jnp.where(row < SLAB_F + N_HEADS, 1, 0), 0)
    put_t(sm_ref, jnp.where(is_f == 1, logf, roped))
    u = jax.nn.gelu(col_t(N_T_GROUPS))
    vg = jax.nn.gelu(col_t(N_T_GROUPS + 1))
    mu = jnp.mean(vg, axis=0, keepdims=True)
    vc_ = vg - mu
    var = jnp.mean(vc_ * vc_, axis=0, keepdims=True)
    vb = vc_ * lax.rsqrt(var + EPS) * lng_ref[...] + lnb_ref[...]
    put(vb_ref, vb.T)
    vbb = vb.astype(BF16)
    mixed = jnp.concatenate([_dot(vbb[g * HEAD_DIM:(g + 1) * HEAD_DIM, :], ws_ref[g]) for g in range(N_HEADS)],
                            axis=0)
    put(ob_ref, (u * (mixed + bs_ref[...])).T)


def _proj(x, mod, g1, w_t, tables, fb, lng, lnb, ws, bs, *, nb, tt, stacked, layer, depth):
    bsz, t, d = x.shape
    m = nb * tt
    n_t = t // tt
    tok = lambda w: pl.BlockSpec((nb, tt, w), lambda ti, bi: (bi, ti, 0))
    const = lambda a: pl.BlockSpec(a.shape, lambda ti, bi: (0,) * a.ndim)
    tab = lambda c: pl.BlockSpec((c, m), lambda ti, bi: (0, ti))
    if stacked is None:
        assert n_t == 1 and nb == bsz
        kv_spec = pl.BlockSpec((BRANCH, m), lambda ti, bi: (0, 0))
        kv_shape = jax.ShapeDtypeStruct((BRANCH, m), F32)
        q_spec = lambda c: pl.BlockSpec((c, m), lambda ti, bi: (0, 0))
        q_shape = lambda c, dt: jax.ShapeDtypeStruct((c, m), dt)
        alias_in = ()
    else:
        assert nb == 1
        kv_spec = pl.BlockSpec((1, 1, BRANCH, tt), lambda ti, bi: (layer, bi, 0, ti))
        kv_shape = jax.ShapeDtypeStruct((depth, bsz, BRANCH, t), F32)
        q_spec = lambda c: pl.BlockSpec((1, c, tt), lambda ti, bi: (bi, 0, ti))
        q_shape = lambda c, dt: jax.ShapeDtypeStruct((bsz, c, t), dt)
        alias_in = tuple(stacked)
    n_alias = len(alias_in)
    n_in = 15
    cosa, sina, cosi, sini, coss, sins = tables
    return pl.pallas_call(
        functools.partial(_proj_kernel, nb=nb, tt=tt, n_alias=n_alias),
        grid=(n_t, bsz // nb),
        in_specs=[tok(d), pl.BlockSpec((nb, 6, d), lambda ti, bi: (bi, 0, 0)), const(g1), const(w_t),
                  tab(BRANCH), tab(BRANCH), tab(BRANCH), tab(BRANCH), tab(SLAB), tab(SLAB),
                  const(fb), const(lng), const(lnb), const(ws), const(bs)]
                 + [pl.BlockSpec(memory_space=pl.ANY)] * n_alias,
        out_specs=[kv_spec] * 6 + [q_spec(BRANCH)] * 4 + [q_spec(SLAB), tok(BRANCH), tok(BRANCH)],
        out_shape=[kv_shape] * 6 + [q_shape(BRANCH, BF16)] * 3 + [q_shape(BRANCH, F32), q_shape(SLAB, F32),
                                                                 jax.ShapeDtypeStruct((bsz, t, BRANCH), BF16),
                                                                 jax.ShapeDtypeStruct((bsz, t, BRANCH), F32)],
        input_output_aliases={n_in + i: i for i in range(n_alias)},
        compiler_params=_params(2),
        name="proj",
    )(x, mod, g1, w_t, cosa, sina, cosi, sini, coss, sins, fb, lng, lnb, ws, bs, *alias_in)


def _load2d(ref):
    return ref[(0,) * (len(ref.shape) - 2)]


def _fill_token_major(dst_sc, srcs):
    off = 0
    for s in srcs:
        n = s.shape[1]
        dst_sc[off:off + n, :] = s.T.astype(BF16)
        off += n
    lp = dst_sc.shape[0]
    if off < lp:
        dst_sc[off:lp, :] = jnp.zeros((lp - off, dst_sc.shape[1]), BF16)


def _fill_channel_major(dst_sc, srcs):
    off = 0
    for s in srcs:
        n = s.shape[1]
        dst_sc[:, off:off + n] = s.astype(BF16)
        off += n
    lp = dst_sc.shape[1]
    if off < lp:
        dst_sc[:, off:lp] = jnp.zeros((dst_sc.shape[0], lp - off), BF16)


def _flash_init(tq):
    return tuple((jnp.full((1, tq), NEG, F32), jnp.zeros((1, tq), F32), jnp.zeros((HEAD_DIM, tq), F32))
                 for _ in range(N_HEADS))


def _flash_step(carry, score_fns, vt_sc, off, tk):
    scores = [fn() for fn in score_fns]
    ms = [jnp.maximum(carry[h][0], jnp.max(scores[h], axis=0, keepdims=True)) for h in range(N_HEADS)]
    ps = [jnp.exp2(scores[h] - ms[h]) for h in range(N_HEADS)]
    pvs = [_dot(vt_sc[h * HEAD_DIM:(h + 1) * HEAD_DIM, pl.ds(off, tk)], ps[h].astype(BF16))
           for h in range(N_HEADS)]
    new = []
    for h in range(N_HEADS):
        m_run, l_run, acc = carry[h]
        alpha = jnp.exp2(m_run - ms[h])
        new.append((ms[h], alpha * l_run + jnp.sum(ps[h], axis=0, keepdims=True), alpha * acc + pvs[h]))
    return tuple(new)


def _flash_finish(carry):
    out_t = jnp.concatenate([acc / l_run for (_, l_run, acc) in carry], axis=0)
    return out_t.T


def _tree_sum(terms):
    while len(terms) > 1:
        terms = [a + b for a, b in zip(terms[::2], terms[1::2])] + ([terms[-1]] if len(terms) % 2 else [])
    return terms[0]


def _masked_heads_t(q_t):
    head = lax.broadcasted_iota(I32, q_t.shape, 0) // HEAD_DIM
    return [jnp.where(head == h, q_t, jnp.zeros_like(q_t)) for h in range(N_HEADS)]


def _src_spec(arr, layer):
    if arr.ndim == 4:
        return pl.BlockSpec((1, 1) + arr.shape[2:], lambda b, i: (layer, b, 0, 0))
    return pl.BlockSpec((1,) + arr.shape[1:], lambda b, i: (b, 0, 0))


def _q_spec(c, tq):
    return pl.BlockSpec((1, c, tq), lambda b, i: (b, 0, i))


def _o_spec(tq):
    return pl.BlockSpec((1, tq, BRANCH), lambda b, i: (b, i, 0))


def _dsa_kernel(qa_ref, qi_ref, qsm_ref, *refs, n_src, tq, tk, past_len, l_valid, n_sel, t_valid, single_q):
    k_refs, v_refs, ki_refs = refs[:n_src], refs[n_src:2 * n_src], refs[2 * n_src:3 * n_src]
    o_ref, kb_sc, vt_sc, kic_sc, lhs_sc, key_sc, hi_sc, lo_sc, thr_sc, room_sc, tie_sc = refs[3 * n_src:]
    qb = pl.program_id(1)

    @pl.when(qb == 0)
    def _():
        _fill_token_major(kb_sc, [_load2d(r) for r in k_refs])
        _fill_channel_major(vt_sc, [_load2d(r) for r in v_refs])
        cats = []
        for r in ki_refs:
            ki = _load2d(r)[SLAB_KI:SLAB_KI + IDX_DIM, :]
            hi = ki.astype(BF16).astype(F32)
            cats.append(jnp.concatenate([hi, ki - hi, hi, jnp.zeros_like(hi)], axis=0))
        _fill_token_major(kic_sc, cats)

    qi = qi_ref[0]
    for h in range(IDX_HEADS):
        piece = qi[h * IDX_DIM:(h + 1) * IDX_DIM, :]
        hi = piece.astype(BF16).astype(F32)
        lhs_sc[h] = jnp.concatenate([hi, hi, piece - hi, jnp.zeros_like(hi)], axis=0).astype(BF16)
    w_t = qsm_ref[0]

    qlane = lax.broadcasted_iota(I32, (1, tq), 1)
    if single_q:
        row0 = past_len
        top_limit = min(((row0 + tq - 1) // CHUNK + 1) * CHUNK, l_valid)
        count_loop = functools.partial(lax.fori_loop, unroll=True)
        block_loop = functools.partial(lax.fori_loop, unroll=2)
    else:
        row0 = past_len + qb * tq
        top_limit = jnp.minimum(((row0 + tq - 1) // CHUNK + 1) * CHUNK, l_valid)
        count_loop = block_loop = lax.fori_loop
    pos = row0 + qlane
    limit = jnp.minimum((pos // CHUNK + 1) * CHUNK, l_valid)
    nblk = (top_limit + tk - 1) // tk

    def score_blk(j, _):
        off = pl.multiple_of(j * tk, tk)
        kc = kic_sc[pl.ds(off, tk), :]
        dots = [_dot(kc, lhs_sc[h]) for h in range(IDX_HEADS)]
        acc = jnp.zeros((tk, tq), F32)
        for h in range(IDX_HEADS):
            acc = acc + w_t[SLAB_W + h:SLAB_W + h + 1, :] * jnp.maximum(dots[h], 0.0)
        acc = jnp.where(acc == 0.0, 0.0, acc)
        bits = pltpu.bitcast(acc, I32)
        key = jnp.where(bits < 0, bits ^ 0x7FFFFFFF, bits)
        kidx = off + lax.broadcasted_iota(I32, (tk, tq), 0)
        key = jnp.where(kidx < limit, key, INT_MIN)
        key_sc[pl.ds(off, tk), :] = key
        hi_sc[pl.ds(off, tk), :] = (key >> 16).astype(I16)
        lo_sc[pl.ds(off, tk), :] = ((key & 0xFFFF) + I16_MIN).astype(I16)
        return 0

    block_loop(0, nblk, score_blk, 0)

    def count16(ref, cand):
        cand_b = jnp.broadcast_to(cand, (16, tq)).astype(I16)

        def body(j, part):
            off = pl.multiple_of(j * tk, tk)
            kb = ref[pl.ds(off, tk), :]
            return part + _tree_sum([jnp.where(kb[c * 16:(c + 1) * 16, :] >= cand_b, jnp.int16(1), jnp.int16(0))
                                     for c in range(tk // 16)])

        part = count_loop(0, nblk, body, jnp.zeros((16, tq), I16))
        return jnp.sum(part.astype(F32), axis=0, keepdims=True)

    def count32(cand):
        cand_b = jnp.broadcast_to(cand, (8, tq))

        def body(j, part):
            off = pl.multiple_of(j * tk, tk)
            kb = key_sc[pl.ds(off, tk), :]
            return part + _tree_sum([jnp.where(kb[c * 8:(c + 1) * 8, :] >= cand_b, 1.0, 0.0)
                                     for c in range(tk // 8)])

        part = count_loop(0, nblk, body, jnp.zeros((8, tq), F32))
        return jnp.sum(part, axis=0, keepdims=True)

    def kth16(ref, want):
        t = jnp.where(count16(ref, jnp.zeros((1, tq), I32)) >= want, 0, I16_MIN).astype(I32)

        def search(i, t):
            cand = t | jnp.left_shift(jnp.int32(1), 14 - i)
            return jnp.where(count16(ref, cand) >= want, cand, t)

        return lax.fori_loop(0, 15, search, t)

    kf = float(n_sel)
    thr_sc[...] = jnp.full(thr_sc.shape, INT_MIN + 1, I32)
    room_sc[...] = jnp.zeros(room_sc.shape, F32)
    tie_sc[...] = jnp.zeros(tie_sc.shape, F32)

    @pl.when(top_limit > n_sel)
    def _():
        t_hi = kth16(hi_sc, kf)
        above = jnp.where(t_hi >= -I16_MIN - 1, 0.0, count16(hi_sc, t_hi + 1))
        t_hi_b = jnp.broadcast_to(t_hi, (16, tq)).astype(I16)

        def keep_equal_hi(j, _):
            off = pl.multiple_of(j * tk, tk)
            hi_blk = hi_sc[pl.ds(off, tk), :]
            lo_blk = lo_sc[pl.ds(off, tk), :]
            kept = [jnp.where(hi_blk[c * 16:(c + 1) * 16, :] == t_hi_b, lo_blk[c * 16:(c + 1) * 16, :],
                              jnp.int16(I16_MIN)) for c in range(tk // 16)]
            lo_sc[pl.ds(off, tk), :] = jnp.concatenate(kept, axis=0)
            return 0

        count_loop(0, nblk, keep_equal_hi, 0)
        t_lo = kth16(lo_sc, kf - above)
        t_full = jnp.maximum(t_hi * 65536 + (t_lo - I16_MIN), INT_MIN + 1)
        cnt_ge = count32(t_full)
        cnt_gt = count32(t_full + 1)
        real_q = qlane < (t_valid - qb * tq)
        thr_sc[...] = jnp.broadcast_to(t_full, thr_sc.shape)
        room_sc[...] = jnp.broadcast_to(kf - cnt_gt, room_sc.shape)
        tie_sc[...] = jnp.broadcast_to(jnp.where(real_q, jnp.where(cnt_ge > kf, 1.0, 0.0), 0.0), tie_sc.shape)

    thr = thr_sc[0:1, :]
    room = room_sc[0:1, :]
    any_tie = jnp.max(tie_sc[0:1, :]) > 0.0

    @pl.when(any_tie)
    def _():
        ri = lax.broadcasted_iota(I32, (tk, tk), 0)
        ci = lax.broadcasted_iota(I32, (tk, tk), 1)
        upto = jnp.where(ci <= ri, 1.0, 0.0).astype(BF16)

        def drop_late_ties(j, seen):
            off = pl.multiple_of(j * tk, tk)
            kb = key_sc[pl.ds(off, tk), :]
            eq = jnp.where(kb == thr, 1.0, 0.0)
            rank = _dot(upto, eq.astype(BF16)) + seen
            key_sc[pl.ds(off, tk), :] = jnp.where(eq * rank > room, INT_MIN, kb)
            return seen + jnp.sum(eq, axis=0, keepdims=True)

        lax.fori_loop(0, nblk, drop_late_ties, jnp.zeros((1, tq), F32))

    qh = _masked_heads_t(qa_ref[0])

    def attend(j, carry):
        off = pl.multiple_of(j * tk, tk)
        kblk = kb_sc[pl.ds(off, tk), :]
        keep = key_sc[pl.ds(off, tk), :] >= thr
        score_fns = [functools.partial(lambda h: jnp.where(keep, _dot(kblk, qh[h]), NEG), h) for h in range(N_HEADS)]
        return _flash_step(carry, score_fns, vt_sc, off, tk)

    carry = block_loop(0, nblk, attend, _flash_init(tq))
    o_ref[0] = _flash_finish(carry).astype(o_ref.dtype)


def _dsa(qa, qi, qsm, k_srcs, v_srcs, ki_srcs, *, layer, lp, tq, tk, past_len, l_valid, n_sel, t_valid):
    bsz, _, t = qa.shape
    n_src = len(k_srcs)
    srcs = list(k_srcs) + list(v_srcs) + list(ki_srcs)
    return pl.pallas_call(
        functools.partial(_dsa_kernel, n_src=n_src, tq=tq, tk=tk, past_len=past_len, l_valid=l_valid,
                          n_sel=n_sel, t_valid=t_valid, single_q=(t == tq)),
        grid=(bsz, t // tq),
        in_specs=[_q_spec(BRANCH, tq), _q_spec(BRANCH, tq), _q_spec(SLAB, tq)] + [_src_spec(a, layer) for a in srcs],
        out_specs=_o_spec(tq),
        out_shape=jax.ShapeDtypeStruct((bsz, t, BRANCH), BF16),
        scratch_shapes=[pltpu.VMEM((lp, BRANCH), BF16), pltpu.VMEM((BRANCH, lp), BF16),
                        pltpu.VMEM((lp, SLAB), BF16), pltpu.VMEM((IDX_HEADS, SLAB, tq), BF16),
                        pltpu.VMEM((lp, tq), I32), pltpu.VMEM((lp, tq), I16), pltpu.VMEM((lp, tq), I16),
                        pltpu.VMEM((SUB, tq), I32), pltpu.VMEM((SUB, tq), F32), pltpu.VMEM((SUB, tq), F32)],
        compiler_params=_params(2),
        name="dsa",
    )(qa, qi, qsm, *srcs)


def _fox_kernel(q_ref, *refs, n_src, f_row, tq, tk, past_len, single_q):
    k_refs, v_refs, f_refs = refs[:n_src], refs[n_src:2 * n_src], refs[2 * n_src:3 * n_src]
    o_ref, vt_sc, kaug_sc, f_sc = refs[3 * n_src:]
    qb = pl.program_id(1)
    lp = f_sc.shape[1]

    @pl.when(qb == 0)
    def _():
        _fill_channel_major(vt_sc, [_load2d(r) for r in v_refs])
        ks = [_load2d(r) for r in k_refs]
        fs = []
        for r, row in zip(f_refs, f_row):
            logf = _load2d(r)
            if row + F_ROWS <= logf.shape[0]:
                fs.append(logf[row:row + F_ROWS, :])
            else:
                gates = logf[row:row + N_HEADS, :]
                fs.append(jnp.concatenate([gates, jnp.zeros((F_ROWS - N_HEADS, gates.shape[1]), F32)], axis=0))
        kt = ks[0] if len(ks) == 1 else jnp.concatenate(ks, axis=1)
        ft = fs[0] if len(fs) == 1 else jnp.concatenate(fs, axis=1)
        n_real = kt.shape[1]
        ri = lax.broadcasted_iota(I32, (F_BLK, F_BLK), 0)
        ci = lax.broadcasted_iota(I32, (F_BLK, F_BLK), 1)
        upto = jnp.where(ri <= ci, 1.0, 0.0).astype(BF16)
        rowx = lax.broadcasted_iota(I32, (HEAD_DIM, F_BLK), 0)
        run = jnp.zeros((F_ROWS, 1), F32)
        for b in range(lp // F_BLK):
            lo, hi_ = b * F_BLK, min((b + 1) * F_BLK, n_real)
            if hi_ <= lo:
                kaug_sc[lo:lo + F_BLK, :] = jnp.zeros((F_BLK, N_HEADS * AUG), BF16)
                f_sc[:, lo:lo + F_BLK] = jnp.broadcast_to(run * LOG2E, (F_ROWS, F_BLK))
                continue
            k_blk, f_blk = kt[:, lo:hi_], ft[:, lo:hi_]
            if hi_ - lo < F_BLK:
                k_blk = jnp.concatenate([k_blk, jnp.zeros((BRANCH, F_BLK - (hi_ - lo)), F32)], axis=1)
                f_blk = jnp.concatenate([f_blk, jnp.zeros((F_ROWS, F_BLK - (hi_ - lo)), F32)], axis=1)
            s1, s2, s3 = _split3(f_blk)
            fsum = _dot(s1, upto) + _dot(s2, upto) + _dot(s3, upto) + run
            run = fsum[:, F_BLK - 1:F_BLK]
            fsum = fsum * LOG2E
            f_sc[:, lo:lo + F_BLK] = fsum
            f1, f2, f3 = (p.astype(F32) for p in _split3(fsum))
            for h in range(N_HEADS):
                extra = jnp.where(rowx < 3, 1.0,
                                  jnp.where(rowx == 3, -f1[h:h + 1, :],
                                            jnp.where(rowx == 4, -f2[h:h + 1, :],
                                                      jnp.where(rowx == 5, -f3[h:h + 1, :], 0.0))))
                aug_t = jnp.concatenate([k_blk[h * HEAD_DIM:(h + 1) * HEAD_DIM, :], extra], axis=0)
                kaug_sc[lo:lo + F_BLK, h * AUG:(h + 1) * AUG] = aug_t.T.astype(BF16)

    if single_q:
        row0 = past_len
        f_q = f_sc[:, row0:row0 + tq]
        block_loop = functools.partial(lax.fori_loop, unroll=2)
    else:
        row0 = past_len + qb * tq
        f_q = f_sc[:, pl.ds(pl.multiple_of(row0, Q_PAD), tq)]
        block_loop = lax.fori_loop
    pos = row0 + lax.broadcasted_iota(I32, (1, tq), 1)
    n_full = row0 // tk
    n_all = (row0 + tq + tk - 1) // tk
    q_t = q_ref[0].astype(F32)
    f1, f2, f3 = (p.astype(F32) for p in _split3(f_q))
    rowq = lax.broadcasted_iota(I32, (HEAD_DIM, tq), 0)
    qh = []
    for h in range(N_HEADS):
        extra = jnp.where(rowq == 0, f1[h:h + 1, :],
                          jnp.where(rowq == 1, f2[h:h + 1, :],
                                    jnp.where(rowq == 2, f3[h:h + 1, :], jnp.where(rowq < 6, 1.0, 0.0))))
        qh.append(jnp.concatenate([q_t[h * HEAD_DIM:(h + 1) * HEAD_DIM, :], extra], axis=0).astype(BF16))

    def step(j, carry, masked):
        off = pl.multiple_of(j * tk, tk)
        causal = (off + lax.broadcasted_iota(I32, (tk, tq), 0) <= pos) if masked else None

        def score(h):
            s = _dot(kaug_sc[pl.ds(off, tk), h * AUG:(h + 1) * AUG], qh[h])
            return jnp.where(causal, s, NEG) if masked else s

        return _flash_step(carry, [functools.partial(score, h) for h in range(N_HEADS)], vt_sc, off, tk)

    carry = block_loop(0, n_full, lambda j, c: step(j, c, False), _flash_init(tq))
    carry = lax.fori_loop(n_full, n_all, lambda j, c: step(j, c, True), carry)
    o_ref[0] = _flash_finish(carry).astype(o_ref.dtype)


def _fox(q, k_srcs, v_srcs, f_srcs, f_row, *, layer, lp, tq, tk, past_len):
    bsz, _, t = q.shape
    n_src = len(k_srcs)
    srcs = list(k_srcs) + list(v_srcs) + list(f_srcs)
    return pl.pallas_call(
        functools.partial(_fox_kernel, n_src=n_src, f_row=tuple(f_row), tq=tq, tk=tk, past_len=past_len,
                          single_q=(t == tq)),
        grid=(bsz, t // tq),
        in_specs=[_q_spec(BRANCH, tq)] + [_src_spec(a, layer) for a in srcs],
        out_specs=_o_spec(tq),
        out_shape=jax.ShapeDtypeStruct((bsz, t, BRANCH), BF16),
        scratch_shapes=[pltpu.VMEM((BRANCH, lp), BF16), pltpu.VMEM((lp, N_HEADS * AUG), BF16),
                        pltpu.VMEM((F_ROWS, lp), F32)],
        compiler_params=_params(2),
        name="fox",
    )(q, *srcs)


def _sb_kernel(q_ref, *refs, n_src, tq, tk, past_len):
    k_refs, v_refs = refs[:n_src], refs[n_src:2 * n_src]
    o_ref, kb_sc, vt_sc = refs[2 * n_src:]
    qb = pl.program_id(1)

    @pl.when(qb == 0)
    def _():
        _fill_token_major(kb_sc, [_load2d(r) for r in k_refs])
        _fill_channel_major(vt_sc, [_load2d(r) for r in v_refs])

    qh = _masked_heads_t(q_ref[0])
    row0 = past_len + qb * tq
    pos = row0 + lax.broadcasted_iota(I32, (1, tq), 1)
    j_top = (row0 + tq - 1) // tk
    ri = lax.broadcasted_iota(I32, (tk, tk), 0)
    ci = lax.broadcasted_iota(I32, (tk, tk), 1)
    after = jnp.where(ci > ri, 1.0, 0.0).astype(BF16)

    def cond(carry):
        j, state = carry
        live = state[0][0]
        for h in range(1, N_HEADS):
            live = jnp.maximum(live, state[h][0])
        return jnp.logical_and(j >= 0, jnp.max(live) > -SB_CUTOFF)

    def body(carry, masked=False):
        j, state = carry
        off = pl.multiple_of(j * tk, tk)
        kblk = kb_sc[pl.ds(off, tk), :]
        zs = [_dot(kblk, qh[h]) for h in range(N_HEADS)]
        keeps = [-jnp.maximum(jnp.log2(1.0 + jnp.exp2(jnp.minimum(z, SB_SATURATE))), z) for z in zs]
        if masked:
            strict = off + lax.broadcasted_iota(I32, (tk, tq), 0) < pos
            keeps = [jnp.where(strict, kp, 0.0) for kp in keeps]
        laters = []
        for h in range(N_HEADS):
            k_hi, k_lo = _split2(keeps[h])
            laters.append(_dot(after, k_hi) + _dot(after, k_lo) + state[h][0])
        ws = [jnp.exp2(zs[h] + keeps[h] + laters[h]) for h in range(N_HEADS)]
        if masked:
            ws = [jnp.where(strict, w, 0.0) for w in ws]
        pvs = [_dot(vt_sc[h * HEAD_DIM:(h + 1) * HEAD_DIM, pl.ds(off, tk)], ws[h].astype(BF16))
               for h in range(N_HEADS)]
        new = tuple((state[h][0] + jnp.sum(keeps[h], axis=0, keepdims=True), state[h][1] + pvs[h])
                    for h in range(N_HEADS))
        return j - 1, new

    init = tuple((jnp.zeros((1, tq), F32), jnp.zeros((HEAD_DIM, tq), F32)) for _ in range(N_HEADS))
    first = body((j_top, init), masked=True)
    _, state = lax.while_loop(cond, body, first)
    out_t = jnp.concatenate([acc for (_, acc) in state], axis=0)
    o_ref[0] = out_t.T.astype(o_ref.dtype)


def _sb(q, k_srcs, v_srcs, *, layer, lp, tq, tk, past_len):
    bsz, _, t = q.shape
    assert tk % tq == 0 and past_len % tq == 0
    n_src = len(k_srcs)
    srcs = list(k_srcs) + list(v_srcs)
    return pl.pallas_call(
        functools.partial(_sb_kernel, n_src=n_src, tq=tq, tk=tk, past_len=past_len),
        grid=(bsz, t // tq),
        in_specs=[_q_spec(BRANCH, tq)] + [_src_spec(a, layer) for a in srcs],
        out_specs=_o_spec(tq),
        out_shape=jax.ShapeDtypeStruct((bsz, t, BRANCH), BF16),
        scratch_shapes=[pltpu.VMEM((lp, BRANCH), BF16), pltpu.VMEM((BRANCH, lp), BF16)],
        compiler_params=_params(2),
        name="sb",
    )(q, *srcs)


def _merge_kernel(x_ref, mod_ref, g1_ref, oa_ref, ob_ref, oc_ref, od_ref, wg_ref, bg_ref, wb_ref, wo_ref,
                  xo_ref, *, nb, tt):
    m = nb * tt
    d = x_ref.shape[-1]
    x = x_ref[...]
    mod = mod_ref[...]
    h = _rms(x, g1_ref[...]) * (1.0 + mod[:, 1:2, :]) + mod[:, 0:1, :]
    hb = h.reshape(m, d).astype(BF16)
    o_refs = (oa_ref, ob_ref, oc_ref, od_ref)

    def pre(i):
        return (_dot(hb, wg_ref[:, i * d:(i + 1) * d]), _dot(o_refs[i][...].reshape(m, BRANCH), wb_ref[i]))

    merged = None
    cur = pre(0)
    for i in range(N_BRANCH):
        nxt = pre(i + 1) if i + 1 < N_BRANCH else None
        term = jax.nn.sigmoid(cur[0] + bg_ref[:, i * d:(i + 1) * d]) * cur[1]
        merged = term if merged is None else merged + term
        cur = nxt
    y = _dot(merged.astype(BF16), wo_ref[...])
    xo_ref[...] = x + mod[:, 2:3, :] * y.reshape(nb, tt, d)


def _merge(x, mod, g1, oa, ob, oc, od, w_gate, b_gate, w_branch, w_out, *, nb, tt):
    bsz, t, d = x.shape
    tok = lambda w: pl.BlockSpec((nb, tt, w), lambda bi, ti: (bi, ti, 0))
    const = lambda a: pl.BlockSpec(a.shape, lambda bi, ti: (0,) * a.ndim)
    return pl.pallas_call(
        functools.partial(_merge_kernel, nb=nb, tt=tt),
        grid=(bsz // nb, t // tt),
        in_specs=[tok(d), pl.BlockSpec((nb, 6, d), lambda bi, ti: (bi, 0, 0)), const(g1),
                  tok(BRANCH), tok(BRANCH), tok(BRANCH), tok(BRANCH),
                  const(w_gate), const(b_gate), const(w_branch), const(w_out)],
        out_specs=tok(d),
        out_shape=jax.ShapeDtypeStruct((bsz, t, d), F32),
        compiler_params=_params(2),
        name="merge",
    )(x, mod, g1, oa, ob, oc, od, w_gate, b_gate, w_branch, w_out)


def _ffn_kernel(x_ref, mod_ref, g2_ref, pfx_ref, wu_ref, wc_ref, bc_ref, wd_ref, fg_ref,
                xo_ref, st_ref, carry_sc, ext_sc, act_sc, *, nb, tt, cw, final_norm):
    m = nb * tt
    d = x_ref.shape[-1]
    dff = wd_ref.shape[0]
    ti = pl.program_id(1)
    keep = CONV_W - 1

    @pl.when(ti == 0)
    def _():
        carry_sc[...] = jnp.zeros_like(carry_sc)
        carry_sc[:, SUB - keep:, :] = pfx_ref[...]

    x = x_ref[...]
    mod = mod_ref[...]
    h = _rms(x, g2_ref[...]) * (1.0 + mod[:, 4:5, :]) + mod[:, 3:4, :]
    hb = h.reshape(m, d).astype(BF16)

    def up_cols(c0):
        return _dot(hb, wu_ref[:, c0:c0 + cw]).reshape(nb, tt, cw)

    def stage(up, c0, slot):
        ext_sc[slot, :, 0:SUB, :] = carry_sc[:, :, c0:c0 + cw]
        ext_sc[slot, :, SUB:, :] = up
        carry_sc[:, :, c0:c0 + cw] = up[:, tt - SUB:, :]

    def conv(c0, slot):
        wc = wc_ref[:, c0:c0 + cw]
        win = lambda back: ext_sc[slot, :, SUB - back:SUB - back + tt, :]
        out = win(2) * wc[0:1, :] + win(1) * wc[1:2, :] + win(0) * wc[2:3, :] + bc_ref[:, c0:c0 + cw]
        return out.reshape(m, cw)

    n_chunks = dff // cw
    split = (n_chunks + 1) // 2 * cw
    ups = (up_cols(0), up_cols(dff))
    acc = None
    for j in range(n_chunks):
        nxt = (up_cols((j + 1) * cw), up_cols(dff + (j + 1) * cw)) if j + 1 < n_chunks else None
        slot = 2 * (j % 2)
        stage(ups[0], j * cw, slot)
        stage(ups[1], dff + j * cw, slot + 1)
        gate = conv(j * cw, slot)
        val = conv(dff + j * cw, slot + 1)
        act_sc[:, j * cw:(j + 1) * cw] = (gate * jax.nn.sigmoid(gate) * val).astype(BF16)
        if (j + 1) * cw == split:
            acc = _dot(act_sc[:, :split], wd_ref[:split, :])
        ups = nxt
    acc = acc + _dot(act_sc[:, split:], wd_ref[split:, :])
    xo = x + mod[:, 5:6, :] * acc.reshape(nb, tt, d)
    if final_norm:
        xo = _rms(xo, fg_ref[...])
    xo_ref[...] = xo

    @pl.when(ti == pl.num_programs(1) - 1)
    def _():
        st_ref[...] = carry_sc[:, SUB - keep:, :]


def _ffn(x, mod, g2, prefix, w_up, w_conv, b_conv, w_down, final_g, *, nb, tt, final_norm):
    bsz, t, d = x.shape
    dff = w_down.shape[0]
    cw = 256
    tok = pl.BlockSpec((nb, tt, d), lambda bi, ti: (bi, ti, 0))
    const = lambda a: pl.BlockSpec(a.shape, lambda bi, ti: (0,) * a.ndim)
    state = pl.BlockSpec((nb, CONV_W - 1, 2 * dff), lambda bi, ti: (bi, 0, 0))
    return pl.pallas_call(
        functools.partial(_ffn_kernel, nb=nb, tt=tt, cw=cw, final_norm=final_norm),
        grid=(bsz // nb, t // tt),
        in_specs=[tok, pl.BlockSpec((nb, 6, d), lambda bi, ti: (bi, 0, 0)), const(g2), state,
                  const(w_up), const(w_conv), const(b_conv), const(w_down), const(final_g)],
        out_specs=[tok, state],
        out_shape=[jax.ShapeDtypeStruct((bsz, t, d), F32),
                   jax.ShapeDtypeStruct((bsz, CONV_W - 1, 2 * dff), F32)],
        scratch_shapes=[pltpu.VMEM((nb, SUB, 2 * dff), F32), pltpu.VMEM((4, nb, tt + SUB, cw), F32),
                        pltpu.VMEM((nb * tt, dff), BF16)],
        compiler_params=_params(2),
        name="ffn",
    )(x, mod, g2, prefix, w_up, w_conv, b_conv, w_down, final_g)


def _rope_tables_t(pos):
    def tab(half, heads, pad):
        inv = ROPE_THETA ** (-jnp.arange(half, dtype=F32) / half)
        ang = inv[:, None] * pos.astype(F32)[None, :]
        cos, sin = jnp.cos(ang), jnp.sin(ang)
        cos_t = jnp.tile(jnp.concatenate([cos, cos], axis=0), (heads, 1))
        sin_t = jnp.tile(jnp.concatenate([-sin, sin], axis=0), (heads, 1))
        if pad:
            cos_t = jnp.concatenate([cos_t, jnp.ones((pad, pos.shape[0]), F32)], axis=0)
            sin_t = jnp.concatenate([sin_t, jnp.zeros((pad, pos.shape[0]), F32)], axis=0)
        return cos_t, sin_t

    cosa, sina = tab(HEAD_DIM // 2, N_HEADS, 0)
    cosi, sini = tab(IDX_DIM // 2, IDX_HEADS, 0)
    coss, sins = tab(IDX_DIM // 2, 1, SLAB - IDX_DIM)
    return cosa, sina, cosi, sini, coss, sins


def _layout_w_in(w_in_t, layer):
    sizes = (BRANCH, BRANCH, BRANCH, IDX_HEADS * IDX_DIM, IDX_DIM, IDX_HEADS, BRANCH, BRANCH,
             BRANCH, BRANCH, BRANCH, N_HEADS, BRANCH, BRANCH, BRANCH)
    offs = [0]
    for s in sizes:
        offs.append(offs[-1] + s)
    piece = lambda i: w_in_t[offs[i]:offs[i + 1], layer, :]
    a_q, a_k, a_v, a_qi, a_ki, a_w, b_u, b_v, c_q, c_k, c_v, c_f, d_q, d_k, d_v = (piece(i) for i in range(15))
    pad = jnp.zeros((SLAB - IDX_DIM - IDX_HEADS - N_HEADS, w_in_t.shape[-1]), w_in_t.dtype)
    return jnp.concatenate([a_q, a_k, a_v, a_qi, c_q, c_k, c_v, d_q, d_k, d_v, b_u, b_v,
                            a_ki, a_w, c_f, pad], axis=0).astype(BF16)


def _block_diag_tril_t(w_s, n, reps):
    tri = jnp.tril(jnp.ones((n, n), dtype=bool))
    w = jnp.where(tri[None], w_s[:, :n, :n], 0)
    eye = jnp.eye(reps, dtype=w.dtype)
    return jnp.einsum("ab,gts->gbsat", eye, w).reshape(w.shape[0], reps * n, reps * n).astype(BF16)


def _round_up(a, b):
    return (a + b - 1) // b * b


def _channel_major(cache):
    depth, bsz, length = cache.shape[:3]
    flat = cache.reshape(depth, bsz, length, -1)
    return jnp.swapaxes(flat, 2, 3)


def _layer(x, mod, lp, past, prefix, final_g, stacked, *, layer, depth, final_norm, past_len):
    bsz, t, d = x.shape
    is_prompt = past is None
    n = min(t, B_CHUNK)
    if is_prompt:
        nb, tt = 1, min(t, 512)
        reps = tt // n
        pos_rows = jnp.arange(t)
    else:
        nb, tt = bsz, t
        reps = (nb * tt) // n
        pos_rows = jnp.tile(past_len + jnp.arange(t), nb)
    m = nb * tt
    tables = _rope_tables_t(pos_rows)
    ws = _block_diag_tril_t(lp["w_s"], n, reps)
    bs = jnp.tile(jnp.repeat(lp["b_s"][:, :n], HEAD_DIM, axis=0), (1, reps))
    fb = jnp.broadcast_to(lp["fb"], (SLAB, m))
    lng = jnp.broadcast_to(lp["lng"], (BRANCH, m))
    lnb = jnp.broadcast_to(lp["lnb"], (BRANCH, m))
    outs = _proj(x, mod, lp["g1"], lp["w_t"], tables, fb, lng, lnb, ws, bs,
                 nb=nb, tt=tt, stacked=stacked if is_prompt else None, layer=layer, depth=depth)
    kv, (qa, qc, qd, qi, sm, ob, vb) = outs[:6], outs[6:]

    l_valid = past_len + t
    tqp = _round_up(t, Q_PAD)
    if is_prompt:
        tk = min(512, t)
        tk_d = min(256, t)
        lpad = t
        tq = min(256, tqp)
        n_sel = min(TOPK_MAX, t // 4)
        ka, va, kc, vc, kd, vd = ([a] for a in kv)
        ki_srcs, f_srcs, f_row = [sm], [sm], (SLAB_F,)
        att_layer = layer
    else:
        tk = tk_d = 256
        tq = tqp
        lpad = _round_up(past_len + tqp, tk)
        n_sel = min(TOPK_MAX, l_valid // 4)

        def tokens(a):
            a = jnp.swapaxes(a.reshape(a.shape[0], bsz, t), 0, 1)
            return jnp.pad(a, ((0, 0), (0, 0), (0, tqp - t)))

        pk_a, pv_a, pki, pk_c, pv_c, plogf, pk_d, pv_d = past
        qa, qc, qd, qi, sm_tok = (tokens(a) for a in (qa, qc, qd, qi, sm))
        new = [tokens(a) for a in kv]
        ka, va, kc, vc, kd, vd = ([p_, n_] for p_, n_ in zip((pk_a, pv_a, pk_c, pv_c, pk_d, pv_d), new))
        ki_srcs, f_srcs, f_row = [pki, sm_tok], [plogf, sm_tok], (0, SLAB_F)
        att_layer = layer
    sm_q = sm if is_prompt else sm_tok

    tq_a = min(512, tqp) if is_prompt else tq
    oa = _dsa(qa, qi, sm_q, ka, va, ki_srcs, layer=att_layer, lp=lpad, tq=tq_a, tk=tk, past_len=past_len,
              l_valid=l_valid, n_sel=n_sel, t_valid=t)[:, :t]
    oc = _fox(qc, kc, vc, f_srcs, f_row, layer=att_layer, lp=lpad, tq=tq_a, tk=tk, past_len=past_len)[:, :t]
    od = _sb(qd, kd, vd, layer=att_layer, lp=lpad, tq=tq, tk=tk_d, past_len=past_len)[:, :t]

    mtt = min(t, 512) if is_prompt else tt
    x = _merge(x, mod, lp["g1"], oa, ob, oc, od, lp["w_gate"], lp["b_gate"], lp["w_branch"], lp["w_out"],
               nb=nb, tt=mtt)
    x, conv_state = _ffn(x, mod, lp["g2"], prefix, lp["w_up"], lp["w_conv"], lp["b_conv"], lp["w_down"],
                         final_g, nb=nb, tt=tt, final_norm=final_norm)
    if is_prompt:
        rows = (sm, conv_state)
    else:
        heads = lambda a: jnp.transpose(a.reshape(N_HEADS, HEAD_DIM, bsz, t), (2, 3, 0, 1))
        small = lambda a, r0, r1: jnp.transpose(a[r0:r1].reshape(r1 - r0, bsz, t), (1, 2, 0))
        ka_n, va_n, kc_n, vc_n, kd_n, vd_n = kv
        rows = (heads(ka_n), heads(va_n), small(sm, SLAB_KI, SLAB_KI + IDX_DIM), vb, heads(kc_n), heads(vc_n),
                small(sm, SLAB_F, SLAB_F + N_HEADS), heads(kd_n), heads(vd_n), conv_state)
    return x, rows, kv


def kernel(x_prompt, x_sample, cache_a_k, cache_a_v, cache_a_kidx, cache_c_k, cache_c_v, cache_c_logf, cache_d_k, cache_d_v, state_ffn_conv, c_prompt, c_sample, norm1_g, norm2_g, w_mod, b_mod, w_in, f_bias, lnv_g, lnv_b, w_spatial, b_spatial, w_branch, w_gate, b_gate, w_out, w_up, w_conv, b_conv, w_down, final_g):
    depth = w_in.shape[0]
    bsz, t_prompt, d = x_prompt.shape
    dbsz = x_sample.shape[0]
    past_len = cache_a_k.shape[2]
    mods = _modulation(jnp.concatenate([c_prompt, c_sample], axis=0), w_mod, b_mod)
    mods = mods.reshape(depth, bsz + dbsz, 6, d)
    prefix = jnp.zeros((bsz, CONV_W - 1, w_up.shape[-1]), x_prompt.dtype)
    fg = final_g.reshape(1, d)
    past = tuple(_channel_major(c) for c in (cache_a_k, cache_a_v, cache_a_kidx, cache_c_k, cache_c_v,
                                              cache_c_logf.astype(F32), cache_d_k, cache_d_v))
    w_in_t = jnp.transpose(w_in, (2, 0, 1))
    xp, xs = x_prompt, x_sample
    rows_p, rows_s = [], []
    stacked = ()
    for l in range(depth):
        fb = jnp.zeros((SLAB, 1), F32).at[SLAB_F:SLAB_F + N_HEADS, 0].set(f_bias[l].astype(F32))
        lp = dict(g1=norm1_g[l].reshape(1, d), g2=norm2_g[l].reshape(1, d), w_t=_layout_w_in(w_in_t, l), fb=fb,
                  lng=lnv_g[l].reshape(BRANCH, 1), lnb=lnv_b[l].reshape(BRANCH, 1),
                  w_s=w_spatial[l], b_s=b_spatial[l], w_branch=w_branch[l].astype(BF16),
                  w_gate=w_gate[l].astype(BF16), b_gate=b_gate[l].reshape(1, -1), w_out=w_out[l].astype(BF16),
                  w_up=w_up[l].astype(BF16), w_conv=w_conv[l], b_conv=b_conv[l].reshape(1, -1),
                  w_down=w_down[l].astype(BF16))
        last = l == depth - 1
        xp, new_p, stacked = _layer(xp, mods[l, :bsz], lp, None, prefix, fg, stacked, layer=l, depth=depth,
                                    final_norm=last, past_len=0)
        xs, new_s, _ = _layer(xs, mods[l, bsz:], lp, past, state_ffn_conv[l], fg, None, layer=l, depth=depth,
                              final_norm=last, past_len=past_len)
        rows_p.append(new_p)
        rows_s.append(new_s)

    def stacked_s(i):
        return jnp.stack([r[i] for r in rows_s], axis=0)

    def heads_p(a):
        return jnp.transpose(a.reshape(depth, bsz, N_HEADS, HEAD_DIM, t_prompt), (0, 1, 4, 2, 3))

    sm_p = jnp.stack([r[0] for r in rows_p], axis=0)
    small_p = lambda r0, r1: jnp.swapaxes(sm_p[:, :, r0:r1, :], 2, 3)
    ka_p, va_p, kc_p, vc_p, kd_p, vd_p = (heads_p(a) for a in stacked)
    conv_p = jnp.stack([r[1] for r in rows_p], axis=0)
    return (xp, xs,
            ka_p, stacked_s(0),
            va_p, stacked_s(1),
            small_p(SLAB_KI, SLAB_KI + IDX_DIM), stacked_s(2),
            stacked_s(3),
            kc_p, stacked_s(4),
            vc_p, stacked_s(5),
            small_p(SLAB_F, SLAB_F + N_HEADS), stacked_s(6),
            kd_p, stacked_s(7),
            vd_p, stacked_s(8),
            conv_p, stacked_s(9))
```

```python
import functools

import jax
import jax.numpy as jnp
from jax import lax
from jax.experimental import pallas as pl
from jax.experimental.pallas import tpu as pltpu

F32 = jnp.float32
BF16 = jnp.bfloat16
I32 = jnp.int32
I16 = jnp.int16

HEAD_DIM = 64
N_HEADS = 4
BRANCH = N_HEADS * HEAD_DIM
IDX_HEADS = 8
IDX_DIM = 32
CHUNK = 64
TOPK_MAX = 256
B_CHUNK = 128
N_BRANCH = 4
CONV_W = 3
ROPE_THETA = 10000.0
EPS = 1e-6
LOG2E = 1.4426950408889634
Q_SCALE = HEAD_DIM ** -0.5 * LOG2E

SLAB = 128
SLAB_KI = 0
SLAB_W = 32
SLAB_F = 40
AUG = 128
F_ROWS = 8
F_BLK = 256
N_T_GROUPS = 10
NEG = -1e30
INT_MIN = -2 ** 31
INT_MAX = 2 ** 31 - 1
I16_MIN = -2 ** 15
SB_CUTOFF = 220.0
SB_SATURATE = 100.0
Q_PAD = 128
SUB = 8
TOKEN_TILE = 512
ATT_TQ, ATT_TK = 512, 512
SB_TQ, SB_TK = 256, 256
MOD_TN = 1536
FFN_CW = 256
HALF_BITS = 16
VMEM_LIMIT = 56 * 1024 * 1024


def _params(n_axes, vmem=VMEM_LIMIT):
    return pltpu.CompilerParams(dimension_semantics=("arbitrary",) * n_axes, vmem_limit_bytes=vmem)


def _dot(a, b):
    return jnp.dot(a, b, preferred_element_type=F32)


def _split2(x):
    hi = x.astype(BF16)
    lo = (x - hi.astype(F32)).astype(BF16)
    return hi, lo


def _split3(x):
    x1 = x.astype(BF16)
    r = x - x1.astype(F32)
    x2 = r.astype(BF16)
    x3 = (r - x2.astype(F32)).astype(BF16)
    return x1, x2, x3


def _softplus(z):
    return jnp.maximum(z, 0.0) + jnp.log1p(jnp.exp(-jnp.abs(z)))


def _rms(x, g):
    ms = jnp.mean(x * x, axis=-1, keepdims=True)
    return x * lax.rsqrt(ms + EPS) * g


def _rope_t(x, cos, sin_signed, half):
    rows = x.shape[0]
    row = lax.broadcasted_iota(I32, x.shape, 0)
    fwd = pltpu.roll(x, half, axis=0)
    bwd = pltpu.roll(x, rows - half, axis=0)
    rot = jnp.where((row % (2 * half)) < half, bwd, fwd)
    return x * cos + rot * sin_signed


def _mod_kernel(c_ref, w_ref, b_ref, o_ref):
    c = c_ref[...]
    a = c * jax.nn.sigmoid(c)
    a1, a2 = _split2(a)
    w1, w2 = _split2(w_ref[0])
    o_ref[0] = _dot(a1, w1) + _dot(a1, w2) + _dot(a2, w1) + b_ref[0]


def _modulation(c_all, w_mod, b_mod):
    depth, d, n = w_mod.shape
    rows = c_all.shape[0]
    tn = MOD_TN
    return pl.pallas_call(
        _mod_kernel,
        grid=(depth, n // tn),
        in_specs=[pl.BlockSpec((rows, d), lambda l, j: (0, 0)),
                  pl.BlockSpec((1, d, tn), lambda l, j: (l, 0, j)),
                  pl.BlockSpec((1, 1, tn), lambda l, j: (l, 0, j))],
        out_specs=pl.BlockSpec((1, rows, tn), lambda l, j: (l, 0, j)),
        out_shape=jax.ShapeDtypeStruct((depth, rows, n), F32),
        compiler_params=_params(2),
        name="modulation",
    )(c_all, w_mod, b_mod.reshape(depth, 1, n))


def _proj_kernel(x_ref, mod_ref, g1_ref, wt_ref, cosa_ref, sina_ref, cosi_ref, sini_ref, coss_ref,
                 sins_ref, fb_ref, lng_ref, lnb_ref, ws_ref, bs_ref, *rest, nb, tt, n_alias):
    (ka_ref, va_ref, kc_ref, vc_ref, kd_ref, vd_ref,
     qa_ref, qc_ref, qd_ref, qi_ref, sm_ref, ob_ref, vb_ref) = rest[n_alias:]
    m = nb * tt
    d = x_ref.shape[-1]
    x = x_ref[...]
    mod = mod_ref[...]
    h = (_rms(x, g1_ref[...]) * (1.0 + mod[:, 1:2, :]) + mod[:, 0:1, :]).reshape(m, d)
    ht = h.T.astype(BF16)

    def col_t(j):
        return _dot(wt_ref[j * BRANCH:(j + 1) * BRANCH, :], ht)

    def put_t(ref, val):
        ref[...] = val.reshape(ref.shape).astype(ref.dtype)

    def put(ref, val):
        ref[...] = val.reshape(nb, tt, val.shape[-1]).astype(ref.dtype)

    cosa, sina = cosa_ref[...], sina_ref[...]
    put_t(qa_ref, _rope_t(col_t(0), cosa, sina, HEAD_DIM // 2) * Q_SCALE)
    put_t(ka_ref, _rope_t(col_t(1), cosa, sina, HEAD_DIM // 2))
    put_t(va_ref, col_t(2))
    put_t(qi_ref, _rope_t(col_t(3), cosi_ref[...], sini_ref[...], IDX_DIM // 2))
    put_t(qc_ref, col_t(4) * Q_SCALE)
    put_t(kc_ref, col_t(5))
    put_t(vc_ref, col_t(6))
    put_t(qd_ref, col_t(7) * Q_SCALE)
    put_t(kd_ref, col_t(8))
    put_t(vd_ref, col_t(9))
    sm = _dot(wt_ref[(N_T_GROUPS + 2) * BRANCH:(N_T_GROUPS + 2) * BRANCH + SLAB, :], ht)
    row = lax.broadcasted_iota(I32, sm.shape, 0)
    roped = _rope_t(sm, coss_ref[...], sins_ref[...], IDX_DIM // 2)
    logf = -_softplus(-(sm + fb_ref[...]))
    is_f = jnp.where(row >= SLAB_F, jnp.where(row < SLAB_F + N_HEADS, 1, 0), 0)
    put_t(sm_ref, jnp.where(is_f == 1, logf, roped))
    u = jax.nn.gelu(col_t(N_T_GROUPS))
    vg = jax.nn.gelu(col_t(N_T_GROUPS + 1))
    mu = jnp.mean(vg, axis=0, keepdims=True)
    vc_ = vg - mu
    var = jnp.mean(vc_ * vc_, axis=0, keepdims=True)
    vb = vc_ * lax.rsqrt(var + EPS) * lng_ref[...] + lnb_ref[...]
    put(vb_ref, vb.T)
    vbb = vb.astype(BF16)
    mixed = jnp.concatenate([_dot(vbb[g * HEAD_DIM:(g + 1) * HEAD_DIM, :], ws_ref[g]) for g in range(N_HEADS)],
                            axis=0)
    put(ob_ref, (u * (mixed + bs_ref[...])).T)


def _proj(x, mod, g1, w_t, tables, fb, lng, lnb, ws, bs, *, nb, tt, stacked, layer, depth):
    bsz, t, d = x.shape
    m = nb * tt
    n_t = t // tt
    tok = lambda w: pl.BlockSpec((nb, tt, w), lambda ti, bi: (bi, ti, 0))
    const = lambda a: pl.BlockSpec(a.shape, lambda ti, bi: (0,) * a.ndim)
    tab = lambda c: pl.BlockSpec((c, m), lambda ti, bi: (0, ti))
    if stacked is None:
        assert n_t == 1 and nb == bsz
        kv_spec = pl.BlockSpec((BRANCH, m), lambda ti, bi: (0, 0))
        kv_shape = jax.ShapeDtypeStruct((BRANCH, m), F32)
        q_spec = lambda c: pl.BlockSpec((c, m), lambda ti, bi: (0, 0))
        q_shape = lambda c, dt: jax.ShapeDtypeStruct((c, m), dt)
        alias_in = ()
    else:
        assert nb == 1
        kv_spec = pl.BlockSpec((1, 1, BRANCH, tt), lambda ti, bi: (layer, bi, 0, ti))
        kv_shape = jax.ShapeDtypeStruct((depth, bsz, BRANCH, t), F32)
        q_spec = lambda c: pl.BlockSpec((1, c, tt), lambda ti, bi: (bi, 0, ti))
        q_shape = lambda c, dt: jax.ShapeDtypeStruct((bsz, c, t), dt)
        alias_in = tuple(stacked)
    n_alias = len(alias_in)
    n_in = 15
    cosa, sina, cosi, sini, coss, sins = tables
    return pl.pallas_call(
        functools.partial(_proj_kernel, nb=nb, tt=tt, n_alias=n_alias),
        grid=(n_t, bsz // nb),
        in_specs=[tok(d), pl.BlockSpec((nb, 6, d), lambda ti, bi: (bi, 0, 0)), const(g1), const(w_t),
                  tab(BRANCH), tab(BRANCH), tab(BRANCH), tab(BRANCH), tab(SLAB), tab(SLAB),
                  const(fb), const(lng), const(lnb), const(ws), const(bs)]
                 + [pl.BlockSpec(memory_space=pl.ANY)] * n_alias,
        out_specs=[kv_spec] * 6 + [q_spec(BRANCH)] * 4 + [q_spec(SLAB), tok(BRANCH), tok(BRANCH)],
        out_shape=[kv_shape] * 6 + [q_shape(BRANCH, BF16)] * 3 + [q_shape(BRANCH, F32), q_shape(SLAB, F32),
                                                                 jax.ShapeDtypeStruct((bsz, t, BRANCH), BF16),
                                                                 jax.ShapeDtypeStruct((bsz, t, BRANCH), F32)],
        input_output_aliases={n_in + i: i for i in range(n_alias)},
        compiler_params=_params(2),
        name="proj",
    )(x, mod, g1, w_t, cosa, sina, cosi, sini, coss, sins, fb, lng, lnb, ws, bs, *alias_in)


def _load2d(ref):
    return ref[(0,) * (len(ref.shape) - 2)]


def _fill_token_major(dst_sc, srcs):
    off = 0
    for s in srcs:
        n = s.shape[1]
        dst_sc[off:off + n, :] = s.T.astype(BF16)
        off += n
    lp = dst_sc.shape[0]
    if off < lp:
        dst_sc[off:lp, :] = jnp.zeros((lp - off, dst_sc.shape[1]), BF16)


def _fill_channel_major(dst_sc, srcs):
    off = 0
    for s in srcs:
        n = s.shape[1]
        dst_sc[:, off:off + n] = s.astype(BF16)
        off += n
    lp = dst_sc.shape[1]
    if off < lp:
        dst_sc[:, off:lp] = jnp.zeros((dst_sc.shape[0], lp - off), BF16)


def _flash_init(tq):
    return tuple((jnp.full((1, tq), NEG, F32), jnp.zeros((1, tq), F32), jnp.zeros((HEAD_DIM, tq), F32))
                 for _ in range(N_HEADS))


def _flash_step(carry, score_fns, vt_sc, off, tk):
    scores = [fn() for fn in score_fns]
    ms = [jnp.maximum(carry[h][0], jnp.max(scores[h], axis=0, keepdims=True)) for h in range(N_HEADS)]
    ps = [jnp.exp2(scores[h] - ms[h]) for h in range(N_HEADS)]
    pvs = [_dot(vt_sc[h * HEAD_DIM:(h + 1) * HEAD_DIM, pl.ds(off, tk)], ps[h].astype(BF16))
           for h in range(N_HEADS)]
    new = []
    for h in range(N_HEADS):
        m_run, l_run, acc = carry[h]
        alpha = jnp.exp2(m_run - ms[h])
        new.append((ms[h], alpha * l_run + jnp.sum(ps[h], axis=0, keepdims=True), alpha * acc + pvs[h]))
    return tuple(new)


def _flash_finish(carry):
    out_t = jnp.concatenate([acc / l_run for (_, l_run, acc) in carry], axis=0)
    return out_t.T


def _tree_sum(terms):
    while len(terms) > 1:
        terms = [a + b for a, b in zip(terms[::2], terms[1::2])] + ([terms[-1]] if len(terms) % 2 else [])
    return terms[0]


def _masked_heads_t(q_t):
    head = lax.broadcasted_iota(I32, q_t.shape, 0) // HEAD_DIM
    return [jnp.where(head == h, q_t, jnp.zeros_like(q_t)) for h in range(N_HEADS)]


def _src_spec(arr, layer):
    if arr.ndim == 4:
        return pl.BlockSpec((1, 1) + arr.shape[2:], lambda b, i: (layer, b, 0, 0))
    return pl.BlockSpec((1,) + arr.shape[1:], lambda b, i: (b, 0, 0))


def _q_spec(c, tq):
    return pl.BlockSpec((1, c, tq), lambda b, i: (b, 0, i))


def _o_spec(tq):
    return pl.BlockSpec((1, tq, BRANCH), lambda b, i: (b, i, 0))


def _dsa_kernel(qa_ref, qi_ref, qsm_ref, *refs, n_src, tq, tk, past_len, l_valid, n_sel, t_valid, single_q):
    k_refs, v_refs, ki_refs = refs[:n_src], refs[n_src:2 * n_src], refs[2 * n_src:3 * n_src]
    o_ref, kb_sc, vt_sc, kic_sc, lhs_sc, key_sc, hi_sc, lo_sc, thr_sc, room_sc, tie_sc = refs[3 * n_src:]
    qb = pl.program_id(1)

    @pl.when(qb == 0)
    def _():
        _fill_token_major(kb_sc, [_load2d(r) for r in k_refs])
        _fill_channel_major(vt_sc, [_load2d(r) for r in v_refs])
        cats = []
        for r in ki_refs:
            ki = _load2d(r)[SLAB_KI:SLAB_KI + IDX_DIM, :]
            hi = ki.astype(BF16).astype(F32)
            cats.append(jnp.concatenate([hi, ki - hi, hi, jnp.zeros_like(hi)], axis=0))
        _fill_token_major(kic_sc, cats)

    qi = qi_ref[0]
    for h in range(IDX_HEADS):
        piece = qi[h * IDX_DIM:(h + 1) * IDX_DIM, :]
        hi = piece.astype(BF16).astype(F32)
        lhs_sc[h] = jnp.concatenate([hi, hi, piece - hi, jnp.zeros_like(hi)], axis=0).astype(BF16)
    w_t = qsm_ref[0]

    qlane = lax.broadcasted_iota(I32, (1, tq), 1)
    if single_q:
        row0 = past_len
        top_limit = min(((row0 + tq - 1) // CHUNK + 1) * CHUNK, l_valid)
        count_loop = functools.partial(lax.fori_loop, unroll=True)
        block_loop = functools.partial(lax.fori_loop, unroll=2)
    else:
        row0 = past_len + qb * tq
        top_limit = jnp.minimum(((row0 + tq - 1) // CHUNK + 1) * CHUNK, l_valid)
        count_loop = block_loop = lax.fori_loop
    pos = row0 + qlane
    limit = jnp.minimum((pos // CHUNK + 1) * CHUNK, l_valid)
    nblk = (top_limit + tk - 1) // tk

    def score_blk(j, _):
        off = pl.multiple_of(j * tk, tk)
        kc = kic_sc[pl.ds(off, tk), :]
        dots = [_dot(kc, lhs_sc[h]) for h in range(IDX_HEADS)]
        acc = jnp.zeros((tk, tq), F32)
        for h in range(IDX_HEADS):
            acc = acc + w_t[SLAB_W + h:SLAB_W + h + 1, :] * jnp.maximum(dots[h], 0.0)
        acc = jnp.where(acc == 0.0, 0.0, acc)
        bits = pltpu.bitcast(acc, I32)
        key = jnp.where(bits < 0, bits ^ INT_MAX, bits)
        kidx = off + lax.broadcasted_iota(I32, (tk, tq), 0)
        key = jnp.where(kidx < limit, key, INT_MIN)
        key_sc[pl.ds(off, tk), :] = key
        hi_sc[pl.ds(off, tk), :] = (key >> HALF_BITS).astype(I16)
        lo_sc[pl.ds(off, tk), :] = ((key & (2 ** HALF_BITS - 1)) + I16_MIN).astype(I16)
        return 0

    block_loop(0, nblk, score_blk, 0)

    def count16(ref, cand):
        cand_b = jnp.broadcast_to(cand, (16, tq)).astype(I16)

        def body(j, part):
            off = pl.multiple_of(j * tk, tk)
            kb = ref[pl.ds(off, tk), :]
            return part + _tree_sum([jnp.where(kb[c * 16:(c + 1) * 16, :] >= cand_b, jnp.int16(1), jnp.int16(0))
                                     for c in range(tk // 16)])

        part = count_loop(0, nblk, body, jnp.zeros((16, tq), I16))
        return jnp.sum(part.astype(F32), axis=0, keepdims=True)

    def count32(cand):
        cand_b = jnp.broadcast_to(cand, (8, tq))

        def body(j, part):
            off = pl.multiple_of(j * tk, tk)
            kb = key_sc[pl.ds(off, tk), :]
            return part + _tree_sum([jnp.where(kb[c * 8:(c + 1) * 8, :] >= cand_b, 1.0, 0.0)
                                     for c in range(tk // 8)])

        part = count_loop(0, nblk, body, jnp.zeros((8, tq), F32))
        return jnp.sum(part, axis=0, keepdims=True)

    def kth16(ref, want):
        t = jnp.where(count16(ref, jnp.zeros((1, tq), I32)) >= want, 0, I16_MIN).astype(I32)

        def search(i, t):
            cand = t | jnp.left_shift(jnp.int32(1), HALF_BITS - 2 - i)
            return jnp.where(count16(ref, cand) >= want, cand, t)

        return lax.fori_loop(0, HALF_BITS - 1, search, t)

    kf = float(n_sel)
    thr_sc[...] = jnp.full(thr_sc.shape, INT_MIN + 1, I32)
    room_sc[...] = jnp.zeros(room_sc.shape, F32)
    tie_sc[...] = jnp.zeros(tie_sc.shape, F32)

    @pl.when(top_limit > n_sel)
    def _():
        t_hi = kth16(hi_sc, kf)
        above = jnp.where(t_hi >= -I16_MIN - 1, 0.0, count16(hi_sc, t_hi + 1))
        t_hi_b = jnp.broadcast_to(t_hi, (16, tq)).astype(I16)

        def keep_equal_hi(j, _):
            off = pl.multiple_of(j * tk, tk)
            hi_blk = hi_sc[pl.ds(off, tk), :]
            lo_blk = lo_sc[pl.ds(off, tk), :]
            kept = [jnp.where(hi_blk[c * 16:(c + 1) * 16, :] == t_hi_b, lo_blk[c * 16:(c + 1) * 16, :],
                              jnp.int16(I16_MIN)) for c in range(tk // 16)]
            lo_sc[pl.ds(off, tk), :] = jnp.concatenate(kept, axis=0)
            return 0

        count_loop(0, nblk, keep_equal_hi, 0)
        t_lo = kth16(lo_sc, kf - above)
        t_full = jnp.maximum(t_hi * 2 ** HALF_BITS + (t_lo - I16_MIN), INT_MIN + 1)
        cnt_ge = count32(t_full)
        cnt_gt = count32(t_full + 1)
        real_q = qlane < (t_valid - qb * tq)
        thr_sc[...] = jnp.broadcast_to(t_full, thr_sc.shape)
        room_sc[...] = jnp.broadcast_to(kf - cnt_gt, room_sc.shape)
        tie_sc[...] = jnp.broadcast_to(jnp.where(real_q, jnp.where(cnt_ge > kf, 1.0, 0.0), 0.0), tie_sc.shape)

    thr = thr_sc[0:1, :]
    room = room_sc[0:1, :]
    any_tie = jnp.max(tie_sc[0:1, :]) > 0.0

    @pl.when(any_tie)
    def _():
        ri = lax.broadcasted_iota(I32, (tk, tk), 0)
        ci = lax.broadcasted_iota(I32, (tk, tk), 1)
        upto = jnp.where(ci <= ri, 1.0, 0.0).astype(BF16)

        def drop_late_ties(j, seen):
            off = pl.multiple_of(j * tk, tk)
            kb = key_sc[pl.ds(off, tk), :]
            eq = jnp.where(kb == thr, 1.0, 0.0)
            rank = _dot(upto, eq.astype(BF16)) + seen
            key_sc[pl.ds(off, tk), :] = jnp.where(eq * rank > room, INT_MIN, kb)
            return seen + jnp.sum(eq, axis=0, keepdims=True)

        lax.fori_loop(0, nblk, drop_late_ties, jnp.zeros((1, tq), F32))

    qh = _masked_heads_t(qa_ref[0])

    def attend(j, carry):
        off = pl.multiple_of(j * tk, tk)
        kblk = kb_sc[pl.ds(off, tk), :]
        keep = key_sc[pl.ds(off, tk), :] >= thr
        score_fns = [functools.partial(lambda h: jnp.where(keep, _dot(kblk, qh[h]), NEG), h) for h in range(N_HEADS)]
        return _flash_step(carry, score_fns, vt_sc, off, tk)

    carry = block_loop(0, nblk, attend, _flash_init(tq))
    o_ref[0] = _flash_finish(carry).astype(o_ref.dtype)


def _dsa(qa, qi, qsm, k_srcs, v_srcs, ki_srcs, *, layer, lp, tq, tk, past_len, l_valid, n_sel, t_valid):
    bsz, _, t = qa.shape
    n_src = len(k_srcs)
    srcs = list(k_srcs) + list(v_srcs) + list(ki_srcs)
    return pl.pallas_call(
        functools.partial(_dsa_kernel, n_src=n_src, tq=tq, tk=tk, past_len=past_len, l_valid=l_valid,
                          n_sel=n_sel, t_valid=t_valid, single_q=(t == tq)),
        grid=(bsz, t // tq),
        in_specs=[_q_spec(BRANCH, tq), _q_spec(BRANCH, tq), _q_spec(SLAB, tq)] + [_src_spec(a, layer) for a in srcs],
        out_specs=_o_spec(tq),
        out_shape=jax.ShapeDtypeStruct((bsz, t, BRANCH), BF16),
        scratch_shapes=[pltpu.VMEM((lp, BRANCH), BF16), pltpu.VMEM((BRANCH, lp), BF16),
                        pltpu.VMEM((lp, SLAB), BF16), pltpu.VMEM((IDX_HEADS, SLAB, tq), BF16),
                        pltpu.VMEM((lp, tq), I32), pltpu.VMEM((lp, tq), I16), pltpu.VMEM((lp, tq), I16),
                        pltpu.VMEM((SUB, tq), I32), pltpu.VMEM((SUB, tq), F32), pltpu.VMEM((SUB, tq), F32)],
        compiler_params=_params(2),
        name="dsa",
    )(qa, qi, qsm, *srcs)


def _fox_kernel(q_ref, *refs, n_src, f_row, tq, tk, past_len, single_q):
    k_refs, v_refs, f_refs = refs[:n_src], refs[n_src:2 * n_src], refs[2 * n_src:3 * n_src]
    o_ref, vt_sc, kaug_sc, f_sc = refs[3 * n_src:]
    qb = pl.program_id(1)
    lp = f_sc.shape[1]

    @pl.when(qb == 0)
    def _():
        _fill_channel_major(vt_sc, [_load2d(r) for r in v_refs])
        ks = [_load2d(r) for r in k_refs]
        fs = []
        for r, row in zip(f_refs, f_row):
            logf = _load2d(r)
            if row + F_ROWS <= logf.shape[0]:
                fs.append(logf[row:row + F_ROWS, :])
            else:
                gates = logf[row:row + N_HEADS, :]
                fs.append(jnp.concatenate([gates, jnp.zeros((F_ROWS - N_HEADS, gates.shape[1]), F32)], axis=0))
        kt = ks[0] if len(ks) == 1 else jnp.concatenate(ks, axis=1)
        ft = fs[0] if len(fs) == 1 else jnp.concatenate(fs, axis=1)
        n_real = kt.shape[1]
        ri = lax.broadcasted_iota(I32, (F_BLK, F_BLK), 0)
        ci = lax.broadcasted_iota(I32, (F_BLK, F_BLK), 1)
        upto = jnp.where(ri <= ci, 1.0, 0.0).astype(BF16)
        rowx = lax.broadcasted_iota(I32, (HEAD_DIM, F_BLK), 0)
        run = jnp.zeros((F_ROWS, 1), F32)
        for b in range(lp // F_BLK):
            lo, hi_ = b * F_BLK, min((b + 1) * F_BLK, n_real)
            if hi_ <= lo:
                kaug_sc[lo:lo + F_BLK, :] = jnp.zeros((F_BLK, N_HEADS * AUG), BF16)
                f_sc[:, lo:lo + F_BLK] = jnp.broadcast_to(run * LOG2E, (F_ROWS, F_BLK))
                continue
            k_blk, f_blk = kt[:, lo:hi_], ft[:, lo:hi_]
            if hi_ - lo < F_BLK:
                k_blk = jnp.concatenate([k_blk, jnp.zeros((BRANCH, F_BLK - (hi_ - lo)), F32)], axis=1)
                f_blk = jnp.concatenate([f_blk, jnp.zeros((F_ROWS, F_BLK - (hi_ - lo)), F32)], axis=1)
            s1, s2, s3 = _split3(f_blk)
            fsum = _dot(s1, upto) + _dot(s2, upto) + _dot(s3, upto) + run
            run = fsum[:, F_BLK - 1:F_BLK]
            fsum = fsum * LOG2E
            f_sc[:, lo:lo + F_BLK] = fsum
            f1, f2, f3 = (p.astype(F32) for p in _split3(fsum))
            for h in range(N_HEADS):
                extra = jnp.where(rowx < 3, 1.0,
                                  jnp.where(rowx == 3, -f1[h:h + 1, :],
                                            jnp.where(rowx == 4, -f2[h:h + 1, :],
                                                      jnp.where(rowx == 5, -f3[h:h + 1, :], 0.0))))
                aug_t = jnp.concatenate([k_blk[h * HEAD_DIM:(h + 1) * HEAD_DIM, :], extra], axis=0)
                kaug_sc[lo:lo + F_BLK, h * AUG:(h + 1) * AUG] = aug_t.T.astype(BF16)

    if single_q:
        row0 = past_len
        f_q = f_sc[:, row0:row0 + tq]
        block_loop = functools.partial(lax.fori_loop, unroll=2)
    else:
        row0 = past_len + qb * tq
        f_q = f_sc[:, pl.ds(pl.multiple_of(row0, Q_PAD), tq)]
        block_loop = lax.fori_loop
    pos = row0 + lax.broadcasted_iota(I32, (1, tq), 1)
    n_full = row0 // tk
    n_all = (row0 + tq + tk - 1) // tk
    q_t = q_ref[0].astype(F32)
    f1, f2, f3 = (p.astype(F32) for p in _split3(f_q))
    rowq = lax.broadcasted_iota(I32, (HEAD_DIM, tq), 0)
    qh = []
    for h in range(N_HEADS):
        extra = jnp.where(rowq == 0, f1[h:h + 1, :],
                          jnp.where(rowq == 1, f2[h:h + 1, :],
                                    jnp.where(rowq == 2, f3[h:h + 1, :], jnp.where(rowq < 6, 1.0, 0.0))))
        qh.append(jnp.concatenate([q_t[h * HEAD_DIM:(h + 1) * HEAD_DIM, :], extra], axis=0).astype(BF16))

    def step(j, carry, masked):
        off = pl.multiple_of(j * tk, tk)
        causal = (off + lax.broadcasted_iota(I32, (tk, tq), 0) <= pos) if masked else None

        def score(h):
            s = _dot(kaug_sc[pl.ds(off, tk), h * AUG:(h + 1) * AUG], qh[h])
            return jnp.where(causal, s, NEG) if masked else s

        return _flash_step(carry, [functools.partial(score, h) for h in range(N_HEADS)], vt_sc, off, tk)

    carry = block_loop(0, n_full, lambda j, c: step(j, c, False), _flash_init(tq))
    carry = lax.fori_loop(n_full, n_all, lambda j, c: step(j, c, True), carry)
    o_ref[0] = _flash_finish(carry).astype(o_ref.dtype)


def _fox(q, k_srcs, v_srcs, f_srcs, f_row, *, layer, lp, tq, tk, past_len):
    bsz, _, t = q.shape
    n_src = len(k_srcs)
    srcs = list(k_srcs) + list(v_srcs) + list(f_srcs)
    return pl.pallas_call(
        functools.partial(_fox_kernel, n_src=n_src, f_row=tuple(f_row), tq=tq, tk=tk, past_len=past_len,
                          single_q=(t == tq)),
        grid=(bsz, t // tq),
        in_specs=[_q_spec(BRANCH, tq)] + [_src_spec(a, layer) for a in srcs],
        out_specs=_o_spec(tq),
        out_shape=jax.ShapeDtypeStruct((bsz, t, BRANCH), BF16),
        scratch_shapes=[pltpu.VMEM((BRANCH, lp), BF16), pltpu.VMEM((lp, N_HEADS * AUG), BF16),
                        pltpu.VMEM((F_ROWS, lp), F32)],
        compiler_params=_params(2),
        name="fox",
    )(q, *srcs)


def _sb_kernel(q_ref, *refs, n_src, tq, tk, past_len):
    k_refs, v_refs = refs[:n_src], refs[n_src:2 * n_src]
    o_ref, kb_sc, vt_sc = refs[2 * n_src:]
    qb = pl.program_id(1)

    @pl.when(qb == 0)
    def _():
        _fill_token_major(kb_sc, [_load2d(r) for r in k_refs])
        _fill_channel_major(vt_sc, [_load2d(r) for r in v_refs])

    qh = _masked_heads_t(q_ref[0])
    row0 = past_len + qb * tq
    pos = row0 + lax.broadcasted_iota(I32, (1, tq), 1)
    j_top = (row0 + tq - 1) // tk
    ri = lax.broadcasted_iota(I32, (tk, tk), 0)
    ci = lax.broadcasted_iota(I32, (tk, tk), 1)
    after = jnp.where(ci > ri, 1.0, 0.0).astype(BF16)

    def cond(carry):
        j, state = carry
        live = state[0][0]
        for h in range(1, N_HEADS):
            live = jnp.maximum(live, state[h][0])
        return jnp.logical_and(j >= 0, jnp.max(live) > -SB_CUTOFF)

    def body(carry, masked=False):
        j, state = carry
        off = pl.multiple_of(j * tk, tk)
        kblk = kb_sc[pl.ds(off, tk), :]
        zs = [_dot(kblk, qh[h]) for h in range(N_HEADS)]
        keeps = [-jnp.maximum(jnp.log2(1.0 + jnp.exp2(jnp.minimum(z, SB_SATURATE))), z) for z in zs]
        if masked:
            strict = off + lax.broadcasted_iota(I32, (tk, tq), 0) < pos
            keeps = [jnp.where(strict, kp, 0.0) for kp in keeps]
        laters = []
        for h in range(N_HEADS):
            k_hi, k_lo = _split2(keeps[h])
            laters.append(_dot(after, k_hi) + _dot(after, k_lo) + state[h][0])
        ws = [jnp.exp2(zs[h] + keeps[h] + laters[h]) for h in range(N_HEADS)]
        if masked:
            ws = [jnp.where(strict, w, 0.0) for w in ws]
        pvs = [_dot(vt_sc[h * HEAD_DIM:(h + 1) * HEAD_DIM, pl.ds(off, tk)], ws[h].astype(BF16))
               for h in range(N_HEADS)]
        new = tuple((state[h][0] + jnp.sum(keeps[h], axis=0, keepdims=True), state[h][1] + pvs[h])
                    for h in range(N_HEADS))
        return j - 1, new

    init = tuple((jnp.zeros((1, tq), F32), jnp.zeros((HEAD_DIM, tq), F32)) for _ in range(N_HEADS))
    first = body((j_top, init), masked=True)
    _, state = lax.while_loop(cond, body, first)
    out_t = jnp.concatenate([acc for (_, acc) in state], axis=0)
    o_ref[0] = out_t.T.astype(o_ref.dtype)


def _sb(q, k_srcs, v_srcs, *, layer, lp, tq, tk, past_len):
    bsz, _, t = q.shape
    assert tk % tq == 0 and past_len % tq == 0
    n_src = len(k_srcs)
    srcs = list(k_srcs) + list(v_srcs)
    return pl.pallas_call(
        functools.partial(_sb_kernel, n_src=n_src, tq=tq, tk=tk, past_len=past_len),
        grid=(bsz, t // tq),
        in_specs=[_q_spec(BRANCH, tq)] + [_src_spec(a, layer) for a in srcs],
        out_specs=_o_spec(tq),
        out_shape=jax.ShapeDtypeStruct((bsz, t, BRANCH), BF16),
        scratch_shapes=[pltpu.VMEM((lp, BRANCH), BF16), pltpu.VMEM((BRANCH, lp), BF16)],
        compiler_params=_params(2),
        name="sb",
    )(q, *srcs)


def _merge_kernel(x_ref, mod_ref, g1_ref, oa_ref, ob_ref, oc_ref, od_ref, wg_ref, bg_ref, wb_ref, wo_ref,
                  xo_ref, *, nb, tt):
    m = nb * tt
    d = x_ref.shape[-1]
    x = x_ref[...]
    mod = mod_ref[...]
    h = _rms(x, g1_ref[...]) * (1.0 + mod[:, 1:2, :]) + mod[:, 0:1, :]
    hb = h.reshape(m, d).astype(BF16)
    o_refs = (oa_ref, ob_ref, oc_ref, od_ref)

    def pre(i):
        return (_dot(hb, wg_ref[:, i * d:(i + 1) * d]), _dot(o_refs[i][...].reshape(m, BRANCH), wb_ref[i]))

    merged = None
    cur = pre(0)
    for i in range(N_BRANCH):
        nxt = pre(i + 1) if i + 1 < N_BRANCH else None
        term = jax.nn.sigmoid(cur[0] + bg_ref[:, i * d:(i + 1) * d]) * cur[1]
        merged = term if merged is None else merged + term
        cur = nxt
    y = _dot(merged.astype(BF16), wo_ref[...])
    xo_ref[...] = x + mod[:, 2:3, :] * y.reshape(nb, tt, d)


def _merge(x, mod, g1, oa, ob, oc, od, w_gate, b_gate, w_branch, w_out, *, nb, tt):
    bsz, t, d = x.shape
    tok = lambda w: pl.BlockSpec((nb, tt, w), lambda bi, ti: (bi, ti, 0))
    const = lambda a: pl.BlockSpec(a.shape, lambda bi, ti: (0,) * a.ndim)
    return pl.pallas_call(
        functools.partial(_merge_kernel, nb=nb, tt=tt),
        grid=(bsz // nb, t // tt),
        in_specs=[tok(d), pl.BlockSpec((nb, 6, d), lambda bi, ti: (bi, 0, 0)), const(g1),
                  tok(BRANCH), tok(BRANCH), tok(BRANCH), tok(BRANCH),
                  const(w_gate), const(b_gate), const(w_branch), const(w_out)],
        out_specs=tok(d),
        out_shape=jax.ShapeDtypeStruct((bsz, t, d), F32),
        compiler_params=_params(2),
        name="merge",
    )(x, mod, g1, oa, ob, oc, od, w_gate, b_gate, w_branch, w_out)


def _ffn_kernel(x_ref, mod_ref, g2_ref, pfx_ref, wu_ref, wc_ref, bc_ref, wd_ref, fg_ref,
                xo_ref, st_ref, carry_sc, ext_sc, act_sc, *, nb, tt, cw, final_norm):
    m = nb * tt
    d = x_ref.shape[-1]
    dff = wd_ref.shape[0]
    ti = pl.program_id(1)
    keep = CONV_W - 1

    @pl.when(ti == 0)
    def _():
        carry_sc[...] = jnp.zeros_like(carry_sc)
        carry_sc[:, SUB - keep:, :] = pfx_ref[...]

    x = x_ref[...]
    mod = mod_ref[...]
    h = _rms(x, g2_ref[...]) * (1.0 + mod[:, 4:5, :]) + mod[:, 3:4, :]
    hb = h.reshape(m, d).astype(BF16)

    def up_cols(c0):
        return _dot(hb, wu_ref[:, c0:c0 + cw]).reshape(nb, tt, cw)

    def stage(up, c0, slot):
        ext_sc[slot, :, 0:SUB, :] = carry_sc[:, :, c0:c0 + cw]
        ext_sc[slot, :, SUB:, :] = up
        carry_sc[:, :, c0:c0 + cw] = up[:, tt - SUB:, :]

    def conv(c0, slot):
        wc = wc_ref[:, c0:c0 + cw]
        win = lambda back: ext_sc[slot, :, SUB - back:SUB - back + tt, :]
        out = win(2) * wc[0:1, :] + win(1) * wc[1:2, :] + win(0) * wc[2:3, :] + bc_ref[:, c0:c0 + cw]
        return out.reshape(m, cw)

    n_chunks = dff // cw
    split = (n_chunks + 1) // 2 * cw
    ups = (up_cols(0), up_cols(dff))
    acc = None
    for j in range(n_chunks):
        nxt = (up_cols((j + 1) * cw), up_cols(dff + (j + 1) * cw)) if j + 1 < n_chunks else None
        slot = 2 * (j % 2)
        stage(ups[0], j * cw, slot)
        stage(ups[1], dff + j * cw, slot + 1)
        gate = conv(j * cw, slot)
        val = conv(dff + j * cw, slot + 1)
        act_sc[:, j * cw:(j + 1) * cw] = (gate * jax.nn.sigmoid(gate) * val).astype(BF16)
        if (j + 1) * cw == split:
            acc = _dot(act_sc[:, :split], wd_ref[:split, :])
        ups = nxt
    acc = acc + _dot(act_sc[:, split:], wd_ref[split:, :])
    xo = x + mod[:, 5:6, :] * acc.reshape(nb, tt, d)
    if final_norm:
        xo = _rms(xo, fg_ref[...])
    xo_ref[...] = xo

    @pl.when(ti == pl.num_programs(1) - 1)
    def _():
        st_ref[...] = carry_sc[:, SUB - keep:, :]


def _ffn(x, mod, g2, prefix, w_up, w_conv, b_conv, w_down, final_g, *, nb, tt, final_norm):
    bsz, t, d = x.shape
    dff = w_down.shape[0]
    cw = FFN_CW
    tok = pl.BlockSpec((nb, tt, d), lambda bi, ti: (bi, ti, 0))
    const = lambda a: pl.BlockSpec(a.shape, lambda bi, ti: (0,) * a.ndim)
    state = pl.BlockSpec((nb, CONV_W - 1, 2 * dff), lambda bi, ti: (bi, 0, 0))
    return pl.pallas_call(
        functools.partial(_ffn_kernel, nb=nb, tt=tt, cw=cw, final_norm=final_norm),
        grid=(bsz // nb, t // tt),
        in_specs=[tok, pl.BlockSpec((nb, 6, d), lambda bi, ti: (bi, 0, 0)), const(g2), state,
                  const(w_up), const(w_conv), const(b_conv), const(w_down), const(final_g)],
        out_specs=[tok, state],
        out_shape=[jax.ShapeDtypeStruct((bsz, t, d), F32),
                   jax.ShapeDtypeStruct((bsz, CONV_W - 1, 2 * dff), F32)],
        scratch_shapes=[pltpu.VMEM((nb, SUB, 2 * dff), F32), pltpu.VMEM((4, nb, tt + SUB, cw), F32),
                        pltpu.VMEM((nb * tt, dff), BF16)],
        compiler_params=_params(2),
        name="ffn",
    )(x, mod, g2, prefix, w_up, w_conv, b_conv, w_down, final_g)


def _rope_tables_t(pos):
    def tab(half, heads, pad):
        inv = ROPE_THETA ** (-jnp.arange(half, dtype=F32) / half)
        ang = inv[:, None] * pos.astype(F32)[None, :]
        cos, sin = jnp.cos(ang), jnp.sin(ang)
        cos_t = jnp.tile(jnp.concatenate([cos, cos], axis=0), (heads, 1))
        sin_t = jnp.tile(jnp.concatenate([-sin, sin], axis=0), (heads, 1))
        if pad:
            cos_t = jnp.concatenate([cos_t, jnp.ones((pad, pos.shape[0]), F32)], axis=0)
            sin_t = jnp.concatenate([sin_t, jnp.zeros((pad, pos.shape[0]), F32)], axis=0)
        return cos_t, sin_t

    cosa, sina = tab(HEAD_DIM // 2, N_HEADS, 0)
    cosi, sini = tab(IDX_DIM // 2, IDX_HEADS, 0)
    coss, sins = tab(IDX_DIM // 2, 1, SLAB - IDX_DIM)
    return cosa, sina, cosi, sini, coss, sins


def _layout_w_in(w_in_t, layer):
    sizes = (BRANCH, BRANCH, BRANCH, IDX_HEADS * IDX_DIM, IDX_DIM, IDX_HEADS, BRANCH, BRANCH,
             BRANCH, BRANCH, BRANCH, N_HEADS, BRANCH, BRANCH, BRANCH)
    offs = [0]
    for s in sizes:
        offs.append(offs[-1] + s)
    piece = lambda i: w_in_t[offs[i]:offs[i + 1], layer, :]
    a_q, a_k, a_v, a_qi, a_ki, a_w, b_u, b_v, c_q, c_k, c_v, c_f, d_q, d_k, d_v = (piece(i) for i in range(15))
    pad = jnp.zeros((SLAB - IDX_DIM - IDX_HEADS - N_HEADS, w_in_t.shape[-1]), w_in_t.dtype)
    return jnp.concatenate([a_q, a_k, a_v, a_qi, c_q, c_k, c_v, d_q, d_k, d_v, b_u, b_v,
                            a_ki, a_w, c_f, pad], axis=0).astype(BF16)


def _block_diag_tril_t(w_s, n, reps):
    tri = jnp.tril(jnp.ones((n, n), dtype=bool))
    w = jnp.where(tri[None], w_s[:, :n, :n], 0)
    eye = jnp.eye(reps, dtype=w.dtype)
    return jnp.einsum("ab,gts->gbsat", eye, w).reshape(w.shape[0], reps * n, reps * n).astype(BF16)


def _round_up(a, b):
    return (a + b - 1) // b * b


def _channel_major(cache):
    depth, bsz, length = cache.shape[:3]
    flat = cache.reshape(depth, bsz, length, -1)
    return jnp.swapaxes(flat, 2, 3)


def _layer(x, mod, lp, past, prefix, final_g, stacked, *, layer, depth, final_norm, past_len):
    bsz, t, d = x.shape
    is_prompt = past is None
    n = min(t, B_CHUNK)
    if is_prompt:
        nb, tt = 1, min(t, TOKEN_TILE)
        reps = tt // n
        pos_rows = jnp.arange(t)
    else:
        nb, tt = bsz, t
        reps = (nb * tt) // n
        pos_rows = jnp.tile(past_len + jnp.arange(t), nb)
    m = nb * tt
    tables = _rope_tables_t(pos_rows)
    ws = _block_diag_tril_t(lp["w_s"], n, reps)
    bs = jnp.tile(jnp.repeat(lp["b_s"][:, :n], HEAD_DIM, axis=0), (1, reps))
    fb = jnp.broadcast_to(lp["fb"], (SLAB, m))
    lng = jnp.broadcast_to(lp["lng"], (BRANCH, m))
    lnb = jnp.broadcast_to(lp["lnb"], (BRANCH, m))
    outs = _proj(x, mod, lp["g1"], lp["w_t"], tables, fb, lng, lnb, ws, bs,
                 nb=nb, tt=tt, stacked=stacked if is_prompt else None, layer=layer, depth=depth)
    kv, (qa, qc, qd, qi, sm, ob, vb) = outs[:6], outs[6:]

    l_valid = past_len + t
    tqp = _round_up(t, Q_PAD)
    tk, tk_d = min(ATT_TK, past_len + tqp), min(SB_TK, past_len + tqp)
    tq_a, tq_d = min(ATT_TQ, tqp), min(SB_TQ, tqp)
    if is_prompt:
        lpad = t
        n_sel = min(TOPK_MAX, t // 4)
        ka, va, kc, vc, kd, vd = ([a] for a in kv)
        ki_srcs, f_srcs, f_row = [sm], [sm], (SLAB_F,)
    else:
        lpad = _round_up(past_len + tqp, tk)
        n_sel = min(TOPK_MAX, l_valid // 4)

        def tokens(a):
            a = jnp.swapaxes(a.reshape(a.shape[0], bsz, t), 0, 1)
            return jnp.pad(a, ((0, 0), (0, 0), (0, tqp - t)))

        pk_a, pv_a, pki, pk_c, pv_c, plogf, pk_d, pv_d = past
        qa, qc, qd, qi, sm_tok = (tokens(a) for a in (qa, qc, qd, qi, sm))
        new = [tokens(a) for a in kv]
        ka, va, kc, vc, kd, vd = ([p_, n_] for p_, n_ in zip((pk_a, pv_a, pk_c, pv_c, pk_d, pv_d), new))
        ki_srcs, f_srcs, f_row = [pki, sm_tok], [plogf, sm_tok], (0, SLAB_F)
    sm_q = sm if is_prompt else sm_tok

    oa = _dsa(qa, qi, sm_q, ka, va, ki_srcs, layer=layer, lp=lpad, tq=tq_a, tk=tk, past_len=past_len,
              l_valid=l_valid, n_sel=n_sel, t_valid=t)[:, :t]
    oc = _fox(qc, kc, vc, f_srcs, f_row, layer=layer, lp=lpad, tq=tq_a, tk=tk, past_len=past_len)[:, :t]
    od = _sb(qd, kd, vd, layer=layer, lp=lpad, tq=tq_d, tk=tk_d, past_len=past_len)[:, :t]

    x = _merge(x, mod, lp["g1"], oa, ob, oc, od, lp["w_gate"], lp["b_gate"], lp["w_branch"], lp["w_out"],
               nb=nb, tt=tt)
    x, conv_state = _ffn(x, mod, lp["g2"], prefix, lp["w_up"], lp["w_conv"], lp["b_conv"], lp["w_down"],
                         final_g, nb=nb, tt=tt, final_norm=final_norm)
    if is_prompt:
        rows = (sm, conv_state)
    else:
        heads = lambda a: jnp.transpose(a.reshape(N_HEADS, HEAD_DIM, bsz, t), (2, 3, 0, 1))
        small = lambda a, r0, r1: jnp.transpose(a[r0:r1].reshape(r1 - r0, bsz, t), (1, 2, 0))
        ka_n, va_n, kc_n, vc_n, kd_n, vd_n = kv
        rows = (heads(ka_n), heads(va_n), small(sm, SLAB_KI, SLAB_KI + IDX_DIM), vb, heads(kc_n), heads(vc_n),
                small(sm, SLAB_F, SLAB_F + N_HEADS), heads(kd_n), heads(vd_n), conv_state)
    return x, rows, kv


def kernel(x_prompt, x_sample, cache_a_k, cache_a_v, cache_a_kidx, cache_c_k, cache_c_v, cache_c_logf, cache_d_k, cache_d_v, state_ffn_conv, c_prompt, c_sample, norm1_g, norm2_g, w_mod, b_mod, w_in, f_bias, lnv_g, lnv_b, w_spatial, b_spatial, w_branch, w_gate, b_gate, w_out, w_up, w_conv, b_conv, w_down, final_g):
    depth = w_in.shape[0]
    bsz, t_prompt, d = x_prompt.shape
    dbsz = x_sample.shape[0]
    past_len = cache_a_k.shape[2]
    mods = _modulation(jnp.concatenate([c_prompt, c_sample], axis=0), w_mod, b_mod)
    mods = mods.reshape(depth, bsz + dbsz, 6, d)
    prefix = jnp.zeros((bsz, CONV_W - 1, w_up.shape[-1]), x_prompt.dtype)
    fg = final_g.reshape(1, d)
    past = tuple(_channel_major(c) for c in (cache_a_k, cache_a_v, cache_a_kidx, cache_c_k, cache_c_v,
                                              cache_c_logf.astype(F32), cache_d_k, cache_d_v))
    w_in_t = jnp.transpose(w_in, (2, 0, 1))
    xp, xs = x_prompt, x_sample
    rows_p, rows_s = [], []
    stacked = ()
    for l in range(depth):
        fb = jnp.zeros((SLAB, 1), F32).at[SLAB_F:SLAB_F + N_HEADS, 0].set(f_bias[l].astype(F32))
        lp = dict(g1=norm1_g[l].reshape(1, d), g2=norm2_g[l].reshape(1, d), w_t=_layout_w_in(w_in_t, l), fb=fb,
                  lng=lnv_g[l].reshape(BRANCH, 1), lnb=lnv_b[l].reshape(BRANCH, 1),
                  w_s=w_spatial[l], b_s=b_spatial[l], w_branch=w_branch[l].astype(BF16),
                  w_gate=w_gate[l].astype(BF16), b_gate=b_gate[l].reshape(1, -1), w_out=w_out[l].astype(BF16),
                  w_up=w_up[l].astype(BF16), w_conv=w_conv[l], b_conv=b_conv[l].reshape(1, -1),
                  w_down=w_down[l].astype(BF16))
        last = l == depth - 1
        xp, new_p, stacked = _layer(xp, mods[l, :bsz], lp, None, prefix, fg, stacked, layer=l, depth=depth,
                                    final_norm=last, past_len=0)
        xs, new_s, _ = _layer(xs, mods[l, bsz:], lp, past, state_ffn_conv[l], fg, None, layer=l, depth=depth,
                              final_norm=last, past_len=past_len)
        rows_p.append(new_p)
        rows_s.append(new_s)

    def stacked_s(i):
        return jnp.stack([r[i] for r in rows_s], axis=0)

    def heads_p(a):
        return jnp.transpose(a.reshape(depth, bsz, N_HEADS, HEAD_DIM, t_prompt), (0, 1, 4, 2, 3))

    sm_p = jnp.stack([r[0] for r in rows_p], axis=0)
    small_p = lambda r0, r1: jnp.swapaxes(sm_p[:, :, r0:r1, :], 2, 3)
    ka_p, va_p, kc_p, vc_p, kd_p, vd_p = (heads_p(a) for a in stacked)
    conv_p = jnp.stack([r[1] for r in rows_p], axis=0)
    return (xp, xs,
            ka_p, stacked_s(0),
            va_p, stacked_s(1),
            small_p(SLAB_KI, SLAB_KI + IDX_DIM), stacked_s(2),
            stacked_s(3),
            kc_p, stacked_s(4),
            vc_p, stacked_s(5),
            small_p(SLAB_F, SLAB_F + N_HEADS), stacked_s(6),
            kd_p, stacked_s(7),
            vd_p, stacked_s(8),
            conv_p, stacked_s(9))
```

```python
import functools

import jax
import jax.numpy as jnp
from jax import lax
from jax.experimental import pallas as pl
from jax.experimental.pallas import tpu as pltpu

F32 = jnp.float32
BF16 = jnp.bfloat16
I32 = jnp.int32
I16 = jnp.int16

HEAD_DIM = 64
N_HEADS = 4
BRANCH = N_HEADS * HEAD_DIM
IDX_HEADS = 8
IDX_DIM = 32
CHUNK = 64
TOPK_MAX = 256
B_CHUNK = 128
N_BRANCH = 4
CONV_W = 3
ROPE_THETA = 10000.0
EPS = 1e-6
LOG2E = 1.4426950408889634
Q_SCALE = HEAD_DIM ** -0.5 * LOG2E

SLAB = 128
SLAB_KI = 0
SLAB_W = 32
SLAB_F = 40
AUG = 128
F_ROWS = 8
F_BLK = 256
N_T_GROUPS = 10
NEG = -1e30
INT_MIN = -2 ** 31
INT_MAX = 2 ** 31 - 1
I16_MIN = -2 ** 15
SB_CUTOFF = 220.0
SB_SATURATE = 100.0
Q_PAD = 128
SUB = 8
TOKEN_TILE = 512
ATT_TQ, ATT_TK = 512, 512
SB_TQ, SB_TK = 256, 256
MOD_TN = 1536
FFN_CW = 256
HALF_BITS = 16
VMEM_LIMIT = 56 * 1024 * 1024


def _params(n_axes, vmem=VMEM_LIMIT):
    return pltpu.CompilerParams(dimension_semantics=("arbitrary",) * n_axes, vmem_limit_bytes=vmem)


def _dot(a, b):
    return jnp.dot(a, b, preferred_element_type=F32)


def _split2(x):
    hi = x.astype(BF16)
    lo = (x - hi.astype(F32)).astype(BF16)
    return hi, lo


def _split3(x):
    x1 = x.astype(BF16)
    r = x - x1.astype(F32)
    x2 = r.astype(BF16)
    x3 = (r - x2.astype(F32)).astype(BF16)
    return x1, x2, x3


def _softplus(z):
    return jnp.maximum(z, 0.0) + jnp.log1p(jnp.exp(-jnp.abs(z)))


def _rms(x, g):
    ms = jnp.mean(x * x, axis=-1, keepdims=True)
    return x * lax.rsqrt(ms + EPS) * g


def _rope_t(x, cos, sin_signed, half):
    rows = x.shape[0]
    row = lax.broadcasted_iota(I32, x.shape, 0)
    fwd = pltpu.roll(x, half, axis=0)
    bwd = pltpu.roll(x, rows - half, axis=0)
    rot = jnp.where((row % (2 * half)) < half, bwd, fwd)
    return x * cos + rot * sin_signed


def _mod_kernel(c_ref, w_ref, b_ref, o_ref):
    c = c_ref[...]
    a = c * jax.nn.sigmoid(c)
    a1, a2 = _split2(a)
    w1, w2 = _split2(w_ref[0])
    o_ref[0] = _dot(a1, w1) + _dot(a1, w2) + _dot(a2, w1) + b_ref[0]


def _modulation(c_all, w_mod, b_mod):
    depth, d, n = w_mod.shape
    rows = c_all.shape[0]
    tn = MOD_TN
    return pl.pallas_call(
        _mod_kernel,
        grid=(depth, n // tn),
        in_specs=[pl.BlockSpec((rows, d), lambda l, j: (0, 0)),
                  pl.BlockSpec((1, d, tn), lambda l, j: (l, 0, j)),
                  pl.BlockSpec((1, 1, tn), lambda l, j: (l, 0, j))],
        out_specs=pl.BlockSpec((1, rows, tn), lambda l, j: (l, 0, j)),
        out_shape=jax.ShapeDtypeStruct((depth, rows, n), F32),
        compiler_params=_params(2),
        name="modulation",
    )(c_all, w_mod, b_mod.reshape(depth, 1, n))


def _proj_kernel(x_ref, mod_ref, g1_ref, wt_ref, cosa_ref, sina_ref, cosi_ref, sini_ref, coss_ref,
                 sins_ref, fb_ref, lng_ref, lnb_ref, ws_ref, bs_ref, *rest, nb, tt, n_alias):
    (ka_ref, va_ref, kc_ref, vc_ref, kd_ref, vd_ref,
     qa_ref, qc_ref, qd_ref, qi_ref, sm_ref, ob_ref, vb_ref) = rest[n_alias:]
    m = nb * tt
    d = x_ref.shape[-1]
    x = x_ref[...]
    mod = mod_ref[...]
    h = (_rms(x, g1_ref[...]) * (1.0 + mod[:, 1:2, :]) + mod[:, 0:1, :]).reshape(m, d)
    ht = h.T.astype(BF16)

    def col_t(j):
        return _dot(wt_ref[j * BRANCH:(j + 1) * BRANCH, :], ht)

    def put_t(ref, val):
        ref[...] = val.reshape(ref.shape).astype(ref.dtype)

    def put(ref, val):
        ref[...] = val.reshape(nb, tt, val.shape[-1]).astype(ref.dtype)

    cosa, sina = cosa_ref[...], sina_ref[...]
    put_t(qa_ref, _rope_t(col_t(0), cosa, sina, HEAD_DIM // 2) * Q_SCALE)
    put_t(ka_ref, _rope_t(col_t(1), cosa, sina, HEAD_DIM // 2))
    put_t(va_ref, col_t(2))
    put_t(qi_ref, _rope_t(col_t(3), cosi_ref[...], sini_ref[...], IDX_DIM // 2))
    put_t(qc_ref, col_t(4) * Q_SCALE)
    put_t(kc_ref, col_t(5))
    put_t(vc_ref, col_t(6))
    put_t(qd_ref, col_t(7) * Q_SCALE)
    put_t(kd_ref, col_t(8))
    put_t(vd_ref, col_t(9))
    sm = _dot(wt_ref[(N_T_GROUPS + 2) * BRANCH:(N_T_GROUPS + 2) * BRANCH + SLAB, :], ht)
    row = lax.broadcasted_iota(I32, sm.shape, 0)
    roped = _rope_t(sm, coss_ref[...], sins_ref[...], IDX_DIM // 2)
    logf = -_softplus(-(sm + fb_ref[...]))
    is_f = jnp.where(row >= SLAB_F, jnp.where(row < SLAB_F + N_HEADS, 1, 0), 0)
    put_t(sm_ref, jnp.where(is_f == 1, logf, roped))
    u = jax.nn.gelu(col_t(N_T_GROUPS))
    vg = jax.nn.gelu(col_t(N_T_GROUPS + 1))
    mu = jnp.mean(vg, axis=0, keepdims=True)
    vc_ = vg - mu
    var = jnp.mean(vc_ * vc_, axis=0, keepdims=True)
    vb = vc_ * lax.rsqrt(var + EPS) * lng_ref[...] + lnb_ref[...]
    put(vb_ref, vb.T)
    vbb = vb.astype(BF16)
    mixed = jnp.concatenate([_dot(vbb[g * HEAD_DIM:(g + 1) * HEAD_DIM, :], ws_ref[g]) for g in range(N_HEADS)],
                            axis=0)
    put(ob_ref, (u * (mixed + bs_ref[...])).T)


def _proj(x, mod, g1, w_t, tables, fb, lng, lnb, ws, bs, *, nb, tt, stacked, layer, depth):
    bsz, t, d = x.shape
    m = nb * tt
    n_t = t // tt
    tok = lambda w: pl.BlockSpec((nb, tt, w), lambda ti, bi: (bi, ti, 0))
    const = lambda a: pl.BlockSpec(a.shape, lambda ti, bi: (0,) * a.ndim)
    tab = lambda c: pl.BlockSpec((c, m), lambda ti, bi: (0, ti))
    if stacked is None:
        assert n_t == 1 and nb == bsz
        kv_spec = pl.BlockSpec((BRANCH, m), lambda ti, bi: (0, 0))
        kv_shape = jax.ShapeDtypeStruct((BRANCH, m), F32)
        q_spec = lambda c: pl.BlockSpec((c, m), lambda ti, bi: (0, 0))
        q_shape = lambda c, dt: jax.ShapeDtypeStruct((c, m), dt)
        alias_in = ()
    else:
        assert nb == 1
        kv_spec = pl.BlockSpec((1, 1, BRANCH, tt), lambda ti, bi: (layer, bi, 0, ti))
        kv_shape = jax.ShapeDtypeStruct((depth, bsz, BRANCH, t), F32)
        q_spec = lambda c: pl.BlockSpec((1, c, tt), lambda ti, bi: (bi, 0, ti))
        q_shape = lambda c, dt: jax.ShapeDtypeStruct((bsz, c, t), dt)
        alias_in = tuple(stacked)
    n_alias = len(alias_in)
    n_in = 15
    cosa, sina, cosi, sini, coss, sins = tables
    return pl.pallas_call(
        functools.partial(_proj_kernel, nb=nb, tt=tt, n_alias=n_alias),
        grid=(n_t, bsz // nb),
        in_specs=[tok(d), pl.BlockSpec((nb, 6, d), lambda ti, bi: (bi, 0, 0)), const(g1), const(w_t),
                  tab(BRANCH), tab(BRANCH), tab(BRANCH), tab(BRANCH), tab(SLAB), tab(SLAB),
                  const(fb), const(lng), const(lnb), const(ws), const(bs)]
                 + [pl.BlockSpec(memory_space=pl.ANY)] * n_alias,
        out_specs=[kv_spec] * 6 + [q_spec(BRANCH)] * 4 + [q_spec(SLAB), tok(BRANCH), tok(BRANCH)],
        out_shape=[kv_shape] * 6 + [q_shape(BRANCH, BF16)] * 3 + [q_shape(BRANCH, F32), q_shape(SLAB, F32),
                                                                 jax.ShapeDtypeStruct((bsz, t, BRANCH), BF16),
                                                                 jax.ShapeDtypeStruct((bsz, t, BRANCH), F32)],
        input_output_aliases={n_in + i: i for i in range(n_alias)},
        compiler_params=_params(2),
        name="proj",
    )(x, mod, g1, w_t, cosa, sina, cosi, sini, coss, sins, fb, lng, lnb, ws, bs, *alias_in)


def _load2d(ref):
    return ref[(0,) * (len(ref.shape) - 2)]


def _fill_token_major(dst_sc, srcs):
    off = 0
    for s in srcs:
        n = s.shape[1]
        dst_sc[off:off + n, :] = s.T.astype(BF16)
        off += n
    lp = dst_sc.shape[0]
    if off < lp:
        dst_sc[off:lp, :] = jnp.zeros((lp - off, dst_sc.shape[1]), BF16)


def _fill_channel_major(dst_sc, srcs):
    off = 0
    for s in srcs:
        n = s.shape[1]
        dst_sc[:, off:off + n] = s.astype(BF16)
        off += n
    lp = dst_sc.shape[1]
    if off < lp:
        dst_sc[:, off:lp] = jnp.zeros((dst_sc.shape[0], lp - off), BF16)


def _flash_init(tq):
    return tuple((jnp.full((1, tq), NEG, F32), jnp.zeros((1, tq), F32), jnp.zeros((HEAD_DIM, tq), F32))
                 for _ in range(N_HEADS))


def _flash_step(carry, score_fns, vt_sc, off, tk):
    scores = [fn() for fn in score_fns]
    ms = [jnp.maximum(carry[h][0], jnp.max(scores[h], axis=0, keepdims=True)) for h in range(N_HEADS)]
    ps = [jnp.exp2(scores[h] - ms[h]) for h in range(N_HEADS)]
    pvs = [_dot(vt_sc[h * HEAD_DIM:(h + 1) * HEAD_DIM, pl.ds(off, tk)], ps[h].astype(BF16))
           for h in range(N_HEADS)]
    new = []
    for h in range(N_HEADS):
        m_run, l_run, acc = carry[h]
        alpha = jnp.exp2(m_run - ms[h])
        new.append((ms[h], alpha * l_run + jnp.sum(ps[h], axis=0, keepdims=True), alpha * acc + pvs[h]))
    return tuple(new)


def _flash_finish(carry):
    out_t = jnp.concatenate([acc / l_run for (_, l_run, acc) in carry], axis=0)
    return out_t.T


def _tree_sum(terms):
    while len(terms) > 1:
        terms = [a + b for a, b in zip(terms[::2], terms[1::2])] + ([terms[-1]] if len(terms) % 2 else [])
    return terms[0]


def _masked_heads_t(q_t):
    head = lax.broadcasted_iota(I32, q_t.shape, 0) // HEAD_DIM
    return [jnp.where(head == h, q_t, jnp.zeros_like(q_t)) for h in range(N_HEADS)]


def _src_spec(arr, layer):
    if arr.ndim == 4:
        return pl.BlockSpec((1, 1) + arr.shape[2:], lambda b, i: (layer, b, 0, 0))
    return pl.BlockSpec((1,) + arr.shape[1:], lambda b, i: (b, 0, 0))


def _q_spec(c, tq):
    return pl.BlockSpec((1, c, tq), lambda b, i: (b, 0, i))


def _o_spec(tq):
    return pl.BlockSpec((1, tq, BRANCH), lambda b, i: (b, i, 0))


def _dsa_kernel(qa_ref, qi_ref, qsm_ref, *refs, n_src, tq, tk, past_len, l_valid, n_sel, t_valid, single_q):
    k_refs, v_refs, ki_refs = refs[:n_src], refs[n_src:2 * n_src], refs[2 * n_src:3 * n_src]
    o_ref, kb_sc, vt_sc, kic_sc, lhs_sc, key_sc, hi_sc, lo_sc, thr_sc, room_sc, tie_sc = refs[3 * n_src:]
    qb = pl.program_id(1)

    @pl.when(qb == 0)
    def _():
        _fill_token_major(kb_sc, [_load2d(r) for r in k_refs])
        _fill_channel_major(vt_sc, [_load2d(r) for r in v_refs])
        cats = []
        for r in ki_refs:
            ki = _load2d(r)[SLAB_KI:SLAB_KI + IDX_DIM, :]
            hi = ki.astype(BF16).astype(F32)
            cats.append(jnp.concatenate([hi, ki - hi, hi, jnp.zeros_like(hi)], axis=0))
        _fill_token_major(kic_sc, cats)

    qi = qi_ref[0]
    for h in range(IDX_HEADS):
        piece = qi[h * IDX_DIM:(h + 1) * IDX_DIM, :]
        hi = piece.astype(BF16).astype(F32)
        lhs_sc[h] = jnp.concatenate([hi, hi, piece - hi, jnp.zeros_like(hi)], axis=0).astype(BF16)
    w_t = qsm_ref[0]

    qlane = lax.broadcasted_iota(I32, (1, tq), 1)
    if single_q:
        row0 = past_len
        top_limit = min(((row0 + tq - 1) // CHUNK + 1) * CHUNK, l_valid)
        count_loop = functools.partial(lax.fori_loop, unroll=True)
        block_loop = functools.partial(lax.fori_loop, unroll=2)
    else:
        row0 = past_len + qb * tq
        top_limit = jnp.minimum(((row0 + tq - 1) // CHUNK + 1) * CHUNK, l_valid)
        count_loop = block_loop = lax.fori_loop
    pos = row0 + qlane
    limit = jnp.minimum((pos // CHUNK + 1) * CHUNK, l_valid)
    nblk = (top_limit + tk - 1) // tk

    def score_blk(j, _):
        off = pl.multiple_of(j * tk, tk)
        kc = kic_sc[pl.ds(off, tk), :]
        dots = [_dot(kc, lhs_sc[h]) for h in range(IDX_HEADS)]
        acc = jnp.zeros((tk, tq), F32)
        for h in range(IDX_HEADS):
            acc = acc + w_t[SLAB_W + h:SLAB_W + h + 1, :] * jnp.maximum(dots[h], 0.0)
        acc = jnp.where(acc == 0.0, 0.0, acc)
        bits = pltpu.bitcast(acc, I32)
        key = jnp.where(bits < 0, bits ^ INT_MAX, bits)
        kidx = off + lax.broadcasted_iota(I32, (tk, tq), 0)
        key = jnp.where(kidx < limit, key, INT_MIN)
        key_sc[pl.ds(off, tk), :] = key
        hi_sc[pl.ds(off, tk), :] = (key >> HALF_BITS).astype(I16)
        lo_sc[pl.ds(off, tk), :] = ((key & (2 ** HALF_BITS - 1)) + I16_MIN).astype(I16)
        return 0

    block_loop(0, nblk, score_blk, 0)

    def count16(ref, cand):
        cand_b = jnp.broadcast_to(cand, (16, tq)).astype(I16)

        def body(j, part):
            off = pl.multiple_of(j * tk, tk)
            kb = ref[pl.ds(off, tk), :]
            return part + _tree_sum([jnp.where(kb[c * 16:(c + 1) * 16, :] >= cand_b, jnp.int16(1), jnp.int16(0))
                                     for c in range(tk // 16)])

        part = count_loop(0, nblk, body, jnp.zeros((16, tq), I16))
        return jnp.sum(part.astype(F32), axis=0, keepdims=True)

    def count32(cand):
        cand_b = jnp.broadcast_to(cand, (8, tq))

        def body(j, part):
            off = pl.multiple_of(j * tk, tk)
            kb = key_sc[pl.ds(off, tk), :]
            return part + _tree_sum([jnp.where(kb[c * 8:(c + 1) * 8, :] >= cand_b, 1.0, 0.0)
                                     for c in range(tk // 8)])

        part = count_loop(0, nblk, body, jnp.zeros((8, tq), F32))
        return jnp.sum(part, axis=0, keepdims=True)

    def kth16(ref, want):
        t = jnp.where(count16(ref, jnp.zeros((1, tq), I32)) >= want, 0, I16_MIN).astype(I32)

        def search(i, t):
            cand = t | jnp.left_shift(jnp.int32(1), HALF_BITS - 2 - i)
            return jnp.where(count16(ref, cand) >= want, cand, t)

        return lax.fori_loop(0, HALF_BITS - 1, search, t)

    kf = float(n_sel)
    thr_sc[...] = jnp.full(thr_sc.shape, INT_MIN + 1, I32)
    room_sc[...] = jnp.zeros(room_sc.shape, F32)
    tie_sc[...] = jnp.zeros(tie_sc.shape, F32)

    @pl.when(top_limit > n_sel)
    def _():
        t_hi = kth16(hi_sc, kf)
        above = jnp.where(t_hi >= -I16_MIN - 1, 0.0, count16(hi_sc, t_hi + 1))
        t_hi_b = jnp.broadcast_to(t_hi, (16, tq)).astype(I16)

        def keep_equal_hi(j, _):
            off = pl.multiple_of(j * tk, tk)
            hi_blk = hi_sc[pl.ds(off, tk), :]
            lo_blk = lo_sc[pl.ds(off, tk), :]
            kept = [jnp.where(hi_blk[c * 16:(c + 1) * 16, :] == t_hi_b, lo_blk[c * 16:(c + 1) * 16, :],
                              jnp.int16(I16_MIN)) for c in range(tk // 16)]
            lo_sc[pl.ds(off, tk), :] = jnp.concatenate(kept, axis=0)
            return 0

        count_loop(0, nblk, keep_equal_hi, 0)
        t_lo = kth16(lo_sc, kf - above)
        t_full = jnp.maximum(t_hi * 2 ** HALF_BITS + (t_lo - I16_MIN), INT_MIN + 1)
        cnt_ge = count32(t_full)
        cnt_gt = count32(t_full + 1)
        real_q = qlane < (t_valid - qb * tq)
        thr_sc[...] = jnp.broadcast_to(t_full, thr_sc.shape)
        room_sc[...] = jnp.broadcast_to(kf - cnt_gt, room_sc.shape)
        tie_sc[...] = jnp.broadcast_to(jnp.where(real_q, jnp.where(cnt_ge > kf, 1.0, 0.0), 0.0), tie_sc.shape)

    thr = thr_sc[0:1, :]
    room = room_sc[0:1, :]
    any_tie = jnp.max(tie_sc[0:1, :]) > 0.0

    @pl.when(any_tie)
    def _():
        ri = lax.broadcasted_iota(I32, (tk, tk), 0)
        ci = lax.broadcasted_iota(I32, (tk, tk), 1)
        upto = jnp.where(ci <= ri, 1.0, 0.0).astype(BF16)

        def drop_late_ties(j, seen):
            off = pl.multiple_of(j * tk, tk)
            kb = key_sc[pl.ds(off, tk), :]
            eq = jnp.where(kb == thr, 1.0, 0.0)
            rank = _dot(upto, eq.astype(BF16)) + seen
            key_sc[pl.ds(off, tk), :] = jnp.where(eq * rank > room, INT_MIN, kb)
            return seen + jnp.sum(eq, axis=0, keepdims=True)

        lax.fori_loop(0, nblk, drop_late_ties, jnp.zeros((1, tq), F32))

    qh = _masked_heads_t(qa_ref[0])

    def attend(j, carry):
        off = pl.multiple_of(j * tk, tk)
        kblk = kb_sc[pl.ds(off, tk), :]
        keep = key_sc[pl.ds(off, tk), :] >= thr
        score_fns = [functools.partial(lambda h: jnp.where(keep, _dot(kblk, qh[h]), NEG), h) for h in range(N_HEADS)]
        return _flash_step(carry, score_fns, vt_sc, off, tk)

    carry = block_loop(0, nblk, attend, _flash_init(tq))
    o_ref[0] = _flash_finish(carry).astype(o_ref.dtype)


def _dsa(qa, qi, qsm, k_srcs, v_srcs, ki_srcs, *, layer, lp, tq, tk, past_len, l_valid, n_sel, t_valid):
    bsz, _, t = qa.shape
    n_src = len(k_srcs)
    srcs = list(k_srcs) + list(v_srcs) + list(ki_srcs)
    return pl.pallas_call(
        functools.partial(_dsa_kernel, n_src=n_src, tq=tq, tk=tk, past_len=past_len, l_valid=l_valid,
                          n_sel=n_sel, t_valid=t_valid, single_q=(t == tq)),
        grid=(bsz, t // tq),
        in_specs=[_q_spec(BRANCH, tq), _q_spec(BRANCH, tq), _q_spec(SLAB, tq)] + [_src_spec(a, layer) for a in srcs],
        out_specs=_o_spec(tq),
        out_shape=jax.ShapeDtypeStruct((bsz, t, BRANCH), BF16),
        scratch_shapes=[pltpu.VMEM((lp, BRANCH), BF16), pltpu.VMEM((BRANCH, lp), BF16),
                        pltpu.VMEM((lp, SLAB), BF16), pltpu.VMEM((IDX_HEADS, SLAB, tq), BF16),
                        pltpu.VMEM((lp, tq), I32), pltpu.VMEM((lp, tq), I16), pltpu.VMEM((lp, tq), I16),
                        pltpu.VMEM((SUB, tq), I32), pltpu.VMEM((SUB, tq), F32), pltpu.VMEM((SUB, tq), F32)],
        compiler_params=_params(2),
        name="dsa",
    )(qa, qi, qsm, *srcs)


def _fox_kernel(q_ref, *refs, n_src, f_row, tq, tk, past_len, single_q):
    k_refs, v_refs, f_refs = refs[:n_src], refs[n_src:2 * n_src], refs[2 * n_src:3 * n_src]
    o_ref, vt_sc, kaug_sc, f_sc = refs[3 * n_src:]
    qb = pl.program_id(1)
    lp = f_sc.shape[1]

    @pl.when(qb == 0)
    def _():
        _fill_channel_major(vt_sc, [_load2d(r) for r in v_refs])
        ks = [_load2d(r) for r in k_refs]
        fs = []
        for r, row in zip(f_refs, f_row):
            logf = _load2d(r)
            if row + F_ROWS <= logf.shape[0]:
                fs.append(logf[row:row + F_ROWS, :])
            else:
                gates = logf[row:row + N_HEADS, :]
                fs.append(jnp.concatenate([gates, jnp.zeros((F_ROWS - N_HEADS, gates.shape[1]), F32)], axis=0))
        kt = ks[0] if len(ks) == 1 else jnp.concatenate(ks, axis=1)
        ft = fs[0] if len(fs) == 1 else jnp.concatenate(fs, axis=1)
        n_real = kt.shape[1]
        ri = lax.broadcasted_iota(I32, (F_BLK, F_BLK), 0)
        ci = lax.broadcasted_iota(I32, (F_BLK, F_BLK), 1)
        upto = jnp.where(ri <= ci, 1.0, 0.0).astype(BF16)
        rowx = lax.broadcasted_iota(I32, (HEAD_DIM, F_BLK), 0)
        run = jnp.zeros((F_ROWS, 1), F32)
        for b in range(lp // F_BLK):
            lo, hi_ = b * F_BLK, min((b + 1) * F_BLK, n_real)
            if hi_ <= lo:
                kaug_sc[lo:lo + F_BLK, :] = jnp.zeros((F_BLK, N_HEADS * AUG), BF16)
                f_sc[:, lo:lo + F_BLK] = jnp.broadcast_to(run * LOG2E, (F_ROWS, F_BLK))
                continue
            k_blk, f_blk = kt[:, lo:hi_], ft[:, lo:hi_]
            if hi_ - lo < F_BLK:
                k_blk = jnp.concatenate([k_blk, jnp.zeros((BRANCH, F_BLK - (hi_ - lo)), F32)], axis=1)
                f_blk = jnp.concatenate([f_blk, jnp.zeros((F_ROWS, F_BLK - (hi_ - lo)), F32)], axis=1)
            s1, s2, s3 = _split3(f_blk)
            fsum = _dot(s1, upto) + _dot(s2, upto) + _dot(s3, upto) + run
            run = fsum[:, F_BLK - 1:F_BLK]
            fsum = fsum * LOG2E
            f_sc[:, lo:lo + F_BLK] = fsum
            f1, f2, f3 = (p.astype(F32) for p in _split3(fsum))
            for h in range(N_HEADS):
                extra = jnp.where(rowx < 3, 1.0,
                                  jnp.where(rowx == 3, -f1[h:h + 1, :],
                                            jnp.where(rowx == 4, -f2[h:h + 1, :],
                                                      jnp.where(rowx == 5, -f3[h:h + 1, :], 0.0))))
                aug_t = jnp.concatenate([k_blk[h * HEAD_DIM:(h + 1) * HEAD_DIM, :], extra], axis=0)
                kaug_sc[lo:lo + F_BLK, h * AUG:(h + 1) * AUG] = aug_t.T.astype(BF16)

    if single_q:
        row0 = past_len
        f_q = f_sc[:, row0:row0 + tq]
        block_loop = functools.partial(lax.fori_loop, unroll=2)
    else:
        row0 = past_len + qb * tq
        f_q = f_sc[:, pl.ds(pl.multiple_of(row0, Q_PAD), tq)]
        block_loop = lax.fori_loop
    pos = row0 + lax.broadcasted_iota(I32, (1, tq), 1)
    n_full = row0 // tk
    n_all = (row0 + tq + tk - 1) // tk
    q_t = q_ref[0].astype(F32)
    f1, f2, f3 = (p.astype(F32) for p in _split3(f_q))
    rowq = lax.broadcasted_iota(I32, (HEAD_DIM, tq), 0)
    qh = []
    for h in range(N_HEADS):
        extra = jnp.where(rowq == 0, f1[h:h + 1, :],
                          jnp.where(rowq == 1, f2[h:h + 1, :],
                                    jnp.where(rowq == 2, f3[h:h + 1, :], jnp.where(rowq < 6, 1.0, 0.0))))
        qh.append(jnp.concatenate([q_t[h * HEAD_DIM:(h + 1) * HEAD_DIM, :], extra], axis=0).astype(BF16))

    def step(j, carry, masked):
        off = pl.multiple_of(j * tk, tk)
        causal = (off + lax.broadcasted_iota(I32, (tk, tq), 0) <= pos) if masked else None

        def score(h):
            s = _dot(kaug_sc[pl.ds(off, tk), h * AUG:(h + 1) * AUG], qh[h])
            return jnp.where(causal, s, NEG) if masked else s

        return _flash_step(carry, [functools.partial(score, h) for h in range(N_HEADS)], vt_sc, off, tk)

    carry = block_loop(0, n_full, lambda j, c: step(j, c, False), _flash_init(tq))
    carry = lax.fori_loop(n_full, n_all, lambda j, c: step(j, c, True), carry)
    o_ref[0] = _flash_finish(carry).astype(o_ref.dtype)


def _fox(q, k_srcs, v_srcs, f_srcs, f_row, *, layer, lp, tq, tk, past_len):
    bsz, _, t = q.shape
    n_src = len(k_srcs)
    srcs = list(k_srcs) + list(v_srcs) + list(f_srcs)
    return pl.pallas_call(
        functools.partial(_fox_kernel, n_src=n_src, f_row=tuple(f_row), tq=tq, tk=tk, past_len=past_len,
                          single_q=(t == tq)),
        grid=(bsz, t // tq),
        in_specs=[_q_spec(BRANCH, tq)] + [_src_spec(a, layer) for a in srcs],
        out_specs=_o_spec(tq),
        out_shape=jax.ShapeDtypeStruct((bsz, t, BRANCH), BF16),
        scratch_shapes=[pltpu.VMEM((BRANCH, lp), BF16), pltpu.VMEM((lp, N_HEADS * AUG), BF16),
                        pltpu.VMEM((F_ROWS, lp), F32)],
        compiler_params=_params(2),
        name="fox",
    )(q, *srcs)


def _sb_kernel(q_ref, *refs, n_src, tq, tk, past_len):
    k_refs, v_refs = refs[:n_src], refs[n_src:2 * n_src]
    o_ref, kb_sc, vt_sc = refs[2 * n_src:]
    qb = pl.program_id(1)

    @pl.when(qb == 0)
    def _():
        _fill_token_major(kb_sc, [_load2d(r) for r in k_refs])
        _fill_channel_major(vt_sc, [_load2d(r) for r in v_refs])

    qh = _masked_heads_t(q_ref[0])
    row0 = past_len + qb * tq
    pos = row0 + lax.broadcasted_iota(I32, (1, tq), 1)
    j_top = (row0 + tq - 1) // tk
    ri = lax.broadcasted_iota(I32, (tk, tk), 0)
    ci = lax.broadcasted_iota(I32, (tk, tk), 1)
    after = jnp.where(ci > ri, 1.0, 0.0).astype(BF16)
    after2 = jnp.concatenate([after, after], axis=1)

    def cond(carry):
        j, state = carry
        live = state[0][0]
        for h in range(1, N_HEADS):
            live = jnp.maximum(live, state[h][0])
        return jnp.logical_and(j >= 0, jnp.max(live) > -SB_CUTOFF)

    def body(carry, masked=False):
        j, state = carry
        off = pl.multiple_of(j * tk, tk)
        kblk = kb_sc[pl.ds(off, tk), :]
        zs = [_dot(kblk, qh[h]) for h in range(N_HEADS)]
        keeps = [-jnp.maximum(jnp.log2(1.0 + jnp.exp2(jnp.minimum(z, SB_SATURATE))), z) for z in zs]
        if masked:
            strict = off + lax.broadcasted_iota(I32, (tk, tq), 0) < pos
            keeps = [jnp.where(strict, kp, 0.0) for kp in keeps]
        laters = []
        for h in range(N_HEADS):
            laters.append(_dot(after2, jnp.concatenate(_split2(keeps[h]), axis=0)) + state[h][0])
        ws = [jnp.exp2(zs[h] + keeps[h] + laters[h]) for h in range(N_HEADS)]
        if masked:
            ws = [jnp.where(strict, w, 0.0) for w in ws]
        pvs = [_dot(vt_sc[h * HEAD_DIM:(h + 1) * HEAD_DIM, pl.ds(off, tk)], ws[h].astype(BF16))
               for h in range(N_HEADS)]
        new = tuple((state[h][0] + jnp.sum(keeps[h], axis=0, keepdims=True), state[h][1] + pvs[h])
                    for h in range(N_HEADS))
        return j - 1, new

    init = tuple((jnp.zeros((1, tq), F32), jnp.zeros((HEAD_DIM, tq), F32)) for _ in range(N_HEADS))
    first = body((j_top, init), masked=True)
    _, state = lax.while_loop(cond, body, first)
    out_t = jnp.concatenate([acc for (_, acc) in state], axis=0)
    o_ref[0] = out_t.T.astype(o_ref.dtype)


def _sb(q, k_srcs, v_srcs, *, layer, lp, tq, tk, past_len):
    bsz, _, t = q.shape
    assert tk % tq == 0 and past_len % tq == 0
    n_src = len(k_srcs)
    srcs = list(k_srcs) + list(v_srcs)
    return pl.pallas_call(
        functools.partial(_sb_kernel, n_src=n_src, tq=tq, tk=tk, past_len=past_len),
        grid=(bsz, t // tq),
        in_specs=[_q_spec(BRANCH, tq)] + [_src_spec(a, layer) for a in srcs],
        out_specs=_o_spec(tq),
        out_shape=jax.ShapeDtypeStruct((bsz, t, BRANCH), BF16),
        scratch_shapes=[pltpu.VMEM((lp, BRANCH), BF16), pltpu.VMEM((BRANCH, lp), BF16)],
        compiler_params=_params(2),
        name="sb",
    )(q, *srcs)


def _merge_kernel(x_ref, mod_ref, g1_ref, oa_ref, ob_ref, oc_ref, od_ref, wg_ref, bg_ref, wb_ref, wo_ref,
                  xo_ref, *, nb, tt):
    m = nb * tt
    d = x_ref.shape[-1]
    x = x_ref[...]
    mod = mod_ref[...]
    h = _rms(x, g1_ref[...]) * (1.0 + mod[:, 1:2, :]) + mod[:, 0:1, :]
    hb = h.reshape(m, d).astype(BF16)
    o_refs = (oa_ref, ob_ref, oc_ref, od_ref)

    def pre(i):
        return (_dot(hb, wg_ref[:, i * d:(i + 1) * d]), _dot(o_refs[i][...].reshape(m, BRANCH), wb_ref[i]))

    merged = None
    cur = pre(0)
    for i in range(N_BRANCH):
        nxt = pre(i + 1) if i + 1 < N_BRANCH else None
        term = jax.nn.sigmoid(cur[0] + bg_ref[:, i * d:(i + 1) * d]) * cur[1]
        merged = term if merged is None else merged + term
        cur = nxt
    y = _dot(merged.astype(BF16), wo_ref[...])
    xo_ref[...] = x + mod[:, 2:3, :] * y.reshape(nb, tt, d)


def _merge(x, mod, g1, oa, ob, oc, od, w_gate, b_gate, w_branch, w_out, *, nb, tt):
    bsz, t, d = x.shape
    tok = lambda w: pl.BlockSpec((nb, tt, w), lambda bi, ti: (bi, ti, 0))
    const = lambda a: pl.BlockSpec(a.shape, lambda bi, ti: (0,) * a.ndim)
    return pl.pallas_call(
        functools.partial(_merge_kernel, nb=nb, tt=tt),
        grid=(bsz // nb, t // tt),
        in_specs=[tok(d), pl.BlockSpec((nb, 6, d), lambda bi, ti: (bi, 0, 0)), const(g1),
                  tok(BRANCH), tok(BRANCH), tok(BRANCH), tok(BRANCH),
                  const(w_gate), const(b_gate), const(w_branch), const(w_out)],
        out_specs=tok(d),
        out_shape=jax.ShapeDtypeStruct((bsz, t, d), F32),
        compiler_params=_params(2),
        name="merge",
    )(x, mod, g1, oa, ob, oc, od, w_gate, b_gate, w_branch, w_out)


def _ffn_kernel(x_ref, mod_ref, g2_ref, pfx_ref, wu_ref, wc_ref, bc_ref, wd_ref, fg_ref,
                xo_ref, st_ref, carry_sc, ext_sc, act_sc, *, nb, tt, cw, final_norm):
    m = nb * tt
    d = x_ref.shape[-1]
    dff = wd_ref.shape[0]
    ti = pl.program_id(1)
    keep = CONV_W - 1

    @pl.when(ti == 0)
    def _():
        carry_sc[...] = jnp.zeros_like(carry_sc)
        carry_sc[:, SUB - keep:, :] = pfx_ref[...]

    x = x_ref[...]
    mod = mod_ref[...]
    h = _rms(x, g2_ref[...]) * (1.0 + mod[:, 4:5, :]) + mod[:, 3:4, :]
    hb = h.reshape(m, d).astype(BF16)

    def up_cols(c0):
        return _dot(hb, wu_ref[:, c0:c0 + cw]).reshape(nb, tt, cw)

    def stage(up, c0, slot):
        ext_sc[slot, :, 0:SUB, :] = carry_sc[:, :, c0:c0 + cw]
        ext_sc[slot, :, SUB:, :] = up
        carry_sc[:, :, c0:c0 + cw] = up[:, tt - SUB:, :]

    def conv(c0, slot):
        wc = wc_ref[:, c0:c0 + cw]
        win = lambda back: ext_sc[slot, :, SUB - back:SUB - back + tt, :]
        out = win(2) * wc[0:1, :] + win(1) * wc[1:2, :] + win(0) * wc[2:3, :] + bc_ref[:, c0:c0 + cw]
        return out.reshape(m, cw)

    n_chunks = dff // cw
    split = (n_chunks + 1) // 2 * cw
    ups = (up_cols(0), up_cols(dff))
    acc = None
    for j in range(n_chunks):
        nxt = (up_cols((j + 1) * cw), up_cols(dff + (j + 1) * cw)) if j + 1 < n_chunks else None
        slot = 2 * (j % 2)
        stage(ups[0], j * cw, slot)
        stage(ups[1], dff + j * cw, slot + 1)
        gate = conv(j * cw, slot)
        val = conv(dff + j * cw, slot + 1)
        act_sc[:, j * cw:(j + 1) * cw] = (gate * jax.nn.sigmoid(gate) * val).astype(BF16)
        if (j + 1) * cw == split:
            acc = _dot(act_sc[:, :split], wd_ref[:split, :])
        ups = nxt
    acc = acc + _dot(act_sc[:, split:], wd_ref[split:, :])
    xo = x + mod[:, 5:6, :] * acc.reshape(nb, tt, d)
    if final_norm:
        xo = _rms(xo, fg_ref[...])
    xo_ref[...] = xo

    @pl.when(ti == pl.num_programs(1) - 1)
    def _():
        st_ref[...] = carry_sc[:, SUB - keep:, :]


def _ffn(x, mod, g2, prefix, w_up, w_conv, b_conv, w_down, final_g, *, nb, tt, final_norm):
    bsz, t, d = x.shape
    dff = w_down.shape[0]
    cw = FFN_CW
    tok = pl.BlockSpec((nb, tt, d), lambda bi, ti: (bi, ti, 0))
    const = lambda a: pl.BlockSpec(a.shape, lambda bi, ti: (0,) * a.ndim)
    state = pl.BlockSpec((nb, CONV_W - 1, 2 * dff), lambda bi, ti: (bi, 0, 0))
    return pl.pallas_call(
        functools.partial(_ffn_kernel, nb=nb, tt=tt, cw=cw, final_norm=final_norm),
        grid=(bsz // nb, t // tt),
        in_specs=[tok, pl.BlockSpec((nb, 6, d), lambda bi, ti: (bi, 0, 0)), const(g2), state,
                  const(w_up), const(w_conv), const(b_conv), const(w_down), const(final_g)],
        out_specs=[tok, state],
        out_shape=[jax.ShapeDtypeStruct((bsz, t, d), F32),
                   jax.ShapeDtypeStruct((bsz, CONV_W - 1, 2 * dff), F32)],
        scratch_shapes=[pltpu.VMEM((nb, SUB, 2 * dff), F32), pltpu.VMEM((4, nb, tt + SUB, cw), F32),
                        pltpu.VMEM((nb * tt, dff), BF16)],
        compiler_params=_params(2),
        name="ffn",
    )(x, mod, g2, prefix, w_up, w_conv, b_conv, w_down, final_g)


def _rope_tables_t(pos):
    def tab(half, heads, pad):
        inv = ROPE_THETA ** (-jnp.arange(half, dtype=F32) / half)
        ang = inv[:, None] * pos.astype(F32)[None, :]
        cos, sin = jnp.cos(ang), jnp.sin(ang)
        cos_t = jnp.tile(jnp.concatenate([cos, cos], axis=0), (heads, 1))
        sin_t = jnp.tile(jnp.concatenate([-sin, sin], axis=0), (heads, 1))
        if pad:
            cos_t = jnp.concatenate([cos_t, jnp.ones((pad, pos.shape[0]), F32)], axis=0)
            sin_t = jnp.concatenate([sin_t, jnp.zeros((pad, pos.shape[0]), F32)], axis=0)
        return cos_t, sin_t

    cosa, sina = tab(HEAD_DIM // 2, N_HEADS, 0)
    cosi, sini = tab(IDX_DIM // 2, IDX_HEADS, 0)
    coss, sins = tab(IDX_DIM // 2, 1, SLAB - IDX_DIM)
    return cosa, sina, cosi, sini, coss, sins


def _layout_w_in(w_in_t, layer):
    sizes = (BRANCH, BRANCH, BRANCH, IDX_HEADS * IDX_DIM, IDX_DIM, IDX_HEADS, BRANCH, BRANCH,
             BRANCH, BRANCH, BRANCH, N_HEADS, BRANCH, BRANCH, BRANCH)
    offs = [0]
    for s in sizes:
        offs.append(offs[-1] + s)
    piece = lambda i: w_in_t[offs[i]:offs[i + 1], layer, :]
    a_q, a_k, a_v, a_qi, a_ki, a_w, b_u, b_v, c_q, c_k, c_v, c_f, d_q, d_k, d_v = (piece(i) for i in range(15))
    pad = jnp.zeros((SLAB - IDX_DIM - IDX_HEADS - N_HEADS, w_in_t.shape[-1]), w_in_t.dtype)
    return jnp.concatenate([a_q, a_k, a_v, a_qi, c_q, c_k, c_v, d_q, d_k, d_v, b_u, b_v,
                            a_ki, a_w, c_f, pad], axis=0).astype(BF16)


def _block_diag_tril_t(w_s, n, reps):
    tri = jnp.tril(jnp.ones((n, n), dtype=bool))
    w = jnp.where(tri[None], w_s[:, :n, :n], 0)
    eye = jnp.eye(reps, dtype=w.dtype)
    return jnp.einsum("ab,gts->gbsat", eye, w).reshape(w.shape[0], reps * n, reps * n).astype(BF16)


def _round_up(a, b):
    return (a + b - 1) // b * b


def _channel_major(cache):
    depth, bsz, length = cache.shape[:3]
    flat = cache.reshape(depth, bsz, length, -1)
    return jnp.swapaxes(flat, 2, 3)


def _layer(x, mod, lp, past, prefix, final_g, stacked, *, layer, depth, final_norm, past_len):
    bsz, t, d = x.shape
    is_prompt = past is None
    n = min(t, B_CHUNK)
    if is_prompt:
        nb, tt = 1, min(t, TOKEN_TILE)
        reps = tt // n
        pos_rows = jnp.arange(t)
    else:
        nb, tt = bsz, t
        reps = (nb * tt) // n
        pos_rows = jnp.tile(past_len + jnp.arange(t), nb)
    m = nb * tt
    tables = _rope_tables_t(pos_rows)
    ws = _block_diag_tril_t(lp["w_s"], n, reps)
    bs = jnp.tile(jnp.repeat(lp["b_s"][:, :n], HEAD_DIM, axis=0), (1, reps))
    fb = jnp.broadcast_to(lp["fb"], (SLAB, m))
    lng = jnp.broadcast_to(lp["lng"], (BRANCH, m))
    lnb = jnp.broadcast_to(lp["lnb"], (BRANCH, m))
    outs = _proj(x, mod, lp["g1"], lp["w_t"], tables, fb, lng, lnb, ws, bs,
                 nb=nb, tt=tt, stacked=stacked if is_prompt else None, layer=layer, depth=depth)
    kv, (qa, qc, qd, qi, sm, ob, vb) = outs[:6], outs[6:]

    l_valid = past_len + t
    tqp = _round_up(t, Q_PAD)
    tk, tk_d = min(ATT_TK, past_len + tqp), min(SB_TK, past_len + tqp)
    tq_a, tq_d = min(ATT_TQ, tqp), min(SB_TQ, tqp)
    if is_prompt:
        lpad = t
        n_sel = min(TOPK_MAX, t // 4)
        ka, va, kc, vc, kd, vd = ([a] for a in kv)
        ki_srcs, f_srcs, f_row = [sm], [sm], (SLAB_F,)
    else:
        lpad = _round_up(past_len + tqp, tk)
        n_sel = min(TOPK_MAX, l_valid // 4)

        def tokens(a):
            a = jnp.swapaxes(a.reshape(a.shape[0], bsz, t), 0, 1)
            return jnp.pad(a, ((0, 0), (0, 0), (0, tqp - t)))

        pk_a, pv_a, pki, pk_c, pv_c, plogf, pk_d, pv_d = past
        qa, qc, qd, qi, sm_tok = (tokens(a) for a in (qa, qc, qd, qi, sm))
        new = [tokens(a) for a in kv]
        ka, va, kc, vc, kd, vd = ([p_, n_] for p_, n_ in zip((pk_a, pv_a, pk_c, pv_c, pk_d, pv_d), new))
        ki_srcs, f_srcs, f_row = [pki, sm_tok], [plogf, sm_tok], (0, SLAB_F)
    sm_q = sm if is_prompt else sm_tok

    oa = _dsa(qa, qi, sm_q, ka, va, ki_srcs, layer=layer, lp=lpad, tq=tq_a, tk=tk, past_len=past_len,
              l_valid=l_valid, n_sel=n_sel, t_valid=t)[:, :t]
    oc = _fox(qc, kc, vc, f_srcs, f_row, layer=layer, lp=lpad, tq=tq_a, tk=tk, past_len=past_len)[:, :t]
    od = _sb(qd, kd, vd, layer=layer, lp=lpad, tq=tq_d, tk=tk_d, past_len=past_len)[:, :t]

    x = _merge(x, mod, lp["g1"], oa, ob, oc, od, lp["w_gate"], lp["b_gate"], lp["w_branch"], lp["w_out"],
               nb=nb, tt=tt)
    x, conv_state = _ffn(x, mod, lp["g2"], prefix, lp["w_up"], lp["w_conv"], lp["b_conv"], lp["w_down"],
                         final_g, nb=nb, tt=tt, final_norm=final_norm)
    if is_prompt:
        rows = (sm, conv_state)
    else:
        heads = lambda a: jnp.transpose(a.reshape(N_HEADS, HEAD_DIM, bsz, t), (2, 3, 0, 1))
        small = lambda a, r0, r1: jnp.transpose(a[r0:r1].reshape(r1 - r0, bsz, t), (1, 2, 0))
        ka_n, va_n, kc_n, vc_n, kd_n, vd_n = kv
        rows = (heads(ka_n), heads(va_n), small(sm, SLAB_KI, SLAB_KI + IDX_DIM), vb, heads(kc_n), heads(vc_n),
                small(sm, SLAB_F, SLAB_F + N_HEADS), heads(kd_n), heads(vd_n), conv_state)
    return x, rows, kv


def kernel(x_prompt, x_sample, cache_a_k, cache_a_v, cache_a_kidx, cache_c_k, cache_c_v, cache_c_logf, cache_d_k, cache_d_v, state_ffn_conv, c_prompt, c_sample, norm1_g, norm2_g, w_mod, b_mod, w_in, f_bias, lnv_g, lnv_b, w_spatial, b_spatial, w_branch, w_gate, b_gate, w_out, w_up, w_conv, b_conv, w_down, final_g):
    depth = w_in.shape[0]
    bsz, t_prompt, d = x_prompt.shape
    dbsz = x_sample.shape[0]
    past_len = cache_a_k.shape[2]
    mods = _modulation(jnp.concatenate([c_prompt, c_sample], axis=0), w_mod, b_mod)
    mods = mods.reshape(depth, bsz + dbsz, 6, d)
    prefix = jnp.zeros((bsz, CONV_W - 1, w_up.shape[-1]), x_prompt.dtype)
    fg = final_g.reshape(1, d)
    past = tuple(_channel_major(c) for c in (cache_a_k, cache_a_v, cache_a_kidx, cache_c_k, cache_c_v,
                                              cache_c_logf.astype(F32), cache_d_k, cache_d_v))
    w_in_t = jnp.transpose(w_in, (2, 0, 1))
    xp, xs = x_prompt, x_sample
    rows_p, rows_s = [], []
    stacked = ()
    for l in range(depth):
        fb = jnp.zeros((SLAB, 1), F32).at[SLAB_F:SLAB_F + N_HEADS, 0].set(f_bias[l].astype(F32))
        lp = dict(g1=norm1_g[l].reshape(1, d), g2=norm2_g[l].reshape(1, d), w_t=_layout_w_in(w_in_t, l), fb=fb,
                  lng=lnv_g[l].reshape(BRANCH, 1), lnb=lnv_b[l].reshape(BRANCH, 1),
                  w_s=w_spatial[l], b_s=b_spatial[l], w_branch=w_branch[l].astype(BF16),
                  w_gate=w_gate[l].astype(BF16), b_gate=b_gate[l].reshape(1, -1), w_out=w_out[l].astype(BF16),
                  w_up=w_up[l].astype(BF16), w_conv=w_conv[l], b_conv=b_conv[l].reshape(1, -1),
                  w_down=w_down[l].astype(BF16))
        last = l == depth - 1
        xp, new_p, stacked = _layer(xp, mods[l, :bsz], lp, None, prefix, fg, stacked, layer=l, depth=depth,
                                    final_norm=last, past_len=0)
        xs, new_s, _ = _layer(xs, mods[l, bsz:], lp, past, state_ffn_conv[l], fg, None, layer=l, depth=depth,
                              final_norm=last, past_len=past_len)
        rows_p.append(new_p)
        rows_s.append(new_s)

    def stacked_s(i):
        return jnp.stack([r[i] for r in rows_s], axis=0)

    def heads_p(a):
        return jnp.transpose(a.reshape(depth, bsz, N_HEADS, HEAD_DIM, t_prompt), (0, 1, 4, 2, 3))

    sm_p = jnp.stack([r[0] for r in rows_p], axis=0)
    small_p = lambda r0, r1: jnp.swapaxes(sm_p[:, :, r0:r1, :], 2, 3)
    ka_p, va_p, kc_p, vc_p, kd_p, vd_p = (heads_p(a) for a in stacked)
    conv_p = jnp.stack([r[1] for r in rows_p], axis=0)
    return (xp, xs,
            ka_p, stacked_s(0),
            va_p, stacked_s(1),
            small_p(SLAB_KI, SLAB_KI + IDX_DIM), stacked_s(2),
            stacked_s(3),
            kc_p, stacked_s(4),
            vc_p, stacked_s(5),
            small_p(SLAB_F, SLAB_F + N_HEADS), stacked_s(6),
            kd_p, stacked_s(7),
            vd_p, stacked_s(8),
            conv_p, stacked_s(9))
```

```python
import functools

import jax
import jax.numpy as jnp
from jax import lax
from jax.experimental import pallas as pl
from jax.experimental.pallas import tpu as pltpu

F32 = jnp.float32
BF16 = jnp.bfloat16
I32 = jnp.int32
I16 = jnp.int16

HEAD_DIM = 64
N_HEADS = 4
BRANCH = N_HEADS * HEAD_DIM
IDX_HEADS = 8
IDX_DIM = 32
CHUNK = 64
TOPK_MAX = 256
B_CHUNK = 128
N_BRANCH = 4
CONV_W = 3
ROPE_THETA = 10000.0
EPS = 1e-6
LOG2E = 1.4426950408889634
Q_SCALE = HEAD_DIM ** -0.5 * LOG2E

SLAB = 128
SLAB_KI = 0
SLAB_W = 32
SLAB_F = 40
AUG = 128
F_ROWS = 8
F_BLK = 256
N_T_GROUPS = 10
NEG = -1e30
INT_MIN = -2 ** 31
INT_MAX = 2 ** 31 - 1
I16_MIN = -2 ** 15
SB_CUTOFF = 220.0
SB_SATURATE = 100.0
Q_PAD = 128
SUB = 8
TOKEN_TILE = 512
ATT_TQ, ATT_TK = 512, 512
SB_TQ, SB_TK = 256, 256
MOD_TN = 1536
FFN_CW = 256
HALF_BITS = 16
VMEM_LIMIT = 56 * 1024 * 1024


def _params(n_axes, vmem=VMEM_LIMIT):
    return pltpu.CompilerParams(dimension_semantics=("arbitrary",) * n_axes, vmem_limit_bytes=vmem)


def _dot(a, b):
    return jnp.dot(a, b, preferred_element_type=F32)


def _split2(x):
    hi = x.astype(BF16)
    lo = (x - hi.astype(F32)).astype(BF16)
    return hi, lo


def _split3(x):
    x1 = x.astype(BF16)
    r = x - x1.astype(F32)
    x2 = r.astype(BF16)
    x3 = (r - x2.astype(F32)).astype(BF16)
    return x1, x2, x3


def _softplus(z):
    return jnp.maximum(z, 0.0) + jnp.log1p(jnp.exp(-jnp.abs(z)))


def _rms(x, g):
    ms = jnp.mean(x * x, axis=-1, keepdims=True)
    return x * lax.rsqrt(ms + EPS) * g


def _rope_t(x, cos, sin_signed, half):
    rows = x.shape[0]
    row = lax.broadcasted_iota(I32, x.shape, 0)
    fwd = pltpu.roll(x, half, axis=0)
    bwd = pltpu.roll(x, rows - half, axis=0)
    rot = jnp.where((row % (2 * half)) < half, bwd, fwd)
    return x * cos + rot * sin_signed


def _mod_kernel(c_ref, w_ref, b_ref, o_ref):
    c = c_ref[...]
    a = c * jax.nn.sigmoid(c)
    a1, a2 = _split2(a)
    w1, w2 = _split2(w_ref[0])
    o_ref[0] = _dot(a1, w1) + _dot(a1, w2) + _dot(a2, w1) + b_ref[0]


def _modulation(c_all, w_mod, b_mod):
    depth, d, n = w_mod.shape
    rows = c_all.shape[0]
    tn = MOD_TN
    return pl.pallas_call(
        _mod_kernel,
        grid=(depth, n // tn),
        in_specs=[pl.BlockSpec((rows, d), lambda l, j: (0, 0)),
                  pl.BlockSpec((1, d, tn), lambda l, j: (l, 0, j)),
                  pl.BlockSpec((1, 1, tn), lambda l, j: (l, 0, j))],
        out_specs=pl.BlockSpec((1, rows, tn), lambda l, j: (l, 0, j)),
        out_shape=jax.ShapeDtypeStruct((depth, rows, n), F32),
        compiler_params=_params(2),
        name="modulation",
    )(c_all, w_mod, b_mod.reshape(depth, 1, n))


def _proj_kernel(x_ref, mod_ref, g1_ref, wt_ref, cosa_ref, sina_ref, cosi_ref, sini_ref, coss_ref,
                 sins_ref, fb_ref, lng_ref, lnb_ref, ws_ref, bs_ref, *rest, nb, tt, n_alias, layer):
    (ka_ref, va_ref, kc_ref, vc_ref, kd_ref, vd_ref,
     qa_ref, qc_ref, qd_ref, qi_ref, sm_ref, ob_ref, vb_ref) = rest[n_alias:]
    m = nb * tt
    d = x_ref.shape[-1]
    x = x_ref[...]
    mod = mod_ref[...]
    h = (_rms(x, g1_ref[...]) * (1.0 + mod[:, 1:2, :]) + mod[:, 0:1, :]).reshape(m, d)
    ht = h.T.astype(BF16)

    def col_t(j):
        return _dot(wt_ref[j * BRANCH:(j + 1) * BRANCH, :], ht)

    def put_t(ref, val):
        if len(ref.shape) == 4 and ref.shape[0] > 1:
            for slot in range(ref.shape[0]):
                piece = val if slot == layer else jnp.zeros_like(val)
                ref[slot] = piece.reshape(ref.shape[1:]).astype(ref.dtype)
        else:
            ref[...] = val.reshape(ref.shape).astype(ref.dtype)

    def put(ref, val):
        ref[...] = val.reshape(nb, tt, val.shape[-1]).astype(ref.dtype)

    cosa, sina = cosa_ref[...], sina_ref[...]
    put_t(qa_ref, _rope_t(col_t(0), cosa, sina, HEAD_DIM // 2) * Q_SCALE)
    put_t(ka_ref, _rope_t(col_t(1), cosa, sina, HEAD_DIM // 2))
    put_t(va_ref, col_t(2))
    put_t(qi_ref, _rope_t(col_t(3), cosi_ref[...], sini_ref[...], IDX_DIM // 2))
    put_t(qc_ref, col_t(4) * Q_SCALE)
    put_t(kc_ref, col_t(5))
    put_t(vc_ref, col_t(6))
    put_t(qd_ref, col_t(7) * Q_SCALE)
    put_t(kd_ref, col_t(8))
    put_t(vd_ref, col_t(9))
    sm = _dot(wt_ref[(N_T_GROUPS + 2) * BRANCH:(N_T_GROUPS + 2) * BRANCH + SLAB, :], ht)
    row = lax.broadcasted_iota(I32, sm.shape, 0)
    roped = _rope_t(sm, coss_ref[...], sins_ref[...], IDX_DIM // 2)
    logf = -_softplus(-(sm + fb_ref[...]))
    is_f = jnp.where(row >= SLAB_F, jnp.where(row < SLAB_F + N_HEADS, 1, 0), 0)
    put_t(sm_ref, jnp.where(is_f == 1, logf, roped))
    u = jax.nn.gelu(col_t(N_T_GROUPS))
    vg = jax.nn.gelu(col_t(N_T_GROUPS + 1))
    mu = jnp.mean(vg, axis=0, keepdims=True)
    vc_ = vg - mu
    var = jnp.mean(vc_ * vc_, axis=0, keepdims=True)
    vb = vc_ * lax.rsqrt(var + EPS) * lng_ref[...] + lnb_ref[...]
    put(vb_ref, vb.T)
    vbb = vb.astype(BF16)
    mixed = jnp.concatenate([_dot(vbb[g * HEAD_DIM:(g + 1) * HEAD_DIM, :], ws_ref[g]) for g in range(N_HEADS)],
                            axis=0)
    put(ob_ref, (u * (mixed + bs_ref[...])).T)


def _proj(x, mod, g1, w_t, tables, fb, lng, lnb, ws, bs, *, nb, tt, stacked, layer, depth):
    bsz, t, d = x.shape
    m = nb * tt
    n_t = t // tt
    tok = lambda w: pl.BlockSpec((nb, tt, w), lambda ti, bi: (bi, ti, 0))
    const = lambda a: pl.BlockSpec(a.shape, lambda ti, bi: (0,) * a.ndim)
    tab = lambda c: pl.BlockSpec((c, m), lambda ti, bi: (0, ti))
    if stacked is None:
        assert n_t == 1 and nb == bsz
        kv_spec = pl.BlockSpec((BRANCH, m), lambda ti, bi: (0, 0))
        kv_shape = jax.ShapeDtypeStruct((BRANCH, m), F32)
        q_spec = lambda c: pl.BlockSpec((c, m), lambda ti, bi: (0, 0))
        q_shape = lambda c, dt: jax.ShapeDtypeStruct((c, m), dt)
        alias_in = ()
    else:
        assert nb == 1
        if len(stacked):
            kv_spec = pl.BlockSpec((1, 1, BRANCH, tt), lambda ti, bi: (layer, bi, 0, ti))
        else:
            kv_spec = pl.BlockSpec((depth, 1, BRANCH, tt), lambda ti, bi: (0, bi, 0, ti))
        kv_shape = jax.ShapeDtypeStruct((depth, bsz, BRANCH, t), F32)
        q_spec = lambda c: pl.BlockSpec((1, c, tt), lambda ti, bi: (bi, 0, ti))
        q_shape = lambda c, dt: jax.ShapeDtypeStruct((bsz, c, t), dt)
        alias_in = tuple(stacked)
    n_alias = len(alias_in)
    n_in = 15
    cosa, sina, cosi, sini, coss, sins = tables
    return pl.pallas_call(
        functools.partial(_proj_kernel, nb=nb, tt=tt, n_alias=n_alias, layer=layer),
        grid=(n_t, bsz // nb),
        in_specs=[tok(d), pl.BlockSpec((nb, 6, d), lambda ti, bi: (bi, 0, 0)), const(g1), const(w_t),
                  tab(BRANCH), tab(BRANCH), tab(BRANCH), tab(BRANCH), tab(SLAB), tab(SLAB),
                  const(fb), const(lng), const(lnb), const(ws), const(bs)]
                 + [pl.BlockSpec(memory_space=pl.ANY)] * n_alias,
        out_specs=[kv_spec] * 6 + [q_spec(BRANCH)] * 4 + [q_spec(SLAB), tok(BRANCH), tok(BRANCH)],
        out_shape=[kv_shape] * 6 + [q_shape(BRANCH, BF16)] * 3 + [q_shape(BRANCH, F32), q_shape(SLAB, F32),
                                                                 jax.ShapeDtypeStruct((bsz, t, BRANCH), BF16),
                                                                 jax.ShapeDtypeStruct((bsz, t, BRANCH), F32)],
        input_output_aliases={n_in + i: i for i in range(n_alias)},
        compiler_params=_params(2),
        name="proj",
    )(x, mod, g1, w_t, cosa, sina, cosi, sini, coss, sins, fb, lng, lnb, ws, bs, *alias_in)


def _load2d(ref):
    return ref[(0,) * (len(ref.shape) - 2)]


def _fill_token_major(dst_sc, srcs):
    off = 0
    for s in srcs:
        n = s.shape[1]
        dst_sc[off:off + n, :] = s.T.astype(BF16)
        off += n
    lp = dst_sc.shape[0]
    if off < lp:
        dst_sc[off:lp, :] = jnp.zeros((lp - off, dst_sc.shape[1]), BF16)


def _fill_channel_major(dst_sc, srcs):
    off = 0
    for s in srcs:
        n = s.shape[1]
        dst_sc[:, off:off + n] = s.astype(BF16)
        off += n
    lp = dst_sc.shape[1]
    if off < lp:
        dst_sc[:, off:lp] = jnp.zeros((dst_sc.shape[0], lp - off), BF16)


def _flash_init(tq):
    return tuple((jnp.full((1, tq), NEG, F32), jnp.zeros((1, tq), F32), jnp.zeros((HEAD_DIM, tq), F32))
                 for _ in range(N_HEADS))


def _flash_step(carry, score_fns, vt_sc, off, tk):
    scores = [fn() for fn in score_fns]
    ms = [jnp.maximum(carry[h][0], jnp.max(scores[h], axis=0, keepdims=True)) for h in range(N_HEADS)]
    ps = [jnp.exp2(scores[h] - ms[h]) for h in range(N_HEADS)]
    pvs = [_dot(vt_sc[h * HEAD_DIM:(h + 1) * HEAD_DIM, pl.ds(off, tk)], ps[h].astype(BF16))
           for h in range(N_HEADS)]
    new = []
    for h in range(N_HEADS):
        m_run, l_run, acc = carry[h]
        alpha = jnp.exp2(m_run - ms[h])
        new.append((ms[h], alpha * l_run + jnp.sum(ps[h], axis=0, keepdims=True), alpha * acc + pvs[h]))
    return tuple(new)


def _flash_finish(carry):
    out_t = jnp.concatenate([acc / l_run for (_, l_run, acc) in carry], axis=0)
    return out_t.T


def _tree_sum(terms):
    while len(terms) > 1:
        terms = [a + b for a, b in zip(terms[::2], terms[1::2])] + ([terms[-1]] if len(terms) % 2 else [])
    return terms[0]


def _masked_heads_t(q_t):
    head = lax.broadcasted_iota(I32, q_t.shape, 0) // HEAD_DIM
    return [jnp.where(head == h, q_t, jnp.zeros_like(q_t)) for h in range(N_HEADS)]


def _src_spec(arr, layer):
    if arr.ndim == 4:
        return pl.BlockSpec((1, 1) + arr.shape[2:], lambda b, i: (layer, b, 0, 0))
    return pl.BlockSpec((1,) + arr.shape[1:], lambda b, i: (b, 0, 0))


def _q_spec(c, tq):
    return pl.BlockSpec((1, c, tq), lambda b, i: (b, 0, i))


def _o_spec(tq):
    return pl.BlockSpec((1, tq, BRANCH), lambda b, i: (b, i, 0))


def _dsa_kernel(qa_ref, qi_ref, qsm_ref, *refs, n_src, tq, tk, past_len, l_valid, n_sel, t_valid, single_q):
    k_refs, v_refs, ki_refs = refs[:n_src], refs[n_src:2 * n_src], refs[2 * n_src:3 * n_src]
    o_ref, kb_sc, vt_sc, kic_sc, lhs_sc, key_sc, hi_sc, lo_sc, thr_sc, room_sc, tie_sc = refs[3 * n_src:]
    qb = pl.program_id(1)

    @pl.when(qb == 0)
    def _():
        _fill_token_major(kb_sc, [_load2d(r) for r in k_refs])
        _fill_channel_major(vt_sc, [_load2d(r) for r in v_refs])
        cats = []
        for r in ki_refs:
            ki = _load2d(r)[SLAB_KI:SLAB_KI + IDX_DIM, :]
            hi = ki.astype(BF16).astype(F32)
            cats.append(jnp.concatenate([hi, ki - hi, hi, jnp.zeros_like(hi)], axis=0))
        _fill_token_major(kic_sc, cats)

    qi = qi_ref[0]
    for h in range(IDX_HEADS):
        piece = qi[h * IDX_DIM:(h + 1) * IDX_DIM, :]
        hi = piece.astype(BF16).astype(F32)
        lhs_sc[h] = jnp.concatenate([hi, hi, piece - hi, jnp.zeros_like(hi)], axis=0).astype(BF16)
    w_t = qsm_ref[0]

    qlane = lax.broadcasted_iota(I32, (1, tq), 1)
    if single_q:
        row0 = past_len
        top_limit = min(((row0 + tq - 1) // CHUNK + 1) * CHUNK, l_valid)
        count_loop = functools.partial(lax.fori_loop, unroll=True)
        block_loop = functools.partial(lax.fori_loop, unroll=2)
    else:
        row0 = past_len + qb * tq
        top_limit = jnp.minimum(((row0 + tq - 1) // CHUNK + 1) * CHUNK, l_valid)
        count_loop = block_loop = lax.fori_loop
    pos = row0 + qlane
    limit = jnp.minimum((pos // CHUNK + 1) * CHUNK, l_valid)
    nblk = (top_limit + tk - 1) // tk

    def score_blk(j, _):
        off = pl.multiple_of(j * tk, tk)
        kc = kic_sc[pl.ds(off, tk), :]
        dots = [_dot(kc, lhs_sc[h]) for h in range(IDX_HEADS)]
        acc = jnp.zeros((tk, tq), F32)
        for h in range(IDX_HEADS):
            acc = acc + w_t[SLAB_W + h:SLAB_W + h + 1, :] * jnp.maximum(dots[h], 0.0)
        acc = jnp.where(acc == 0.0, 0.0, acc)
        bits = pltpu.bitcast(acc, I32)
        key = jnp.where(bits < 0, bits ^ INT_MAX, bits)
        kidx = off + lax.broadcasted_iota(I32, (tk, tq), 0)
        key = jnp.where(kidx < limit, key, INT_MIN)
        key_sc[pl.ds(off, tk), :] = key
        hi_sc[pl.ds(off, tk), :] = (key >> HALF_BITS).astype(I16)
        lo_sc[pl.ds(off, tk), :] = ((key & (2 ** HALF_BITS - 1)) + I16_MIN).astype(I16)
        return 0

    block_loop(0, nblk, score_blk, 0)

    def count16(ref, cand):
        cand_b = jnp.broadcast_to(cand, (16, tq)).astype(I16)

        def body(j, part):
            off = pl.multiple_of(j * tk, tk)
            kb = ref[pl.ds(off, tk), :]
            return part + _tree_sum([jnp.where(kb[c * 16:(c + 1) * 16, :] >= cand_b, jnp.int16(1), jnp.int16(0))
                                     for c in range(tk // 16)])

        part = count_loop(0, nblk, body, jnp.zeros((16, tq), I16))
        return jnp.sum(part.astype(F32), axis=0, keepdims=True)

    def count32(cand):
        cand_b = jnp.broadcast_to(cand, (8, tq))

        def body(j, part):
            off = pl.multiple_of(j * tk, tk)
            kb = key_sc[pl.ds(off, tk), :]
            return part + _tree_sum([jnp.where(kb[c * 8:(c + 1) * 8, :] >= cand_b, 1.0, 0.0)
                                     for c in range(tk // 8)])

        part = count_loop(0, nblk, body, jnp.zeros((8, tq), F32))
        return jnp.sum(part, axis=0, keepdims=True)

    def kth16(ref, want):
        t = jnp.where(count16(ref, jnp.zeros((1, tq), I32)) >= want, 0, I16_MIN).astype(I32)

        def search(i, t):
            cand = t | jnp.left_shift(jnp.int32(1), HALF_BITS - 2 - i)
            return jnp.where(count16(ref, cand) >= want, cand, t)

        return lax.fori_loop(0, HALF_BITS - 1, search, t)

    kf = float(n_sel)
    thr_sc[...] = jnp.full(thr_sc.shape, INT_MIN + 1, I32)
    room_sc[...] = jnp.zeros(room_sc.shape, F32)
    tie_sc[...] = jnp.zeros(tie_sc.shape, F32)

    @pl.when(top_limit > n_sel)
    def _():
        t_hi = kth16(hi_sc, kf)
        above = jnp.where(t_hi >= -I16_MIN - 1, 0.0, count16(hi_sc, t_hi + 1))
        t_hi_b = jnp.broadcast_to(t_hi, (16, tq)).astype(I16)

        def keep_equal_hi(j, _):
            off = pl.multiple_of(j * tk, tk)
            hi_blk = hi_sc[pl.ds(off, tk), :]
            lo_blk = lo_sc[pl.ds(off, tk), :]
            kept = [jnp.where(hi_blk[c * 16:(c + 1) * 16, :] == t_hi_b, lo_blk[c * 16:(c + 1) * 16, :],
                              jnp.int16(I16_MIN)) for c in range(tk // 16)]
            lo_sc[pl.ds(off, tk), :] = jnp.concatenate(kept, axis=0)
            return 0

        count_loop(0, nblk, keep_equal_hi, 0)
        t_lo = kth16(lo_sc, kf - above)
        t_full = jnp.maximum(t_hi * 2 ** HALF_BITS + (t_lo - I16_MIN), INT_MIN + 1)
        cnt_ge = count32(t_full)
        cnt_gt = count32(t_full + 1)
        real_q = qlane < (t_valid - qb * tq)
        thr_sc[...] = jnp.broadcast_to(t_full, thr_sc.shape)
        room_sc[...] = jnp.broadcast_to(kf - cnt_gt, room_sc.shape)
        tie_sc[...] = jnp.broadcast_to(jnp.where(real_q, jnp.where(cnt_ge > kf, 1.0, 0.0), 0.0), tie_sc.shape)

    thr = thr_sc[0:1, :]
    room = room_sc[0:1, :]
    any_tie = jnp.max(tie_sc[0:1, :]) > 0.0

    @pl.when(any_tie)
    def _():
        ri = lax.broadcasted_iota(I32, (tk, tk), 0)
        ci = lax.broadcasted_iota(I32, (tk, tk), 1)
        upto = jnp.where(ci <= ri, 1.0, 0.0).astype(BF16)

        def drop_late_ties(j, seen):
            off = pl.multiple_of(j * tk, tk)
            kb = key_sc[pl.ds(off, tk), :]
            eq = jnp.where(kb == thr, 1.0, 0.0)
            rank = _dot(upto, eq.astype(BF16)) + seen
            key_sc[pl.ds(off, tk), :] = jnp.where(eq * rank > room, INT_MIN, kb)
            return seen + jnp.sum(eq, axis=0, keepdims=True)

        lax.fori_loop(0, nblk, drop_late_ties, jnp.zeros((1, tq), F32))

    qh = _masked_heads_t(qa_ref[0])

    def attend(j, carry):
        off = pl.multiple_of(j * tk, tk)
        kblk = kb_sc[pl.ds(off, tk), :]
        keep = key_sc[pl.ds(off, tk), :] >= thr
        score_fns = [functools.partial(lambda h: jnp.where(keep, _dot(kblk, qh[h]), NEG), h) for h in range(N_HEADS)]
        return _flash_step(carry, score_fns, vt_sc, off, tk)

    carry = block_loop(0, nblk, attend, _flash_init(tq))
    o_ref[0] = _flash_finish(carry).astype(o_ref.dtype)


def _dsa(qa, qi, qsm, k_srcs, v_srcs, ki_srcs, *, layer, lp, tq, tk, past_len, l_valid, n_sel, t_valid):
    bsz, _, t = qa.shape
    n_src = len(k_srcs)
    srcs = list(k_srcs) + list(v_srcs) + list(ki_srcs)
    return pl.pallas_call(
        functools.partial(_dsa_kernel, n_src=n_src, tq=tq, tk=tk, past_len=past_len, l_valid=l_valid,
                          n_sel=n_sel, t_valid=t_valid, single_q=(t == tq)),
        grid=(bsz, t // tq),
        in_specs=[_q_spec(BRANCH, tq), _q_spec(BRANCH, tq), _q_spec(SLAB, tq)] + [_src_spec(a, layer) for a in srcs],
        out_specs=_o_spec(tq),
        out_shape=jax.ShapeDtypeStruct((bsz, t, BRANCH), BF16),
        scratch_shapes=[pltpu.VMEM((lp, BRANCH), BF16), pltpu.VMEM((BRANCH, lp), BF16),
                        pltpu.VMEM((lp, SLAB), BF16), pltpu.VMEM((IDX_HEADS, SLAB, tq), BF16),
                        pltpu.VMEM((lp, tq), I32), pltpu.VMEM((lp, tq), I16), pltpu.VMEM((lp, tq), I16),
                        pltpu.VMEM((SUB, tq), I32), pltpu.VMEM((SUB, tq), F32), pltpu.VMEM((SUB, tq), F32)],
        compiler_params=_params(2),
        name="dsa",
    )(qa, qi, qsm, *srcs)


def _fox_kernel(q_ref, *refs, n_src, f_row, tq, tk, past_len, single_q):
    k_refs, v_refs, f_refs = refs[:n_src], refs[n_src:2 * n_src], refs[2 * n_src:3 * n_src]
    o_ref, vt_sc, kaug_sc, f_sc = refs[3 * n_src:]
    qb = pl.program_id(1)
    lp = f_sc.shape[1]

    @pl.when(qb == 0)
    def _():
        _fill_channel_major(vt_sc, [_load2d(r) for r in v_refs])
        ks = [_load2d(r) for r in k_refs]
        fs = []
        for r, row in zip(f_refs, f_row):
            logf = _load2d(r)
            if row + F_ROWS <= logf.shape[0]:
                fs.append(logf[row:row + F_ROWS, :])
            else:
                gates = logf[row:row + N_HEADS, :]
                fs.append(jnp.concatenate([gates, jnp.zeros((F_ROWS - N_HEADS, gates.shape[1]), F32)], axis=0))
        kt = ks[0] if len(ks) == 1 else jnp.concatenate(ks, axis=1)
        ft = fs[0] if len(fs) == 1 else jnp.concatenate(fs, axis=1)
        n_real = kt.shape[1]
        ri = lax.broadcasted_iota(I32, (F_BLK, F_BLK), 0)
        ci = lax.broadcasted_iota(I32, (F_BLK, F_BLK), 1)
        upto = jnp.where(ri <= ci, 1.0, 0.0).astype(BF16)
        rowx = lax.broadcasted_iota(I32, (HEAD_DIM, F_BLK), 0)
        run = jnp.zeros((F_ROWS, 1), F32)
        for b in range(lp // F_BLK):
            lo, hi_ = b * F_BLK, min((b + 1) * F_BLK, n_real)
            if hi_ <= lo:
                kaug_sc[lo:lo + F_BLK, :] = jnp.zeros((F_BLK, N_HEADS * AUG), BF16)
                f_sc[:, lo:lo + F_BLK] = jnp.broadcast_to(run * LOG2E, (F_ROWS, F_BLK))
                continue
            k_blk, f_blk = kt[:, lo:hi_], ft[:, lo:hi_]
            if hi_ - lo < F_BLK:
                k_blk = jnp.concatenate([k_blk, jnp.zeros((BRANCH, F_BLK - (hi_ - lo)), F32)], axis=1)
                f_blk = jnp.concatenate([f_blk, jnp.zeros((F_ROWS, F_BLK - (hi_ - lo)), F32)], axis=1)
            s1, s2, s3 = _split3(f_blk)
            fsum = _dot(s1, upto) + _dot(s2, upto) + _dot(s3, upto) + run
            run = fsum[:, F_BLK - 1:F_BLK]
            fsum = fsum * LOG2E
            f_sc[:, lo:lo + F_BLK] = fsum
            f1, f2, f3 = (p.astype(F32) for p in _split3(fsum))
            for h in range(N_HEADS):
                extra = jnp.where(rowx < 3, 1.0,
                                  jnp.where(rowx == 3, -f1[h:h + 1, :],
                                            jnp.where(rowx == 4, -f2[h:h + 1, :],
                                                      jnp.where(rowx == 5, -f3[h:h + 1, :], 0.0))))
                aug_t = jnp.concatenate([k_blk[h * HEAD_DIM:(h + 1) * HEAD_DIM, :], extra], axis=0)
                kaug_sc[lo:lo + F_BLK, h * AUG:(h + 1) * AUG] = aug_t.T.astype(BF16)

    if single_q:
        row0 = past_len
        f_q = f_sc[:, row0:row0 + tq]
        block_loop = functools.partial(lax.fori_loop, unroll=2)
    else:
        row0 = past_len + qb * tq
        f_q = f_sc[:, pl.ds(pl.multiple_of(row0, Q_PAD), tq)]
        block_loop = lax.fori_loop
    pos = row0 + lax.broadcasted_iota(I32, (1, tq), 1)
    n_full = row0 // tk
    n_all = (row0 + tq + tk - 1) // tk
    q_t = q_ref[0].astype(F32)
    f1, f2, f3 = (p.astype(F32) for p in _split3(f_q))
    rowq = lax.broadcasted_iota(I32, (HEAD_DIM, tq), 0)
    qh = []
    for h in range(N_HEADS):
        extra = jnp.where(rowq == 0, f1[h:h + 1, :],
                          jnp.where(rowq == 1, f2[h:h + 1, :],
                                    jnp.where(rowq == 2, f3[h:h + 1, :], jnp.where(rowq < 6, 1.0, 0.0))))
        qh.append(jnp.concatenate([q_t[h * HEAD_DIM:(h + 1) * HEAD_DIM, :], extra], axis=0).astype(BF16))

    def step(j, carry, masked):
        off = pl.multiple_of(j * tk, tk)
        causal = (off + lax.broadcasted_iota(I32, (tk, tq), 0) <= pos) if masked else None

        def score(h):
            s = _dot(kaug_sc[pl.ds(off, tk), h * AUG:(h + 1) * AUG], qh[h])
            return jnp.where(causal, s, NEG) if masked else s

        return _flash_step(carry, [functools.partial(score, h) for h in range(N_HEADS)], vt_sc, off, tk)

    carry = block_loop(0, n_full, lambda j, c: step(j, c, False), _flash_init(tq))
    carry = lax.fori_loop(n_full, n_all, lambda j, c: step(j, c, True), carry)
    o_ref[0] = _flash_finish(carry).astype(o_ref.dtype)


def _fox(q, k_srcs, v_srcs, f_srcs, f_row, *, layer, lp, tq, tk, past_len):
    bsz, _, t = q.shape
    n_src = len(k_srcs)
    srcs = list(k_srcs) + list(v_srcs) + list(f_srcs)
    return pl.pallas_call(
        functools.partial(_fox_kernel, n_src=n_src, f_row=tuple(f_row), tq=tq, tk=tk, past_len=past_len,
                          single_q=(t == tq)),
        grid=(bsz, t // tq),
        in_specs=[_q_spec(BRANCH, tq)] + [_src_spec(a, layer) for a in srcs],
        out_specs=_o_spec(tq),
        out_shape=jax.ShapeDtypeStruct((bsz, t, BRANCH), BF16),
        scratch_shapes=[pltpu.VMEM((BRANCH, lp), BF16), pltpu.VMEM((lp, N_HEADS * AUG), BF16),
                        pltpu.VMEM((F_ROWS, lp), F32)],
        compiler_params=_params(2),
        name="fox",
    )(q, *srcs)


def _sb_kernel(q_ref, *refs, n_src, tq, tk, past_len):
    k_refs, v_refs = refs[:n_src], refs[n_src:2 * n_src]
    o_ref, kb_sc, vt_sc = refs[2 * n_src:]
    qb = pl.program_id(1)

    @pl.when(qb == 0)
    def _():
        _fill_token_major(kb_sc, [_load2d(r) for r in k_refs])
        _fill_channel_major(vt_sc, [_load2d(r) for r in v_refs])

    qh = _masked_heads_t(q_ref[0])
    row0 = past_len + qb * tq
    pos = row0 + lax.broadcasted_iota(I32, (1, tq), 1)
    j_top = (row0 + tq - 1) // tk
    ri = lax.broadcasted_iota(I32, (tk, tk), 0)
    ci = lax.broadcasted_iota(I32, (tk, tk), 1)
    after = jnp.where(ci > ri, 1.0, 0.0).astype(BF16)
    after2 = jnp.concatenate([after, after], axis=1)

    def cond(carry):
        j, state = carry
        live = state[0][0]
        for h in range(1, N_HEADS):
            live = jnp.maximum(live, state[h][0])
        return jnp.logical_and(j >= 0, jnp.max(live) > -SB_CUTOFF)

    def body(carry, masked=False):
        j, state = carry
        off = pl.multiple_of(j * tk, tk)
        kblk = kb_sc[pl.ds(off, tk), :]
        zs = [_dot(kblk, qh[h]) for h in range(N_HEADS)]
        keeps = [-jnp.maximum(jnp.log2(1.0 + jnp.exp2(jnp.minimum(z, SB_SATURATE))), z) for z in zs]
        if masked:
            strict = off + lax.broadcasted_iota(I32, (tk, tq), 0) < pos
            keeps = [jnp.where(strict, kp, 0.0) for kp in keeps]
        laters = []
        for h in range(N_HEADS):
            laters.append(_dot(after2, jnp.concatenate(_split2(keeps[h]), axis=0)) + state[h][0])
        ws = [jnp.exp2(zs[h] + keeps[h] + laters[h]) for h in range(N_HEADS)]
        if masked:
            ws = [jnp.where(strict, w, 0.0) for w in ws]
        pvs = [_dot(vt_sc[h * HEAD_DIM:(h + 1) * HEAD_DIM, pl.ds(off, tk)], ws[h].astype(BF16))
               for h in range(N_HEADS)]
        new = tuple((state[h][0] + jnp.sum(keeps[h], axis=0, keepdims=True), state[h][1] + pvs[h])
                    for h in range(N_HEADS))
        return j - 1, new

    init = tuple((jnp.zeros((1, tq), F32), jnp.zeros((HEAD_DIM, tq), F32)) for _ in range(N_HEADS))
    first = body((j_top, init), masked=True)
    _, state = lax.while_loop(cond, body, first)
    out_t = jnp.concatenate([acc for (_, acc) in state], axis=0)
    o_ref[0] = out_t.T.astype(o_ref.dtype)


def _sb(q, k_srcs, v_srcs, *, layer, lp, tq, tk, past_len):
    bsz, _, t = q.shape
    assert tk % tq == 0 and past_len % tq == 0
    n_src = len(k_srcs)
    srcs = list(k_srcs) + list(v_srcs)
    return pl.pallas_call(
        functools.partial(_sb_kernel, n_src=n_src, tq=tq, tk=tk, past_len=past_len),
        grid=(bsz, t // tq),
        in_specs=[_q_spec(BRANCH, tq)] + [_src_spec(a, layer) for a in srcs],
        out_specs=_o_spec(tq),
        out_shape=jax.ShapeDtypeStruct((bsz, t, BRANCH), BF16),
        scratch_shapes=[pltpu.VMEM((lp, BRANCH), BF16), pltpu.VMEM((BRANCH, lp), BF16)],
        compiler_params=_params(2),
        name="sb",
    )(q, *srcs)


def _merge_kernel(x_ref, mod_ref, g1_ref, oa_ref, ob_ref, oc_ref, od_ref, wg_ref, bg_ref, wb_ref, wo_ref,
                  xo_ref, *, nb, tt):
    m = nb * tt
    d = x_ref.shape[-1]
    x = x_ref[...]
    mod = mod_ref[...]
    h = _rms(x, g1_ref[...]) * (1.0 + mod[:, 1:2, :]) + mod[:, 0:1, :]
    hb = h.reshape(m, d).astype(BF16)
    o_refs = (oa_ref, ob_ref, oc_ref, od_ref)

    def pre(i):
        return (_dot(hb, wg_ref[:, i * d:(i + 1) * d]), _dot(o_refs[i][...].reshape(m, BRANCH), wb_ref[i]))

    merged = None
    cur = pre(0)
    for i in range(N_BRANCH):
        nxt = pre(i + 1) if i + 1 < N_BRANCH else None
        term = jax.nn.sigmoid(cur[0] + bg_ref[:, i * d:(i + 1) * d]) * cur[1]
        merged = term if merged is None else merged + term
        cur = nxt
    y = _dot(merged.astype(BF16), wo_ref[...])
    xo_ref[...] = x + mod[:, 2:3, :] * y.reshape(nb, tt, d)


def _merge(x, mod, g1, oa, ob, oc, od, w_gate, b_gate, w_branch, w_out, *, nb, tt):
    bsz, t, d = x.shape
    tok = lambda w: pl.BlockSpec((nb, tt, w), lambda bi, ti: (bi, ti, 0))
    const = lambda a: pl.BlockSpec(a.shape, lambda bi, ti: (0,) * a.ndim)
    return pl.pallas_call(
        functools.partial(_merge_kernel, nb=nb, tt=tt),
        grid=(bsz // nb, t // tt),
        in_specs=[tok(d), pl.BlockSpec((nb, 6, d), lambda bi, ti: (bi, 0, 0)), const(g1),
                  tok(BRANCH), tok(BRANCH), tok(BRANCH), tok(BRANCH),
                  const(w_gate), const(b_gate), const(w_branch), const(w_out)],
        out_specs=tok(d),
        out_shape=jax.ShapeDtypeStruct((bsz, t, d), F32),
        compiler_params=_params(2),
        name="merge",
    )(x, mod, g1, oa, ob, oc, od, w_gate, b_gate, w_branch, w_out)


def _ffn_kernel(x_ref, mod_ref, g2_ref, pfx_ref, wu_ref, wc_ref, bc_ref, wd_ref, fg_ref,
                xo_ref, st_ref, carry_sc, ext_sc, act_sc, *, nb, tt, cw, final_norm):
    m = nb * tt
    d = x_ref.shape[-1]
    dff = wd_ref.shape[0]
    ti = pl.program_id(1)
    keep = CONV_W - 1

    @pl.when(ti == 0)
    def _():
        carry_sc[...] = jnp.zeros_like(carry_sc)
        carry_sc[:, SUB - keep:, :] = pfx_ref[...]

    x = x_ref[...]
    mod = mod_ref[...]
    h = _rms(x, g2_ref[...]) * (1.0 + mod[:, 4:5, :]) + mod[:, 3:4, :]
    hb = h.reshape(m, d).astype(BF16)

    def up_cols(c0):
        return _dot(hb, wu_ref[:, c0:c0 + cw]).reshape(nb, tt, cw)

    def stage(up, c0, slot):
        ext_sc[slot, :, 0:SUB, :] = carry_sc[:, :, c0:c0 + cw]
        ext_sc[slot, :, SUB:, :] = up
        carry_sc[:, :, c0:c0 + cw] = up[:, tt - SUB:, :]

    def conv(c0, slot):
        wc = wc_ref[:, c0:c0 + cw]
        win = lambda back: ext_sc[slot, :, SUB - back:SUB - back + tt, :]
        out = win(2) * wc[0:1, :] + win(1) * wc[1:2, :] + win(0) * wc[2:3, :] + bc_ref[:, c0:c0 + cw]
        return out.reshape(m, cw)

    n_chunks = dff // cw
    split = (n_chunks + 1) // 2 * cw
    ups = (up_cols(0), up_cols(dff))
    acc = None
    for j in range(n_chunks):
        nxt = (up_cols((j + 1) * cw), up_cols(dff + (j + 1) * cw)) if j + 1 < n_chunks else None
        slot = 2 * (j % 2)
        stage(ups[0], j * cw, slot)
        stage(ups[1], dff + j * cw, slot + 1)
        gate = conv(j * cw, slot)
        val = conv(dff + j * cw, slot + 1)
        act_sc[:, j * cw:(j + 1) * cw] = (gate * jax.nn.sigmoid(gate) * val).astype(BF16)
        if (j + 1) * cw == split:
            acc = _dot(act_sc[:, :split], wd_ref[:split, :])
        ups = nxt
    acc = acc + _dot(act_sc[:, split:], wd_ref[split:, :])
    xo = x + mod[:, 5:6, :] * acc.reshape(nb, tt, d)
    if final_norm:
        xo = _rms(xo, fg_ref[...])
    xo_ref[...] = xo

    @pl.when(ti == pl.num_programs(1) - 1)
    def _():
        st_ref[...] = carry_sc[:, SUB - keep:, :]


def _ffn(x, mod, g2, prefix, w_up, w_conv, b_conv, w_down, final_g, *, nb, tt, final_norm):
    bsz, t, d = x.shape
    dff = w_down.shape[0]
    cw = FFN_CW
    tok = pl.BlockSpec((nb, tt, d), lambda bi, ti: (bi, ti, 0))
    const = lambda a: pl.BlockSpec(a.shape, lambda bi, ti: (0,) * a.ndim)
    state = pl.BlockSpec((nb, CONV_W - 1, 2 * dff), lambda bi, ti: (bi, 0, 0))
    return pl.pallas_call(
        functools.partial(_ffn_kernel, nb=nb, tt=tt, cw=cw, final_norm=final_norm),
        grid=(bsz // nb, t // tt),
        in_specs=[tok, pl.BlockSpec((nb, 6, d), lambda bi, ti: (bi, 0, 0)), const(g2), state,
                  const(w_up), const(w_conv), const(b_conv), const(w_down), const(final_g)],
        out_specs=[tok, state],
        out_shape=[jax.ShapeDtypeStruct((bsz, t, d), F32),
                   jax.ShapeDtypeStruct((bsz, CONV_W - 1, 2 * dff), F32)],
        scratch_shapes=[pltpu.VMEM((nb, SUB, 2 * dff), F32), pltpu.VMEM((4, nb, tt + SUB, cw), F32),
                        pltpu.VMEM((nb * tt, dff), BF16)],
        compiler_params=_params(2),
        name="ffn",
    )(x, mod, g2, prefix, w_up, w_conv, b_conv, w_down, final_g)


def _rope_tables_t(pos):
    def tab(half, heads, pad):
        inv = ROPE_THETA ** (-jnp.arange(half, dtype=F32) / half)
        ang = inv[:, None] * pos.astype(F32)[None, :]
        cos, sin = jnp.cos(ang), jnp.sin(ang)
        cos_t = jnp.tile(jnp.concatenate([cos, cos], axis=0), (heads, 1))
        sin_t = jnp.tile(jnp.concatenate([-sin, sin], axis=0), (heads, 1))
        if pad:
            cos_t = jnp.concatenate([cos_t, jnp.ones((pad, pos.shape[0]), F32)], axis=0)
            sin_t = jnp.concatenate([sin_t, jnp.zeros((pad, pos.shape[0]), F32)], axis=0)
        return cos_t, sin_t

    cosa, sina = tab(HEAD_DIM // 2, N_HEADS, 0)
    cosi, sini = tab(IDX_DIM // 2, IDX_HEADS, 0)
    coss, sins = tab(IDX_DIM // 2, 1, SLAB - IDX_DIM)
    return cosa, sina, cosi, sini, coss, sins


def _layout_w_in(w_in_t, layer):
    sizes = (BRANCH, BRANCH, BRANCH, IDX_HEADS * IDX_DIM, IDX_DIM, IDX_HEADS, BRANCH, BRANCH,
             BRANCH, BRANCH, BRANCH, N_HEADS, BRANCH, BRANCH, BRANCH)
    offs = [0]
    for s in sizes:
        offs.append(offs[-1] + s)
    piece = lambda i: w_in_t[offs[i]:offs[i + 1], layer, :]
    a_q, a_k, a_v, a_qi, a_ki, a_w, b_u, b_v, c_q, c_k, c_v, c_f, d_q, d_k, d_v = (piece(i) for i in range(15))
    pad = jnp.zeros((SLAB - IDX_DIM - IDX_HEADS - N_HEADS, w_in_t.shape[-1]), w_in_t.dtype)
    return jnp.concatenate([a_q, a_k, a_v, a_qi, c_q, c_k, c_v, d_q, d_k, d_v, b_u, b_v,
                            a_ki, a_w, c_f, pad], axis=0).astype(BF16)


def _block_diag_tril_t(w_s, n, reps):
    tri = jnp.tril(jnp.ones((n, n), dtype=bool))
    w = jnp.where(tri[None], w_s[:, :n, :n], 0)
    eye = jnp.eye(reps, dtype=w.dtype)
    return jnp.einsum("ab,gts->gbsat", eye, w).reshape(w.shape[0], reps * n, reps * n).astype(BF16)


def _round_up(a, b):
    return (a + b - 1) // b * b


def _channel_major(cache):
    depth, bsz, length = cache.shape[:3]
    flat = cache.reshape(depth, bsz, length, -1)
    return jnp.swapaxes(flat, 2, 3)


def _layer(x, mod, lp, past, prefix, final_g, stacked, *, layer, depth, final_norm, past_len):
    bsz, t, d = x.shape
    is_prompt = past is None
    n = min(t, B_CHUNK)
    if is_prompt:
        nb, tt = 1, min(t, TOKEN_TILE)
        reps = tt // n
        pos_rows = jnp.arange(t)
    else:
        nb, tt = bsz, t
        reps = (nb * tt) // n
        pos_rows = jnp.tile(past_len + jnp.arange(t), nb)
    m = nb * tt
    tables = _rope_tables_t(pos_rows)
    ws = _block_diag_tril_t(lp["w_s"], n, reps)
    bs = jnp.tile(jnp.repeat(lp["b_s"][:, :n], HEAD_DIM, axis=0), (1, reps))
    fb = jnp.broadcast_to(lp["fb"], (SLAB, m))
    lng = jnp.broadcast_to(lp["lng"], (BRANCH, m))
    lnb = jnp.broadcast_to(lp["lnb"], (BRANCH, m))
    outs = _proj(x, mod, lp["g1"], lp["w_t"], tables, fb, lng, lnb, ws, bs,
                 nb=nb, tt=tt, stacked=stacked if is_prompt else None, layer=layer, depth=depth)
    kv, (qa, qc, qd, qi, sm, ob, vb) = outs[:6], outs[6:]

    l_valid = past_len + t
    tqp = _round_up(t, Q_PAD)
    tk, tk_d = min(ATT_TK, past_len + tqp), min(SB_TK, past_len + tqp)
    tq_a, tq_d = min(ATT_TQ, tqp), min(SB_TQ, tqp)
    if is_prompt:
        lpad = t
        n_sel = min(TOPK_MAX, t // 4)
        ka, va, kc, vc, kd, vd = ([a] for a in kv)
        ki_srcs, f_srcs, f_row = [sm], [sm], (SLAB_F,)
    else:
        lpad = _round_up(past_len + tqp, tk)
        n_sel = min(TOPK_MAX, l_valid // 4)

        def tokens(a):
            a = jnp.swapaxes(a.reshape(a.shape[0], bsz, t), 0, 1)
            return jnp.pad(a, ((0, 0), (0, 0), (0, tqp - t)))

        pk_a, pv_a, pki, pk_c, pv_c, plogf, pk_d, pv_d = past
        qa, qc, qd, qi, sm_tok = (tokens(a) for a in (qa, qc, qd, qi, sm))
        new = [tokens(a) for a in kv]
        ka, va, kc, vc, kd, vd = ([p_, n_] for p_, n_ in zip((pk_a, pv_a, pk_c, pv_c, pk_d, pv_d), new))
        ki_srcs, f_srcs, f_row = [pki, sm_tok], [plogf, sm_tok], (0, SLAB_F)
    sm_q = sm if is_prompt else sm_tok

    oa = _dsa(qa, qi, sm_q, ka, va, ki_srcs, layer=layer, lp=lpad, tq=tq_a, tk=tk, past_len=past_len,
              l_valid=l_valid, n_sel=n_sel, t_valid=t)[:, :t]
    oc = _fox(qc, kc, vc, f_srcs, f_row, layer=layer, lp=lpad, tq=tq_a, tk=tk, past_len=past_len)[:, :t]
    od = _sb(qd, kd, vd, layer=layer, lp=lpad, tq=tq_d, tk=tk_d, past_len=past_len)[:, :t]

    x = _merge(x, mod, lp["g1"], oa, ob, oc, od, lp["w_gate"], lp["b_gate"], lp["w_branch"], lp["w_out"],
               nb=nb, tt=tt)
    x, conv_state = _ffn(x, mod, lp["g2"], prefix, lp["w_up"], lp["w_conv"], lp["b_conv"], lp["w_down"],
                         final_g, nb=nb, tt=tt, final_norm=final_norm)
    if is_prompt:
        rows = (sm, conv_state)
    else:
        heads = lambda a: jnp.transpose(a.reshape(N_HEADS, HEAD_DIM, bsz, t), (2, 3, 0, 1))
        small = lambda a, r0, r1: jnp.transpose(a[r0:r1].reshape(r1 - r0, bsz, t), (1, 2, 0))
        ka_n, va_n, kc_n, vc_n, kd_n, vd_n = kv
        rows = (heads(ka_n), heads(va_n), small(sm, SLAB_KI, SLAB_KI + IDX_DIM), vb, heads(kc_n), heads(vc_n),
                small(sm, SLAB_F, SLAB_F + N_HEADS), heads(kd_n), heads(vd_n), conv_state)
    return x, rows, kv


def kernel(x_prompt, x_sample, cache_a_k, cache_a_v, cache_a_kidx, cache_c_k, cache_c_v, cache_c_logf, cache_d_k, cache_d_v, state_ffn_conv, c_prompt, c_sample, norm1_g, norm2_g, w_mod, b_mod, w_in, f_bias, lnv_g, lnv_b, w_spatial, b_spatial, w_branch, w_gate, b_gate, w_out, w_up, w_conv, b_conv, w_down, final_g):
    depth = w_in.shape[0]
    bsz, t_prompt, d = x_prompt.shape
    dbsz = x_sample.shape[0]
    past_len = cache_a_k.shape[2]
    mods = _modulation(jnp.concatenate([c_prompt, c_sample], axis=0), w_mod, b_mod)
    mods = mods.reshape(depth, bsz + dbsz, 6, d)
    prefix = jnp.zeros((bsz, CONV_W - 1, w_up.shape[-1]), x_prompt.dtype)
    fg = final_g.reshape(1, d)
    past = tuple(_channel_major(c) for c in (cache_a_k, cache_a_v, cache_a_kidx, cache_c_k, cache_c_v,
                                              cache_c_logf.astype(F32), cache_d_k, cache_d_v))
    w_in_t = jnp.transpose(w_in, (2, 0, 1))
    xp, xs = x_prompt, x_sample
    rows_p, rows_s = [], []
    stacked = ()
    for l in range(depth):
        fb = jnp.zeros((SLAB, 1), F32).at[SLAB_F:SLAB_F + N_HEADS, 0].set(f_bias[l].astype(F32))
        lp = dict(g1=norm1_g[l].reshape(1, d), g2=norm2_g[l].reshape(1, d), w_t=_layout_w_in(w_in_t, l), fb=fb,
                  lng=lnv_g[l].reshape(BRANCH, 1), lnb=lnv_b[l].reshape(BRANCH, 1),
                  w_s=w_spatial[l], b_s=b_spatial[l], w_branch=w_branch[l].astype(BF16),
                  w_gate=w_gate[l].astype(BF16), b_gate=b_gate[l].reshape(1, -1), w_out=w_out[l].astype(BF16),
                  w_up=w_up[l].astype(BF16), w_conv=w_conv[l], b_conv=b_conv[l].reshape(1, -1),
                  w_down=w_down[l].astype(BF16))
        last = l == depth - 1
        xp, new_p, stacked = _layer(xp, mods[l, :bsz], lp, None, prefix, fg, stacked, layer=l, depth=depth,
                                    final_norm=last, past_len=0)
        xs, new_s, _ = _layer(xs, mods[l, bsz:], lp, past, state_ffn_conv[l], fg, None, layer=l, depth=depth,
                              final_norm=last, past_len=past_len)
        rows_p.append(new_p)
        rows_s.append(new_s)

    def stacked_s(i):
        return jnp.stack([r[i] for r in rows_s], axis=0)

    def heads_p(a):
        return jnp.transpose(a.reshape(depth, bsz, N_HEADS, HEAD_DIM, t_prompt), (0, 1, 4, 2, 3))

    sm_p = jnp.stack([r[0] for r in rows_p], axis=0)
    small_p = lambda r0, r1: jnp.swapaxes(sm_p[:, :, r0:r1, :], 2, 3)
    ka_p, va_p, kc_p, vc_p, kd_p, vd_p = (heads_p(a) for a in stacked)
    conv_p = jnp.stack([r[1] for r in rows_p], axis=0)
    return (xp, xs,
            ka_p, stacked_s(0),
            va_p, stacked_s(1),
            small_p(SLAB_KI, SLAB_KI + IDX_DIM), stacked_s(2),
            stacked_s(3),
            kc_p, stacked_s(4),
            vc_p, stacked_s(5),
            small_p(SLAB_F, SLAB_F + N_HEADS), stacked_s(6),
            kd_p, stacked_s(7),
            vd_p, stacked_s(8),
            conv_p, stacked_s(9))
```
